```python
import math
import jax, jax.numpy as jnp
from jax import lax
import numpy as np

D_MODEL = 1024
BATCH = 4
SEQ = 8192
DEPTH = 4

N_MIXERS = 2
N_MLSTM_LAYERS = (DEPTH + 1) // 2
N_MLA_LAYERS = DEPTH // 2

MLSTM_HEADS = 8
MLSTM_DV = D_MODEL // MLSTM_HEADS
MLSTM_DQK = MLSTM_DV // 2
MLSTM_CHUNK = 64
MLSTM_IN = 2 * MLSTM_HEADS * MLSTM_DQK + 2 * MLSTM_HEADS * MLSTM_DV + 2 * MLSTM_HEADS

MLA_HEADS = 8
MLA_Q_LORA = 384
MLA_KV_LORA = 256
MLA_NOPE = 128
MLA_ROPE = 64
MLA_V = 128
MLA_IN = MLA_Q_LORA + MLA_KV_LORA + MLA_ROPE
ROPE_THETA = 10000.0
ATTN_QBLOCK = 128

N_EXPERTS = 32
TOP_K = 4
D_FF = D_MODEL
SWIGLU_LIMIT = 7.0
SWIGLU_ALPHA = 1.702
MOE_BLOCK = 128

DEEPNORM_ALPHA = (2.0 * DEPTH) ** 0.25
DEEPNORM_BETA = (8.0 * DEPTH) ** -0.25
LN_EPS = 1e-5
RMS_EPS = 1e-6

kernel_name = 'hybrid_mlstm_mla_moe_deepnorm'


def layer_norm(x, g, b):
    xf = x.astype(jnp.float32)
    mu = jnp.mean(xf, axis=-1, keepdims=True)
    var = jnp.mean(jnp.square(xf - mu), axis=-1, keepdims=True)
    return ((xf - mu) * lax.rsqrt(var + LN_EPS) * g + b).astype(x.dtype)


def rms_norm(x, g):
    xf = x.astype(jnp.float32)
    return (xf * lax.rsqrt(jnp.mean(jnp.square(xf), axis=-1, keepdims=True) + RMS_EPS) * g).astype(x.dtype)


def rope_tables(positions):
    inv_freq = ROPE_THETA ** (-jnp.arange(0, MLA_ROPE, 2, dtype=jnp.float32) / MLA_ROPE)
    ang = positions.astype(jnp.float32)[..., None] * inv_freq
    return jnp.cos(ang), jnp.sin(ang)


def apply_rope(x, cos, sin):
    x1, x2 = jnp.split(x, 2, axis=-1)
    return jnp.concatenate([x1 * cos - x2 * sin, x2 * cos + x1 * sin], axis=-1).astype(x.dtype)


def mlstm_cell(q, k, v, ig, lf):
    B, H, S, dk = q.shape
    dv = v.shape[-1]
    L = MLSTM_CHUNK
    nc = S // L
    q = q.reshape(B, H, nc, L, dk)
    k = k.reshape(B, H, nc, L, dk)
    v = v.reshape(B, H, nc, L, dv)
    ig = ig.reshape(B, H, nc, L)
    lf = lf.reshape(B, H, nc, L)
    b = jnp.cumsum(lf, axis=-1)
    g_end = b[..., -1:] - b + ig

    def step(carry, inp):
        C, n, m = carry
        k_c, v_c, g_c, bl_c = inp
        m_new = jnp.maximum(bl_c + m, jnp.max(g_c, axis=-1))
        decay = jnp.exp(bl_c + m - m_new)
        w = jnp.exp(g_c - m_new[..., None])
        C_new = decay[..., None, None] * C + jnp.einsum('bhlv,bhlk->bhvk', w[..., None] * v_c, k_c)
        n_new = decay[..., None] * n + jnp.einsum('bhl,bhlk->bhk', w, k_c)
        return (C_new, n_new, m_new), (C, n, m)

    init = (jnp.zeros((B, H, dv, dk), jnp.float32), jnp.zeros((B, H, dk), jnp.float32),
            jnp.zeros((B, H), jnp.float32))
    xs = (jnp.moveaxis(k, 2, 0), jnp.moveaxis(v, 2, 0), jnp.moveaxis(g_end, 2, 0),
          jnp.moveaxis(b[..., -1], 2, 0))
    _, (C_prev, n_prev, m_prev) = lax.scan(step, init, xs)
    C_prev = jnp.moveaxis(C_prev, 0, 2)
    n_prev = jnp.moveaxis(n_prev, 0, 2)
    m_prev = jnp.moveaxis(m_prev, 0, 2)

    causal = jnp.tril(jnp.ones((L, L), dtype=bool))
    D = jnp.where(causal, b[..., :, None] - b[..., None, :] + ig[..., None, :], -jnp.inf)
    inter = b + m_prev[..., None]
    m_row = jnp.maximum(inter, jnp.max(D, axis=-1))
    s = jnp.einsum('bhcqk,bhcsk->bhcqs', q, k) * jnp.exp(D - m_row[..., None])
    w_inter = jnp.exp(inter - m_row)
    num = (jnp.einsum('bhcqs,bhcsv->bhcqv', s, v)
           + w_inter[..., None] * jnp.einsum('bhcqk,bhcvk->bhcqv', q, C_prev))
    den = jnp.sum(s, axis=-1) + w_inter * jnp.einsum('bhcqk,bhck->bhcq', q, n_prev)
    h = num / jnp.maximum(jnp.abs(den), jnp.exp(-m_row))[..., None]
    return h.reshape(B, H, S, dv)


def mlstm_mixer(x, w_in, b_gates, norm_gain, w_out):
    B, S, _ = x.shape
    H, dk, dv = MLSTM_HEADS, MLSTM_DQK, MLSTM_DV
    proj = x @ w_in
    cuts = [H * dk, 2 * H * dk, 2 * H * dk + H * dv, 2 * H * dk + 2 * H * dv]
    q, k, v, o, g = jnp.split(proj, cuts, axis=-1)

    def heads(t, d):
        return t.reshape(B, S, H, d).transpose(0, 2, 1, 3).astype(jnp.float32)

    gates = (g + b_gates).astype(jnp.float32)
    ig = gates[..., :H].transpose(0, 2, 1)
    lf = jax.nn.log_sigmoid(gates[..., H:]).transpose(0, 2, 1)
    h = mlstm_cell(heads(q, dk) * (dk ** -0.5), heads(k, dk), heads(v, dv), ig, lf)
    h = rms_norm(h.transpose(0, 2, 1, 3), norm_gain.reshape(H, dv)).reshape(B, S, H * dv)
    h = (jax.nn.sigmoid(o.astype(jnp.float32)) * h).astype(x.dtype)
    return h @ w_out


def causal_block_attention(q_nope, q_rope, k_nope, k_rope, v):
    B, S, H, dn = q_nope.shape
    QB = ATTN_QBLOCK
    nqb = S // QB
    scale = (MLA_NOPE + MLA_ROPE) ** -0.5
    qn = q_nope.reshape(B, nqb, QB, H, dn).transpose(1, 0, 3, 2, 4)
    qr = q_rope.reshape(B, nqb, QB, H, MLA_ROPE).transpose(1, 0, 3, 2, 4)
    kn = k_nope.transpose(0, 2, 1, 3)
    vh = v.transpose(0, 2, 1, 3)
    key_pos = jnp.arange(S)

    def block(args):
        qn_b, qr_b, blk = args
        s = jnp.einsum('bhqd,bhkd->bhqk', qn_b, kn) + jnp.einsum('bhqr,bkr->bhqk', qr_b, k_rope)
        s = s.astype(jnp.float32) * scale
        q_pos = blk * QB + jnp.arange(QB)
        mask = key_pos[None, :] <= q_pos[:, None]
        p = jax.nn.softmax(jnp.where(mask, s, -jnp.inf), axis=-1)
        return jnp.einsum('bhqk,bhkv->bhqv', p.astype(vh.dtype), vh)

    o = lax.map(block, (qn, qr, jnp.arange(nqb)))
    return o.transpose(1, 0, 3, 2, 4).reshape(B, S, H, MLA_V)


def mla_mixer(x, cos, sin, w_in, q_norm, kv_norm, w_qb, w_kvb, w_out):
    B, S, _ = x.shape
    H = MLA_HEADS
    a = x @ w_in
    c_q, c_kv, k_r = jnp.split(a, [MLA_Q_LORA, MLA_Q_LORA + MLA_KV_LORA], axis=-1)
    c_q = rms_norm(c_q, q_norm)
    c_kv = rms_norm(c_kv, kv_norm)
    q = (c_q @ w_qb).reshape(B, S, H, MLA_NOPE + MLA_ROPE)
    kv = (c_kv @ w_kvb).reshape(B, S, H, MLA_NOPE + MLA_V)
    q_nope, q_rope = q[..., :MLA_NOPE], q[..., MLA_NOPE:]
    k_nope, v = kv[..., :MLA_NOPE], kv[..., MLA_NOPE:]
    q_rope = apply_rope(q_rope, cos[:, :, None, :], sin[:, :, None, :])
    k_rope = apply_rope(k_r, cos, sin)
    o = causal_block_attention(q_nope, q_rope, k_nope, k_rope, v)
    return o.reshape(B, S, H * MLA_V) @ w_out


def clamped_swiglu(gu):
    g = jnp.minimum(gu[..., ::2], SWIGLU_LIMIT)
    l = jnp.clip(gu[..., 1::2], -SWIGLU_LIMIT, SWIGLU_LIMIT)
    return (l + 1.0) * g * jax.nn.sigmoid(SWIGLU_ALPHA * g)


def moe(x, w_router, b_router, w_gu, b_gu, w_down, b_down):
    B, S, D = x.shape
    T = B * S
    A = T * TOP_K
    xt = x.reshape(T, D)
    logits = (xt @ w_router + b_router).astype(jnp.float32)
    top_val, top_idx = lax.top_k(logits, TOP_K)
    gate = jax.nn.softmax(top_val, axis=-1)
    e_flat = top_idx.reshape(-1)
    tok_flat = jnp.repeat(jnp.arange(T, dtype=jnp.int32), TOP_K)
    order = jnp.argsort(e_flat)
    e_sorted = e_flat[order]
    tok_sorted = tok_flat[order]
    gate_sorted = gate.reshape(-1)[order]
    counts = jnp.bincount(e_flat, length=N_EXPERTS)
    seg_start = jnp.cumsum(counts) - counts
    padded = (counts + MOE_BLOCK - 1) // MOE_BLOCK * MOE_BLOCK
    pad_end = jnp.cumsum(padded)
    pad_start = pad_end - padded
    dest = pad_start[e_sorted] + jnp.arange(A) - seg_start[e_sorted]
    n_blocks = -(-A // MOE_BLOCK) + N_EXPERTS
    P = n_blocks * MOE_BLOCK
    row_tok = jnp.full((P,), T, dtype=jnp.int32).at[dest].set(tok_sorted)
    xpad = jnp.concatenate([xt, jnp.zeros((1, D), xt.dtype)], axis=0)
    x_blocks = xpad[row_tok].reshape(n_blocks, MOE_BLOCK, D)
    block_expert = jnp.minimum(
        jnp.searchsorted(pad_end, jnp.arange(n_blocks) * MOE_BLOCK, side='right'), N_EXPERTS - 1)

    def expert_block(args):
        xb, e = args
        h = clamped_swiglu(xb @ w_gu[e] + b_gu[e])
        return h @ w_down[e] + b_down[e]

    y_blocks = lax.map(expert_block, (x_blocks, block_expert))
    y_sorted = y_blocks.reshape(P, D)[dest]
    y = jax.ops.segment_sum(y_sorted * gate_sorted[:, None].astype(y_sorted.dtype), tok_sorted,
                            num_segments=T)
    return y.reshape(B, S, D).astype(x.dtype)


def setup_inputs(seed: int = 0) -> dict:
    key = jax.random.key(seed)
    ks = jax.random.split(key, 24)

    def nrm(k, shape, scale):
        return jax.random.normal(k, shape, jnp.float32) * scale

    H = MLSTM_HEADS
    x = nrm(ks[0], (BATCH, SEQ, D_MODEL), 1.0)
    positions = (jax.random.randint(ks[1], (BATCH, 1), 0, 4096, dtype=jnp.int32)
                 + jnp.arange(SEQ, dtype=jnp.int32)[None, :])
    ln_gain = 1.0 + nrm(ks[2], (DEPTH, 2, D_MODEL), 0.02)
    ln_bias = nrm(ks[3], (DEPTH, 2, D_MODEL), 0.02)
    mlstm_w_in = nrm(ks[4], (N_MLSTM_LAYERS, D_MODEL, MLSTM_IN), D_MODEL ** -0.5)
    mlstm_b_gates = jnp.concatenate([nrm(ks[5], (N_MLSTM_LAYERS, H), 0.1),
                                     3.0 + nrm(ks[6], (N_MLSTM_LAYERS, H), 0.5)], axis=-1)
    mlstm_norm_gain = 1.0 + nrm(ks[7], (N_MLSTM_LAYERS, H * MLSTM_DV), 0.02)
    mlstm_w_out = nrm(ks[8], (N_MLSTM_LAYERS, H * MLSTM_DV, D_MODEL),
                      (H * MLSTM_DV) ** -0.5 * DEEPNORM_BETA)
    mla_w_in = nrm(ks[9], (N_MLA_LAYERS, D_MODEL, MLA_IN), D_MODEL ** -0.5)
    mla_q_norm = 1.0 + nrm(ks[10], (N_MLA_LAYERS, MLA_Q_LORA), 0.02)
    mla_kv_norm = 1.0 + nrm(ks[11], (N_MLA_LAYERS, MLA_KV_LORA), 0.02)
    mla_w_qb = nrm(ks[12], (N_MLA_LAYERS, MLA_Q_LORA, MLA_HEADS * (MLA_NOPE + MLA_ROPE)), MLA_Q_LORA ** -0.5)
    mla_w_kvb = nrm(ks[13], (N_MLA_LAYERS, MLA_KV_LORA, MLA_HEADS * (MLA_NOPE + MLA_V)), MLA_KV_LORA ** -0.5)
    mla_w_out = nrm(ks[14], (N_MLA_LAYERS, MLA_HEADS * MLA_V, D_MODEL),
                    (MLA_HEADS * MLA_V) ** -0.5 * DEEPNORM_BETA)
    moe_w_router = nrm(ks[15], (DEPTH, D_MODEL, N_EXPERTS), D_MODEL ** -0.5)
    moe_b_router = nrm(ks[16], (DEPTH, N_EXPERTS), 0.01)
    moe_w_gate_up = nrm(ks[17], (DEPTH, N_EXPERTS, D_MODEL, 2 * D_FF), D_MODEL ** -0.5)
    moe_b_gate_up = nrm(ks[18], (DEPTH, N_EXPERTS, 2 * D_FF), 0.02)
    moe_w_down = nrm(ks[19], (DEPTH, N_EXPERTS, D_FF, D_MODEL), D_FF ** -0.5 * DEEPNORM_BETA)
    moe_b_down = nrm(ks[20], (DEPTH, N_EXPERTS, D_MODEL), 0.02)
    return {'x': x, 'positions': positions, 'ln_gain': ln_gain, 'ln_bias': ln_bias,
            'mlstm_w_in': mlstm_w_in, 'mlstm_b_gates': mlstm_b_gates,
            'mlstm_norm_gain': mlstm_norm_gain, 'mlstm_w_out': mlstm_w_out,
            'mla_w_in': mla_w_in, 'mla_q_norm': mla_q_norm, 'mla_kv_norm': mla_kv_norm,
            'mla_w_qb': mla_w_qb, 'mla_w_kvb': mla_w_kvb, 'mla_w_out': mla_w_out,
            'moe_w_router': moe_w_router, 'moe_b_router': moe_b_router,
            'moe_w_gate_up': moe_w_gate_up, 'moe_b_gate_up': moe_b_gate_up,
            'moe_w_down': moe_w_down, 'moe_b_down': moe_b_down}


def reference(x, positions, ln_gain, ln_bias, mlstm_w_in, mlstm_b_gates, mlstm_norm_gain,
              mlstm_w_out, mla_w_in, mla_q_norm, mla_kv_norm, mla_w_qb, mla_w_kvb, mla_w_out,
              moe_w_router, moe_b_router, moe_w_gate_up, moe_b_gate_up, moe_w_down, moe_b_down):
    cos, sin = rope_tables(positions)
    for layer in range(DEPTH):
        j = layer // N_MIXERS
        if layer % N_MIXERS == 0:
            mix = mlstm_mixer(x, mlstm_w_in[j], mlstm_b_gates[j], mlstm_norm_gain[j], mlstm_w_out[j])
        else:
            mix = mla_mixer(x, cos, sin, mla_w_in[j], mla_q_norm[j], mla_kv_norm[j],
                            mla_w_qb[j], mla_w_kvb[j], mla_w_out[j])
        x = layer_norm(DEEPNORM_ALPHA * x + mix, ln_gain[layer, 0], ln_bias[layer, 0])
        ff = moe(x, moe_w_router[layer], moe_b_router[layer], moe_w_gate_up[layer],
                 moe_b_gate_up[layer], moe_w_down[layer], moe_b_down[layer])
        x = layer_norm(DEEPNORM_ALPHA * x + ff, ln_gain[layer, 1], ln_bias[layer, 1])
    return x
```

```python
import functools

import jax
import jax.numpy as jnp
from jax import lax
from jax.experimental import pallas as pl
from jax.experimental.pallas import tpu as pltpu
from jax.experimental.pallas import tpu_sc as plsc

F32 = jnp.float32
BF16 = jnp.bfloat16
I32 = jnp.int32

DEPTH = 4
MLSTM_HEADS = 8
MLSTM_DQK = 64
MLSTM_DV = 128
MLA_HEADS = 8
MLA_Q_LORA = 384
MLA_KV_LORA = 256
MLA_NOPE = 128
MLA_ROPE = 64
MLA_V = 128
ROPE_THETA = 10000.0
N_EXPERTS = 32
TOP_K = 4
SWIGLU_LIMIT = 7.0
SWIGLU_ALPHA = 1.702
DEEPNORM_ALPHA = (2.0 * DEPTH) ** 0.25
LN_EPS = 1e-5
RMS_EPS = 1e-6

LANES = 128
MXU_DIM = 256
VMEM_LIMIT_BYTES = 56 * 1024 * 1024

TOKEN_TILE = 512
MLSTM_SEQ_BLOCK = 1024
MLSTM_CHUNK = 256
ATTN_TILE = 512
EXPERT_TILE = 512
SC_WINDOW = 128
SC_COLS = 256

_NT = (((1,), (1,)), ((), ()))


def _params(*sem):
    return pltpu.CompilerParams(dimension_semantics=sem, vmem_limit_bytes=VMEM_LIMIT_BYTES)


def _layer_norm(z, g, b):
    mu = jnp.mean(z, axis=-1, keepdims=True)
    zc = z - mu
    var = jnp.mean(zc * zc, axis=-1, keepdims=True)
    return zc * lax.rsqrt(var + LN_EPS) * g + b


def _rms_norm(z, g):
    return z * lax.rsqrt(jnp.mean(z * z, axis=-1, keepdims=True) + RMS_EPS) * g


def _split3(a):
    hi = a.astype(BF16)
    r = a - hi.astype(F32)
    mid = r.astype(BF16)
    lo = (r - mid.astype(F32)).astype(BF16)
    return hi, mid, lo


def _mlstm_inproj_kernel(x_ref, wq_ref, wkt_ref, wv_ref, wo_ref, wg_ref, bg_ref,
                         q_ref, kt_ref, v_ref, o_ref, gc_ref, gr_ref):
    xb = x_ref[...].astype(BF16)
    dot = functools.partial(jnp.dot, preferred_element_type=F32)
    q_ref[...] = (dot(xb, wq_ref[...]) * (MLSTM_DQK ** -0.5)).astype(BF16)
    kt_ref[...] = lax.dot_general(wkt_ref[...], xb, _NT, preferred_element_type=F32).astype(BF16)
    v_ref[...] = dot(xb, wv_ref[...]).astype(BF16)
    o_ref[...] = dot(xb, wo_ref[...])
    z = dot(xb, wg_ref[...]) + bg_ref[...]
    lane = lax.broadcasted_iota(I32, z.shape, 1)
    log_sig = jnp.minimum(z, 0.0) - jnp.log1p(jnp.exp(-jnp.abs(z)))
    g = jnp.where(lane < MLSTM_HEADS, z, log_sig)
    gc_ref[...] = g
    gr_ref[...] = g.T[:2 * MLSTM_HEADS, :]


def _mlstm_cell_kernel(q_ref, kt_ref, v_ref, o_ref, gc_ref, gr_ref, ng_ref, out_ref,
                       c_ref, m_ref, *, chunk, n_chunks):
    H, dk, dv = MLSTM_HEADS, MLSTM_DQK, MLSTM_DV

    @pl.when(pl.program_id(1) == 0)
    def _():
        c_ref[...] = jnp.zeros_like(c_ref)
        m_ref[...] = jnp.zeros_like(m_ref)

    row = lax.broadcasted_iota(I32, (chunk, chunk), 0)
    col = lax.broadcasted_iota(I32, (chunk, chunk), 1)
    causal = col <= row
    tri_lower = causal.astype(BF16)
    tri_upper = (row <= col).astype(BF16)
    ones_col = (lax.broadcasted_iota(I32, (chunk, LANES), 1) == 0).astype(BF16)
    dot = functools.partial(jnp.dot, preferred_element_type=F32)

    def chunk_body(c, carry):
        r0 = pl.multiple_of(c * chunk, chunk)
        rows = pl.ds(r0, chunk)
        gc = gc_ref[rows, :]
        gr = gr_ref[:, rows]
        bc = sum(dot(tri_lower, p) for p in _split3(gc))
        br = sum(dot(p, tri_upper) for p in _split3(gr))
        for h in range(H):
            ig_c, b_c = gc[:, h:h + 1], bc[:, H + h:H + h + 1]
            ig_r, b_r = gr[h:h + 1, :], br[H + h:H + h + 1, :]
            b_last = b_c[chunk - 1:chunk, :]
            m_old = m_ref[h:h + 1, 0:1]
            qh = q_ref[rows, h * dk:(h + 1) * dk]
            kth = kt_ref[h * dk:(h + 1) * dk, rows]
            vh = v_ref[rows, h * dv:(h + 1) * dv]
            ct = c_ref[h]

            dmat = jnp.where(causal, b_c - b_r + ig_r, -jnp.inf)
            inter = b_c + m_old
            m_row = jnp.maximum(inter, jnp.max(dmat, axis=1, keepdims=True))
            s = dot(qh, kth) * jnp.exp(dmat - m_row)
            w_inter = jnp.exp(inter - m_row)
            qc = dot(qh, ct.astype(BF16))
            num = dot(s.astype(BF16), vh) + w_inter * qc[:, :dv]
            den = jnp.sum(s, axis=1, keepdims=True) + w_inter * qc[:, dv:dv + 1]
            hh = num * (1.0 / jnp.maximum(jnp.abs(den), jnp.exp(-m_row)))
            hn = _rms_norm(hh, ng_ref[:, h * dv:(h + 1) * dv])
            og = o_ref[rows, h * dv:(h + 1) * dv]
            out_ref[rows, h * dv:(h + 1) * dv] = (jax.nn.sigmoid(og) * hn).astype(out_ref.dtype)

            m_new = jnp.maximum(b_last + m_old,
                                jnp.max(b_last - b_r + ig_r, axis=1, keepdims=True))
            decay = jnp.exp(b_last + m_old - m_new)
            w_r = jnp.exp(b_last - b_r + ig_r - m_new)
            ktw = (kth.astype(F32) * w_r).astype(BF16)
            vaug = jnp.concatenate([vh, ones_col], axis=1)
            c_ref[h] = decay * ct + dot(ktw, vaug)
            m_ref[h:h + 1, :] = jnp.broadcast_to(m_new, (1, LANES))
        return carry

    lax.fori_loop(0, n_chunks, chunk_body, 0)


def _mlstm_mixer(x2d, batch, w_in, b_gates, norm_gain):
    T, D = x2d.shape
    H, dk, dv = MLSTM_HEADS, MLSTM_DQK, MLSTM_DV
    seq = T // batch
    tm = TOKEN_TILE
    cq, ck, cv, co = H * dk, 2 * H * dk, 2 * H * dk + H * dv, 2 * H * dk + 2 * H * dv
    wq = w_in[:, :cq].astype(BF16)
    wkt = w_in[:, cq:ck].T.astype(BF16)
    wv = w_in[:, ck:cv].astype(BF16)
    wo = w_in[:, cv:co].astype(BF16)
    wg = jnp.pad(w_in[:, co:], ((0, 0), (0, LANES - 2 * H))).astype(BF16)
    bg = jnp.pad(b_gates, (0, LANES - 2 * H)).reshape(1, LANES)

    full = lambda a: pl.BlockSpec(a.shape, lambda i: (0,) * a.ndim)
    q, kt, v, o, gc, gr = pl.pallas_call(
        _mlstm_inproj_kernel,
        grid=(T // tm,),
        in_specs=[pl.BlockSpec((tm, D), lambda i: (i, 0)), full(wq), full(wkt), full(wv), full(wo),
                  full(wg), full(bg)],
        out_specs=[pl.BlockSpec((tm, H * dk), lambda i: (i, 0)),
                   pl.BlockSpec((H * dk, tm), lambda i: (0, i)),
                   pl.BlockSpec((tm, H * dv), lambda i: (i, 0)),
                   pl.BlockSpec((tm, H * dv), lambda i: (i, 0)),
                   pl.BlockSpec((tm, LANES), lambda i: (i, 0)),
                   pl.BlockSpec((2 * H, tm), lambda i: (0, i))],
        out_shape=[jax.ShapeDtypeStruct((T, H * dk), BF16),
                   jax.ShapeDtypeStruct((H * dk, T), BF16),
                   jax.ShapeDtypeStruct((T, H * dv), BF16),
                   jax.ShapeDtypeStruct((T, H * dv), F32),
                   jax.ShapeDtypeStruct((T, LANES), F32),
                   jax.ShapeDtypeStruct((2 * H, T), F32)],
        compiler_params=_params("parallel"),
        name="mlstm_inproj",
    )(x2d, wq, wkt, wv, wo, wg, bg)

    ts = min(MLSTM_SEQ_BLOCK, seq)
    chunk = min(MLSTM_CHUNK, ts)
    nsb = seq // ts
    ng = norm_gain.reshape(1, H * dv)
    return pl.pallas_call(
        functools.partial(_mlstm_cell_kernel, chunk=chunk, n_chunks=ts // chunk),
        grid=(batch, nsb),
        in_specs=[pl.BlockSpec((ts, H * dk), lambda b, s: (b * nsb + s, 0)),
                  pl.BlockSpec((H * dk, ts), lambda b, s: (0, b * nsb + s)),
                  pl.BlockSpec((ts, H * dv), lambda b, s: (b * nsb + s, 0)),
                  pl.BlockSpec((ts, H * dv), lambda b, s: (b * nsb + s, 0)),
                  pl.BlockSpec((ts, LANES), lambda b, s: (b * nsb + s, 0)),
                  pl.BlockSpec((2 * H, ts), lambda b, s: (0, b * nsb + s)),
                  pl.BlockSpec((1, H * dv), lambda b, s: (0, 0))],
        out_specs=pl.BlockSpec((ts, H * dv), lambda b, s: (b * nsb + s, 0)),
        out_shape=jax.ShapeDtypeStruct((T, H * dv), BF16),
        scratch_shapes=[pltpu.VMEM((H, dk, 2 * dv), F32), pltpu.VMEM((H, LANES), F32)],
        compiler_params=_params("parallel", "arbitrary"),
        name="mlstm_cell",
    )(q, kt, v, o, gc, gr, ng)


def _mla_proj_kernel(x_ref, cos_ref, sin_ref, wcq_ref, wckv_ref, wkr_ref, qn_ref, kvn_ref,
                     wq_ref, wqr_ref, wkn_ref, wvt_ref, q_ref, k_ref, vt_ref):
    H, dn = MLA_HEADS, MLA_NOPE
    dot = functools.partial(jnp.dot, preferred_element_type=F32)
    xb = x_ref[...].astype(BF16)
    cos, sin = cos_ref[...], sin_ref[...]
    c_q = _rms_norm(dot(xb, wcq_ref[...]), qn_ref[...]).astype(BF16)
    c_kv = _rms_norm(dot(xb, wckv_ref[...]), kvn_ref[...]).astype(BF16)
    kr2 = dot(xb, wkr_ref[...])
    kr = (kr2[:, :LANES] * cos + kr2[:, LANES:] * sin).astype(BF16)
    scale = (MLA_NOPE + MLA_ROPE) ** -0.5
    qa = dot(c_q, wq_ref[...])
    qr = dot(c_q, wqr_ref[...])
    kn = dot(c_kv, wkn_ref[...])
    for h in range(H):
        base = h * 2 * LANES
        q_ref[h, :, :dn] = (qa[:, base:base + dn] * scale).astype(BF16)
        rope = qa[:, base + dn:base + 2 * LANES] * cos + qr[:, h * LANES:(h + 1) * LANES] * sin
        q_ref[h, :, dn:] = (rope * scale).astype(BF16)
        k_ref[h, :, :dn] = kn[:, h * dn:(h + 1) * dn].astype(BF16)
        k_ref[h, :, dn:] = kr
    vt_ref[...] = lax.dot_general(wvt_ref[...], c_kv, _NT, preferred_element_type=F32).astype(BF16)


def _attn_kernel(q_ref, k_ref, vt_ref, o_ref, m_ref, l_ref, acc_ref, *, tile):
    qi = pl.program_id(2)
    q = q_ref[0]
    m_ref[...] = jnp.full_like(m_ref, -jnp.inf)
    l_ref[...] = jnp.zeros_like(l_ref)
    acc_ref[...] = jnp.zeros_like(acc_ref)

    def step(j, masked):
        k = k_ref[0, pl.ds(pl.multiple_of(j * tile, tile), tile), :]
        st = lax.dot_general(k, q, _NT, preferred_element_type=F32)
        if masked:
            kpos = lax.broadcasted_iota(I32, st.shape, 0)
            qpos = lax.broadcasted_iota(I32, st.shape, 1)
            st = jnp.where(kpos <= qpos, st, -jnp.inf)
        m_old = m_ref[...]
        m_new = jnp.maximum(m_old, jnp.max(st, axis=0, keepdims=True))
        p = jnp.exp(st - m_new)
        alpha = jnp.exp(m_old - m_new)
        l_ref[...] = alpha * l_ref[...] + jnp.sum(p, axis=0, keepdims=True)
        vt = vt_ref[:, pl.ds(pl.multiple_of(j * tile, tile), tile)]
        acc_ref[...] = alpha * acc_ref[...] + jnp.dot(vt, p.astype(BF16),
                                                      preferred_element_type=F32)
        m_ref[...] = m_new

    def body(j, carry):
        step(j, masked=False)
        return carry

    lax.fori_loop(0, qi, body, 0)
    step(qi, masked=True)
    o_ref[...] = (acc_ref[...] * (1.0 / l_ref[...])).T.astype(o_ref.dtype)


def _mla_mixer(x2d, batch, cos_t, sin_t, w_in, q_norm, kv_norm, w_qb, w_kvb):
    T, D = x2d.shape
    H, dn, dr, dv = MLA_HEADS, MLA_NOPE, MLA_ROPE, MLA_V
    seq = T // batch
    tm = TOKEN_TILE
    ql, kl = MLA_Q_LORA, MLA_KV_LORA
    half = dr // 2

    def rot(w):
        return jnp.concatenate([-w[..., half:], w[..., :half]], axis=-1)

    wcq = w_in[:, :ql].astype(BF16)
    wckv = w_in[:, ql:ql + kl].astype(BF16)
    wr = w_in[:, ql + kl:]
    zr = jnp.zeros((D, LANES - dr), F32)
    wkr = jnp.concatenate([wr, zr, rot(wr), zr], axis=1).astype(BF16)
    wq3 = w_qb.reshape(ql, H, dn + dr)
    zq = jnp.zeros((ql, H, LANES - dr), F32)
    wq = jnp.concatenate([wq3, zq], axis=2).reshape(ql, H * 2 * LANES).astype(BF16)
    wqr = jnp.concatenate([rot(wq3[:, :, dn:]), zq], axis=2).reshape(ql, H * LANES).astype(BF16)
    wkv3 = w_kvb.reshape(kl, H, dn + dv)
    wkn = wkv3[:, :, :dn].reshape(kl, H * dn).astype(BF16)
    wvt = wkv3[:, :, dn:].reshape(kl, H * dv).T.astype(BF16)
    qn = q_norm.reshape(1, ql)
    kvn = kv_norm.reshape(1, kl)

    full = lambda a: pl.BlockSpec(a.shape, lambda i: (0,) * a.ndim)
    q, k, vt = pl.pallas_call(
        _mla_proj_kernel,
        grid=(T // tm,),
        in_specs=[pl.BlockSpec((tm, D), lambda i: (i, 0)),
                  pl.BlockSpec((tm, LANES), lambda i: (i, 0)),
                  pl.BlockSpec((tm, LANES), lambda i: (i, 0)),
                  full(wcq), full(wckv), full(wkr), full(qn), full(kvn),
                  full(wq), full(wqr), full(wkn), full(wvt)],
        out_specs=[pl.BlockSpec((H, tm, 2 * LANES), lambda i: (0, i, 0)),
                   pl.BlockSpec((H, tm, 2 * LANES), lambda i: (0, i, 0)),
                   pl.BlockSpec((H * dv, tm), lambda i: (0, i))],
        out_shape=[jax.ShapeDtypeStruct((H, T, 2 * LANES), BF16),
                   jax.ShapeDtypeStruct((H, T, 2 * LANES), BF16),
                   jax.ShapeDtypeStruct((H * dv, T), BF16)],
        compiler_params=_params("parallel"),
        name="mla_proj",
    )(x2d, cos_t, sin_t, wcq, wckv, wkr, qn, kvn, wq, wqr, wkn, wvt)

    tile = min(ATTN_TILE, seq)
    nq = seq // tile
    return pl.pallas_call(
        functools.partial(_attn_kernel, tile=tile),
        grid=(batch, H, nq),
        in_specs=[pl.BlockSpec((1, tile, 2 * LANES), lambda b, h, i: (h, b * nq + i, 0)),
                  pl.BlockSpec((1, seq, 2 * LANES), lambda b, h, i: (h, b, 0)),
                  pl.BlockSpec((dv, seq), lambda b, h, i: (h, b))],
        out_specs=pl.BlockSpec((tile, dv), lambda b, h, i: (b * nq + i, h)),
        out_shape=jax.ShapeDtypeStruct((T, H * dv), BF16),
        scratch_shapes=[pltpu.VMEM((1, tile), F32), pltpu.VMEM((1, tile), F32),
                        pltpu.VMEM((dv, tile), F32)],
        compiler_params=_params("parallel", "parallel", "arbitrary"),
        name="mla_attention",
    )(q, k, vt)


def _post_kernel(h_ref, x_ref, w_ref, g_ref, b_ref, wr_ref, br_ref,
                 xo_ref, idx_ref, gate_ref, pos_ref, cnt_ref, run_ref):
    E = N_EXPERTS
    tm = x_ref.shape[0]

    @pl.when(pl.program_id(0) == 0)
    def _():
        run_ref[...] = jnp.zeros_like(run_ref)

    mix = jnp.dot(h_ref[...], w_ref[...], preferred_element_type=F32)
    x1 = _layer_norm(DEEPNORM_ALPHA * x_ref[...] + mix, g_ref[...], b_ref[...])
    xo_ref[...] = x1

    xh = x1.astype(BF16)
    xl = (x1 - xh.astype(F32)).astype(BF16)
    wh, wl = wr_ref[0], wr_ref[1]
    ntdot = lambda a, b: lax.dot_general(a, b, _NT, preferred_element_type=F32)
    logits = ntdot(wh, xh) + ntdot(wh, xl) + ntdot(wl, xh) + br_ref[...]

    e_iota = lax.broadcasted_iota(I32, (E, tm), 0)
    rest = logits
    vals, sels = [], []
    for k in range(TOP_K):
        v = jnp.max(rest, axis=0, keepdims=True)
        ik = jnp.min(jnp.where(rest == v, e_iota, E), axis=0, keepdims=True)
        sel = e_iota == ik
        rest = jnp.where(sel, -jnp.inf, rest)
        vals.append(v)
        sels.append(sel)
        idx_ref[k:k + 1, :] = ik
    ex = [jnp.exp(v - vals[0]) for v in vals]
    inv = 1.0 / sum(ex)
    for k in range(TOP_K):
        gate_ref[k:k + 1, :] = ex[k] * inv

    chosen = functools.reduce(jnp.logical_or, sels)
    onehot = chosen.astype(BF16)
    r = lax.broadcasted_iota(I32, (tm, tm), 0)
    c = lax.broadcasted_iota(I32, (tm, tm), 1)
    before = jnp.dot(onehot, (r < c).astype(BF16), preferred_element_type=F32)
    run = run_ref[:, 0:1]
    rank = before + run
    for k in range(TOP_K):
        pos_ref[k:k + 1, :] = jnp.sum(jnp.where(sels[k], rank, 0.0), axis=0,
                                      keepdims=True).astype(I32)
    run_new = run + jnp.sum(chosen.astype(F32), axis=1, keepdims=True)
    run_ref[...] = jnp.broadcast_to(run_new, run_ref.shape)
    cnt_ref[...] = jnp.broadcast_to(run_new, cnt_ref.shape)


def _post_mixer(h, x2d, w_out, ln_g, ln_b, w_router, b_router):
    T, D = x2d.shape
    E = N_EXPERTS
    tm = TOKEN_TILE
    wo = w_out.astype(BF16)
    wrt = w_router.T
    wrh = wrt.astype(BF16)
    wr = jnp.stack([wrh, (wrt - wrh.astype(F32)).astype(BF16)])
    full = lambda a: pl.BlockSpec(a.shape, lambda i: (0,) * a.ndim)
    g, b, br = ln_g.reshape(1, D), ln_b.reshape(1, D), b_router.reshape(E, 1)
    rows = TOP_K
    return pl.pallas_call(
        _post_kernel,
        grid=(T // tm,),
        in_specs=[pl.BlockSpec((tm, h.shape[1]), lambda i: (i, 0)),
                  pl.BlockSpec((tm, D), lambda i: (i, 0)),
                  full(wo), full(g), full(b), full(wr), full(br)],
        out_specs=[pl.BlockSpec((tm, D), lambda i: (i, 0)),
                   pl.BlockSpec((rows, tm), lambda i: (0, i)),
                   pl.BlockSpec((rows, tm), lambda i: (0, i)),
                   pl.BlockSpec((rows, tm), lambda i: (0, i)),
                   pl.BlockSpec((E, LANES), lambda i: (0, 0))],
        out_shape=[jax.ShapeDtypeStruct((T, D), F32),
                   jax.ShapeDtypeStruct((rows, T), I32),
                   jax.ShapeDtypeStruct((rows, T), F32),
                   jax.ShapeDtypeStruct((rows, T), I32),
                   jax.ShapeDtypeStruct((E, LANES), F32)],
        scratch_shapes=[pltpu.VMEM((E, LANES), F32)],
        compiler_params=_params("arbitrary"),
        name="post_mixer_router",
    )(h, x2d, wo, g, b, wr, br)


def _sc_mesh():
    return plsc.VectorSubcoreMesh(core_axis_name="core", subcore_axis_name="subcore")


def _sub_rows(a):
    R, D = a.shape
    return a.reshape(R * (D // SC_COLS), SC_COLS)


def _sub_indices(idx, D):
    per = D // SC_COLS
    return (idx[..., None] * per + jnp.arange(per, dtype=I32)).reshape(*idx.shape[:-1], -1)


def _sc_scatter_rows(x, dest, n_rows):
    T, D = x.shape
    K = dest.shape[0]
    W = SC_WINDOW
    xs = _sub_rows(x)
    nb = xs.shape[0] // W
    idx = _sub_indices(dest, D).reshape(1, K * xs.shape[0])

    @functools.partial(pl.kernel,
                       out_type=jax.ShapeDtypeStruct((n_rows * (D // SC_COLS), SC_COLS), x.dtype),
                       mesh=_sc_mesh(), scratch_types=[], name="moe_dispatch_scatter")
    def scatter(x_hbm, i_hbm, o_hbm):
        def body(x_vmem, i_vmem):
            pltpu.sync_copy(x_vmem, o_hbm.at[i_vmem.at[0]])

        pltpu.emit_pipeline(
            body,
            grid=(K * nb,),
            in_specs=[pl.BlockSpec((W, SC_COLS), lambda g: (g % nb, 0)),
                      pl.BlockSpec((1, W), lambda g: (0, g))],
            out_specs=[],
            core_axis_name=("core", "subcore"),
            dimension_semantics=(pltpu.PARALLEL,),
        )(x_hbm, i_hbm)

    return scatter(xs, idx).reshape(n_rows, D)


def _sc_gather_rows(table, idx):
    N = idx.shape[0]
    D = table.shape[1]
    W = SC_WINDOW
    ts = _sub_rows(table)
    idx2 = _sub_indices(idx, D).reshape(1, -1)
    n_sub = idx2.shape[1]

    @functools.partial(pl.kernel, out_type=jax.ShapeDtypeStruct((n_sub, SC_COLS), table.dtype),
                       mesh=_sc_mesh(), scratch_types=[], name="moe_combine_gather")
    def gather(t_hbm, i_hbm, o_hbm):
        def body(i_vmem, o_vmem):
            pltpu.sync_copy(t_hbm.at[i_vmem.at[0]], o_vmem)

        pltpu.emit_pipeline(
            body,
            grid=(n_sub // W,),
            in_specs=[pl.BlockSpec((1, W), lambda g: (0, g))],
            out_specs=[pl.BlockSpec((W, SC_COLS), lambda g: (g, 0))],
            core_axis_name=("core", "subcore"),
            dimension_semantics=(pltpu.PARALLEL,),
        )(i_hbm, o_hbm)

    return gather(ts, idx2).reshape(N, D)


def _expert_kernel(te_ref, nu_ref, x_ref, wgu_ref, bg_ref, bl_ref, wd_ref, bd_ref, perm_ref,
                   y_ref, wg_s, wl_s, wd_s):
    i = pl.program_id(0)
    active = i < nu_ref[0]
    e = te_ref[i]
    changed = jnp.logical_or(i == 0, e != te_ref[jnp.maximum(i - 1, 0)])
    dot = functools.partial(jnp.dot, preferred_element_type=F32)

    @pl.when(jnp.logical_and(active, changed))
    def _():
        n_blocks = wgu_ref.shape[2] // MXU_DIM
        for blk in range(n_blocks):
            wb = wgu_ref[0, :, blk * MXU_DIM:(blk + 1) * MXU_DIM].astype(BF16)
            wp = dot(wb, perm_ref[...]).astype(BF16)
            wg_s[:, blk * LANES:(blk + 1) * LANES] = wp[:, :LANES]
            wl_s[:, blk * LANES:(blk + 1) * LANES] = wp[:, LANES:]
        wd_s[...] = wd_ref[0].astype(BF16)

    @pl.when(active)
    def _():
        xb = x_ref[...].astype(BF16)
        g = jnp.minimum(dot(xb, wg_s[...]) + bg_ref[0], SWIGLU_LIMIT)
        lin = jnp.clip(dot(xb, wl_s[...]) + bl_ref[0], -SWIGLU_LIMIT, SWIGLU_LIMIT)
        act = (lin + 1.0) * g * jax.nn.sigmoid(SWIGLU_ALPHA * g)
        y_ref[...] = dot(act.astype(BF16), wd_s[...]) + bd_ref[0]


def _combine_kernel(y0_ref, y1_ref, y2_ref, y3_ref, gate_ref, x_ref, g_ref, b_ref, o_ref):
    gate = gate_ref[...]
    ff = (gate[:, 0:1] * y0_ref[...] + gate[:, 1:2] * y1_ref[...]
          + gate[:, 2:3] * y2_ref[...] + gate[:, 3:4] * y3_ref[...])
    o_ref[...] = _layer_norm(DEEPNORM_ALPHA * x_ref[...] + ff, g_ref[...], b_ref[...])


def _moe_and_norm(x1, idx, gate, pos, cnt, w_gu, b_gu, w_down, b_down, ln_g, ln_b):
    T, D = x1.shape
    E, K = N_EXPERTS, TOP_K
    dff = w_down.shape[1]
    te = EXPERT_TILE
    n_tiles = (T * K) // te + E
    n_rows = n_tiles * te

    counts = cnt[:, 0].astype(I32)
    padded = (counts + te - 1) // te * te
    pad_end = jnp.cumsum(padded)
    pad_start = pad_end - padded
    dest = pad_start[idx[:K]] + pos[:K]
    tile_expert = jnp.minimum(
        jnp.searchsorted(pad_end, jnp.arange(n_tiles, dtype=I32) * te, side="right"), E - 1
    ).astype(I32)
    n_used = (pad_end[-1] // te).astype(I32).reshape(1)

    x_rows = _sc_scatter_rows(x1, dest, n_rows)

    half = MXU_DIM // 2
    src = jnp.arange(MXU_DIM)
    perm = (src[:, None] == jnp.where(src < half, 2 * src, 2 * (src - half) + 1)[None, :])
    perm = perm.astype(BF16)
    bg = b_gu[:, 0::2].reshape(E, 1, dff)
    bl = b_gu[:, 1::2].reshape(E, 1, dff)
    bd = b_down.reshape(E, 1, D)

    def row_map(i, te_ref, nu_ref):
        return (jnp.minimum(i, nu_ref[0] - 1), 0)

    def exp_map(i, te_ref, nu_ref):
        return (te_ref[i], 0, 0)

    y_rows = pl.pallas_call(
        _expert_kernel,
        grid_spec=pltpu.PrefetchScalarGridSpec(
            num_scalar_prefetch=2,
            grid=(n_tiles,),
            in_specs=[pl.BlockSpec((te, D), row_map),
                      pl.BlockSpec((1, D, 2 * dff), exp_map),
                      pl.BlockSpec((1, 1, dff), exp_map),
                      pl.BlockSpec((1, 1, dff), exp_map),
                      pl.BlockSpec((1, dff, D), exp_map),
                      pl.BlockSpec((1, 1, D), exp_map),
                      pl.BlockSpec((MXU_DIM, MXU_DIM), lambda i, a, b: (0, 0))],
            out_specs=pl.BlockSpec((te, D), row_map),
            scratch_shapes=[pltpu.VMEM((D, dff), BF16), pltpu.VMEM((D, dff), BF16),
                            pltpu.VMEM((dff, D), BF16)],
        ),
        out_shape=jax.ShapeDtypeStruct((n_rows, D), F32),
        compiler_params=_params("arbitrary"),
        name="moe_experts",
    )(tile_expert, n_used, x_rows, w_gu, bg, bl, w_down, bd, perm)

    y_tok = _sc_gather_rows(y_rows, dest.reshape(K * T))

    tm = TOKEN_TILE
    nt = T // tm
    gate_col = gate[:K].T
    slot = lambda k: pl.BlockSpec((tm, D), lambda i, k=k: (k * nt + i, 0))
    vec = pl.BlockSpec((1, D), lambda i: (0, 0))
    return pl.pallas_call(
        _combine_kernel,
        grid=(nt,),
        in_specs=[slot(0), slot(1), slot(2), slot(3),
                  pl.BlockSpec((tm, K), lambda i: (i, 0)),
                  pl.BlockSpec((tm, D), lambda i: (i, 0)), vec, vec],
        out_specs=pl.BlockSpec((tm, D), lambda i: (i, 0)),
        out_shape=jax.ShapeDtypeStruct((T, D), F32),
        compiler_params=_params("parallel"),
        name="moe_combine_norm",
    )(y_tok, y_tok, y_tok, y_tok, gate_col, x1, ln_g.reshape(1, D), ln_b.reshape(1, D))


def kernel(x, positions, ln_gain, ln_bias, mlstm_w_in, mlstm_b_gates, mlstm_norm_gain,
           mlstm_w_out, mla_w_in, mla_q_norm, mla_kv_norm, mla_w_qb, mla_w_kvb, mla_w_out,
           moe_w_router, moe_b_router, moe_w_gate_up, moe_b_gate_up, moe_w_down, moe_b_down):
    B, S, D = x.shape
    T = B * S
    x2d = x.reshape(T, D)

    inv_freq = ROPE_THETA ** (-jnp.arange(0, MLA_ROPE, 2, dtype=F32) / MLA_ROPE)
    ang = positions.astype(F32).reshape(T, 1) * inv_freq
    reps = LANES // ang.shape[1]
    cos_t = jnp.tile(jnp.cos(ang), (1, reps))
    sin_t = jnp.tile(jnp.sin(ang), (1, reps))

    for layer in range(DEPTH):
        j = layer // 2
        if layer % 2 == 0:
            h = _mlstm_mixer(x2d, B, mlstm_w_in[j], mlstm_b_gates[j], mlstm_norm_gain[j])
            w_out = mlstm_w_out[j]
        else:
            h = _mla_mixer(x2d, B, cos_t, sin_t, mla_w_in[j], mla_q_norm[j], mla_kv_norm[j],
                           mla_w_qb[j], mla_w_kvb[j])
            w_out = mla_w_out[j]
        x1, idx, gate, pos, cnt = _post_mixer(h, x2d, w_out, ln_gain[layer, 0], ln_bias[layer, 0],
                                              moe_w_router[layer], moe_b_router[layer])
        x2d = _moe_and_norm(x1, idx, gate, pos, cnt, moe_w_gate_up[layer], moe_b_gate_up[layer],
                            moe_w_down[layer], moe_b_down[layer],
                            ln_gain[layer, 1], ln_bias[layer, 1])
    return x2d.reshape(B, S, D)
```

```python
import functools

import jax
import jax.numpy as jnp
from jax import lax
from jax.experimental import pallas as pl
from jax.experimental.pallas import tpu as pltpu
from jax.experimental.pallas import tpu_sc as plsc

F32 = jnp.float32
BF16 = jnp.bfloat16
I32 = jnp.int32

DEPTH = 4
MLSTM_HEADS = 8
MLSTM_DQK = 64
MLSTM_DV = 128
MLA_HEADS = 8
MLA_Q_LORA = 384
MLA_KV_LORA = 256
MLA_NOPE = 128
MLA_ROPE = 64
MLA_V = 128
ROPE_THETA = 10000.0
N_EXPERTS = 32
TOP_K = 4
SWIGLU_LIMIT = 7.0
SWIGLU_ALPHA = 1.702
DEEPNORM_ALPHA = (2.0 * DEPTH) ** 0.25
LN_EPS = 1e-5
LOG2_E = 1.4426950408889634
RMS_EPS = 1e-6

LANES = 128
MXU_DIM = 256
VMEM_LIMIT_BYTES = 56 * 1024 * 1024

TOKEN_TILE = 512
MLSTM_SEQ_BLOCK = 1024
MLSTM_CHUNK = 256
ATTN_TILE = 512
ATTN_HEADS_PER_STEP = 2
EXPERT_TILE = 512
SC_WINDOW = 128
SC_COLS = 256

_NT = (((1,), (1,)), ((), ()))


def _params(*sem):
    return pltpu.CompilerParams(dimension_semantics=sem, vmem_limit_bytes=VMEM_LIMIT_BYTES)


def _layer_norm(z, g, b):
    mu = jnp.mean(z, axis=-1, keepdims=True)
    zc = z - mu
    var = jnp.mean(zc * zc, axis=-1, keepdims=True)
    return zc * lax.rsqrt(var + LN_EPS) * g + b


def _rms_norm(z, g):
    return z * lax.rsqrt(jnp.mean(z * z, axis=-1, keepdims=True) + RMS_EPS) * g


def _split3(a):
    hi = a.astype(BF16)
    r = a - hi.astype(F32)
    mid = r.astype(BF16)
    lo = (r - mid.astype(F32)).astype(BF16)
    return hi, mid, lo


def _mlstm_inproj_kernel(x_ref, wq_ref, wkt_ref, wv_ref, wo_ref, wg_ref, bg_ref,
                         q_ref, kt_ref, v_ref, o_ref, gc_ref, gr_ref):
    xb = x_ref[...].astype(BF16)
    dot = functools.partial(jnp.dot, preferred_element_type=F32)
    q_ref[...] = (dot(xb, wq_ref[...]) * (MLSTM_DQK ** -0.5)).astype(BF16)
    kt_ref[...] = lax.dot_general(wkt_ref[...], xb, _NT, preferred_element_type=F32).astype(BF16)
    v_ref[...] = dot(xb, wv_ref[...]).astype(BF16)
    o_ref[...] = dot(xb, wo_ref[...])
    z = dot(xb, wg_ref[...]) + bg_ref[...]
    lane = lax.broadcasted_iota(I32, z.shape, 1)
    log_sig = jnp.minimum(z, 0.0) - jnp.log1p(jnp.exp(-jnp.abs(z)))
    g = jnp.where(lane < MLSTM_HEADS, z, log_sig)
    gc_ref[...] = g
    gr_ref[...] = g.T[:2 * MLSTM_HEADS, :]


def _mlstm_cell_kernel(q_ref, kt_ref, v_ref, o_ref, gc_ref, gr_ref, ng_ref, out_ref,
                       c_ref, m_ref, *, chunk, n_chunks):
    H, dk, dv = MLSTM_HEADS, MLSTM_DQK, MLSTM_DV

    @pl.when(pl.program_id(1) == 0)
    def _():
        c_ref[...] = jnp.zeros_like(c_ref)
        m_ref[...] = jnp.zeros_like(m_ref)

    row = lax.broadcasted_iota(I32, (chunk, chunk), 0)
    col = lax.broadcasted_iota(I32, (chunk, chunk), 1)
    causal = col <= row
    tri_lower = causal.astype(BF16)
    tri_upper = (row <= col).astype(BF16)
    ones_col = (lax.broadcasted_iota(I32, (chunk, LANES), 1) == 0).astype(BF16)
    dot = functools.partial(jnp.dot, preferred_element_type=F32)

    def chunk_body(c, carry):
        r0 = pl.multiple_of(c * chunk, chunk)
        rows = pl.ds(r0, chunk)
        gc = gc_ref[rows, :]
        gr = gr_ref[:, rows]
        bc = sum(dot(tri_lower, p) for p in _split3(gc))
        br = sum(dot(p, tri_upper) for p in _split3(gr))
        for h in range(H):
            ig_c, b_c = gc[:, h:h + 1], bc[:, H + h:H + h + 1]
            ig_r, b_r = gr[h:h + 1, :], br[H + h:H + h + 1, :]
            b_last = b_c[chunk - 1:chunk, :]
            m_old = m_ref[h:h + 1, 0:1]
            qh = q_ref[rows, h * dk:(h + 1) * dk]
            kth = kt_ref[h * dk:(h + 1) * dk, rows]
            vh = v_ref[rows, h * dv:(h + 1) * dv]
            ct = c_ref[h]

            dmat = jnp.where(causal, b_c - b_r + ig_r, -jnp.inf)
            inter = b_c + m_old
            m_row = jnp.maximum(inter, jnp.max(dmat, axis=1, keepdims=True))
            s = dot(qh, kth) * jnp.exp(dmat - m_row)
            w_inter = jnp.exp(inter - m_row)
            qc = dot(qh, ct.astype(BF16))
            num = dot(s.astype(BF16), vh) + w_inter * qc[:, :dv]
            den = jnp.sum(s, axis=1, keepdims=True) + w_inter * qc[:, dv:dv + 1]
            hh = num * (1.0 / jnp.maximum(jnp.abs(den), jnp.exp(-m_row)))
            hn = _rms_norm(hh, ng_ref[:, h * dv:(h + 1) * dv])
            og = o_ref[rows, h * dv:(h + 1) * dv]
            out_ref[rows, h * dv:(h + 1) * dv] = (jax.nn.sigmoid(og) * hn).astype(out_ref.dtype)

            m_new = jnp.maximum(b_last + m_old,
                                jnp.max(b_last - b_r + ig_r, axis=1, keepdims=True))
            decay = jnp.exp(b_last + m_old - m_new)
            w_r = jnp.exp(b_last - b_r + ig_r - m_new)
            ktw = (kth.astype(F32) * w_r).astype(BF16)
            vaug = jnp.concatenate([vh, ones_col], axis=1)
            c_ref[h] = decay * ct + dot(ktw, vaug)
            m_ref[h:h + 1, :] = jnp.broadcast_to(m_new, (1, LANES))
        return carry

    lax.fori_loop(0, n_chunks, chunk_body, 0)


def _mlstm_mixer(x2d, batch, w_in, b_gates, norm_gain):
    T, D = x2d.shape
    H, dk, dv = MLSTM_HEADS, MLSTM_DQK, MLSTM_DV
    seq = T // batch
    tm = TOKEN_TILE
    cq, ck, cv, co = H * dk, 2 * H * dk, 2 * H * dk + H * dv, 2 * H * dk + 2 * H * dv
    wq = w_in[:, :cq].astype(BF16)
    wkt = w_in[:, cq:ck].T.astype(BF16)
    wv = w_in[:, ck:cv].astype(BF16)
    wo = w_in[:, cv:co].astype(BF16)
    wg = jnp.pad(w_in[:, co:], ((0, 0), (0, LANES - 2 * H))).astype(BF16)
    bg = jnp.pad(b_gates, (0, LANES - 2 * H)).reshape(1, LANES)

    full = lambda a: pl.BlockSpec(a.shape, lambda i: (0,) * a.ndim)
    q, kt, v, o, gc, gr = pl.pallas_call(
        _mlstm_inproj_kernel,
        grid=(T // tm,),
        in_specs=[pl.BlockSpec((tm, D), lambda i: (i, 0)), full(wq), full(wkt), full(wv), full(wo),
                  full(wg), full(bg)],
        out_specs=[pl.BlockSpec((tm, H * dk), lambda i: (i, 0)),
                   pl.BlockSpec((H * dk, tm), lambda i: (0, i)),
                   pl.BlockSpec((tm, H * dv), lambda i: (i, 0)),
                   pl.BlockSpec((tm, H * dv), lambda i: (i, 0)),
                   pl.BlockSpec((tm, LANES), lambda i: (i, 0)),
                   pl.BlockSpec((2 * H, tm), lambda i: (0, i))],
        out_shape=[jax.ShapeDtypeStruct((T, H * dk), BF16),
                   jax.ShapeDtypeStruct((H * dk, T), BF16),
                   jax.ShapeDtypeStruct((T, H * dv), BF16),
                   jax.ShapeDtypeStruct((T, H * dv), F32),
                   jax.ShapeDtypeStruct((T, LANES), F32),
                   jax.ShapeDtypeStruct((2 * H, T), F32)],
        compiler_params=_params("parallel"),
        name="mlstm_inproj",
    )(x2d, wq, wkt, wv, wo, wg, bg)

    ts = min(MLSTM_SEQ_BLOCK, seq)
    chunk = min(MLSTM_CHUNK, ts)
    nsb = seq // ts
    ng = norm_gain.reshape(1, H * dv)
    return pl.pallas_call(
        functools.partial(_mlstm_cell_kernel, chunk=chunk, n_chunks=ts // chunk),
        grid=(batch, nsb),
        in_specs=[pl.BlockSpec((ts, H * dk), lambda b, s: (b * nsb + s, 0)),
                  pl.BlockSpec((H * dk, ts), lambda b, s: (0, b * nsb + s)),
                  pl.BlockSpec((ts, H * dv), lambda b, s: (b * nsb + s, 0)),
                  pl.BlockSpec((ts, H * dv), lambda b, s: (b * nsb + s, 0)),
                  pl.BlockSpec((ts, LANES), lambda b, s: (b * nsb + s, 0)),
                  pl.BlockSpec((2 * H, ts), lambda b, s: (0, b * nsb + s)),
                  pl.BlockSpec((1, H * dv), lambda b, s: (0, 0))],
        out_specs=pl.BlockSpec((ts, H * dv), lambda b, s: (b * nsb + s, 0)),
        out_shape=jax.ShapeDtypeStruct((T, H * dv), BF16),
        scratch_shapes=[pltpu.VMEM((H, dk, 2 * dv), F32), pltpu.VMEM((H, LANES), F32)],
        compiler_params=_params("parallel", "arbitrary"),
        name="mlstm_cell",
    )(q, kt, v, o, gc, gr, ng)


def _mla_proj_kernel(x_ref, cos_ref, sin_ref, wcq_ref, wckv_ref, wkr_ref, qn_ref, kvn_ref,
                     wq_ref, wqr_ref, wkn_ref, wvt_ref, q_ref, k_ref, vt_ref):
    H, dn = MLA_HEADS, MLA_NOPE
    dot = functools.partial(jnp.dot, preferred_element_type=F32)
    xb = x_ref[...].astype(BF16)
    cos, sin = cos_ref[...], sin_ref[...]
    c_q = _rms_norm(dot(xb, wcq_ref[...]), qn_ref[...]).astype(BF16)
    c_kv = _rms_norm(dot(xb, wckv_ref[...]), kvn_ref[...]).astype(BF16)
    kr2 = dot(xb, wkr_ref[...])
    kr = (kr2[:, :LANES] * cos + kr2[:, LANES:] * sin).astype(BF16)
    scale = (MLA_NOPE + MLA_ROPE) ** -0.5 * LOG2_E
    qa = dot(c_q, wq_ref[...])
    qr = dot(c_q, wqr_ref[...])
    kn = dot(c_kv, wkn_ref[...])
    for h in range(H):
        base = h * 2 * LANES
        q_ref[h, :, :dn] = (qa[:, base:base + dn] * scale).astype(BF16)
        rope = qa[:, base + dn:base + 2 * LANES] * cos + qr[:, h * LANES:(h + 1) * LANES] * sin
        q_ref[h, :, dn:] = (rope * scale).astype(BF16)
        k_ref[h, :, :dn] = kn[:, h * dn:(h + 1) * dn].astype(BF16)
        k_ref[h, :, dn:] = kr
    vt_ref[...] = lax.dot_general(wvt_ref[...], c_kv, _NT, preferred_element_type=F32).astype(BF16)


def _attn_kernel(q_ref, k_ref, vt_ref, o_ref, m_ref, l_ref, acc_ref, *, tile, heads):
    dv = MLA_V
    qi = pl.program_id(2)
    m_ref[...] = jnp.full_like(m_ref, -jnp.inf)
    l_ref[...] = jnp.zeros_like(l_ref)
    acc_ref[...] = jnp.zeros_like(acc_ref)

    def step(j, masked):
        keys = pl.ds(pl.multiple_of(j * tile, tile), tile)
        for g in range(heads):
            st = lax.dot_general(k_ref[g, keys, :], q_ref[g], _NT,
                                 preferred_element_type=F32)
            if masked:
                kpos = lax.broadcasted_iota(I32, st.shape, 0)
                qpos = lax.broadcasted_iota(I32, st.shape, 1)
                st = jnp.where(kpos <= qpos, st, -jnp.inf)
            m_old = m_ref[g]
            m_new = jnp.maximum(m_old, jnp.max(st, axis=0, keepdims=True))
            p = jnp.exp2(st - m_new)
            alpha = jnp.exp2(m_old - m_new)
            l_ref[g] = alpha * l_ref[g] + jnp.sum(p, axis=0, keepdims=True)
            vt = vt_ref[g * dv:(g + 1) * dv, keys]
            acc_ref[g] = alpha * acc_ref[g] + jnp.dot(vt, p.astype(BF16),
                                                      preferred_element_type=F32)
            m_ref[g] = m_new

    def body(j, carry):
        step(j, masked=False)
        return carry

    lax.fori_loop(0, qi, body, 0)
    step(qi, masked=True)
    for g in range(heads):
        o_ref[:, g * dv:(g + 1) * dv] = (acc_ref[g] * (1.0 / l_ref[g])).T.astype(o_ref.dtype)


def _mla_mixer(x2d, batch, cos_t, sin_t, w_in, q_norm, kv_norm, w_qb, w_kvb):
    T, D = x2d.shape
    H, dn, dr, dv = MLA_HEADS, MLA_NOPE, MLA_ROPE, MLA_V
    seq = T // batch
    tm = TOKEN_TILE
    ql, kl = MLA_Q_LORA, MLA_KV_LORA
    half = dr // 2

    def rot(w):
        return jnp.concatenate([-w[..., half:], w[..., :half]], axis=-1)

    wcq = w_in[:, :ql].astype(BF16)
    wckv = w_in[:, ql:ql + kl].astype(BF16)
    wr = w_in[:, ql + kl:]
    zr = jnp.zeros((D, LANES - dr), F32)
    wkr = jnp.concatenate([wr, zr, rot(wr), zr], axis=1).astype(BF16)
    wq3 = w_qb.reshape(ql, H, dn + dr)
    zq = jnp.zeros((ql, H, LANES - dr), F32)
    wq = jnp.concatenate([wq3, zq], axis=2).reshape(ql, H * 2 * LANES).astype(BF16)
    wqr = jnp.concatenate([rot(wq3[:, :, dn:]), zq], axis=2).reshape(ql, H * LANES).astype(BF16)
    wkv3 = w_kvb.reshape(kl, H, dn + dv)
    wkn = wkv3[:, :, :dn].reshape(kl, H * dn).astype(BF16)
    wvt = wkv3[:, :, dn:].reshape(kl, H * dv).T.astype(BF16)
    qn = q_norm.reshape(1, ql)
    kvn = kv_norm.reshape(1, kl)

    full = lambda a: pl.BlockSpec(a.shape, lambda i: (0,) * a.ndim)
    q, k, vt = pl.pallas_call(
        _mla_proj_kernel,
        grid=(T // tm,),
        in_specs=[pl.BlockSpec((tm, D), lambda i: (i, 0)),
                  pl.BlockSpec((tm, LANES), lambda i: (i, 0)),
                  pl.BlockSpec((tm, LANES), lambda i: (i, 0)),
                  full(wcq), full(wckv), full(wkr), full(qn), full(kvn),
                  full(wq), full(wqr), full(wkn), full(wvt)],
        out_specs=[pl.BlockSpec((H, tm, 2 * LANES), lambda i: (0, i, 0)),
                   pl.BlockSpec((H, tm, 2 * LANES), lambda i: (0, i, 0)),
                   pl.BlockSpec((H * dv, tm), lambda i: (0, i))],
        out_shape=[jax.ShapeDtypeStruct((H, T, 2 * LANES), BF16),
                   jax.ShapeDtypeStruct((H, T, 2 * LANES), BF16),
                   jax.ShapeDtypeStruct((H * dv, T), BF16)],
        compiler_params=_params("parallel"),
        name="mla_proj",
    )(x2d, cos_t, sin_t, wcq, wckv, wkr, qn, kvn, wq, wqr, wkn, wvt)

    tile = min(ATTN_TILE, seq)
    nq = seq // tile
    hp = ATTN_HEADS_PER_STEP
    return pl.pallas_call(
        functools.partial(_attn_kernel, tile=tile, heads=hp),
        grid=(batch, H // hp, nq),
        in_specs=[pl.BlockSpec((hp, tile, 2 * LANES), lambda b, h, i: (h, b * nq + i, 0)),
                  pl.BlockSpec((hp, seq, 2 * LANES), lambda b, h, i: (h, b, 0)),
                  pl.BlockSpec((hp * dv, seq), lambda b, h, i: (h, b))],
        out_specs=pl.BlockSpec((tile, hp * dv), lambda b, h, i: (b * nq + i, h)),
        out_shape=jax.ShapeDtypeStruct((T, H * dv), BF16),
        scratch_shapes=[pltpu.VMEM((hp, 1, tile), F32), pltpu.VMEM((hp, 1, tile), F32),
                        pltpu.VMEM((hp, dv, tile), F32)],
        compiler_params=_params("parallel", "parallel", "arbitrary"),
        name="mla_attention",
    )(q, k, vt)


def _post_kernel(h_ref, x_ref, w_ref, g_ref, b_ref, wr_ref, br_ref,
                 xo_ref, idx_ref, gate_ref, pos_ref, cnt_ref, run_ref):
    E = N_EXPERTS
    tm = x_ref.shape[0]

    @pl.when(pl.program_id(0) == 0)
    def _():
        run_ref[...] = jnp.zeros_like(run_ref)

    mix = jnp.dot(h_ref[...], w_ref[...], preferred_element_type=F32)
    x1 = _layer_norm(DEEPNORM_ALPHA * x_ref[...] + mix, g_ref[...], b_ref[...])
    for c in range(xo_ref.shape[0]):
        xo_ref[c] = x1[:, c * SC_COLS:(c + 1) * SC_COLS]

    xh = x1.astype(BF16)
    xl = (x1 - xh.astype(F32)).astype(BF16)
    wh, wl = wr_ref[0], wr_ref[1]
    ntdot = lambda a, b: lax.dot_general(a, b, _NT, preferred_element_type=F32)
    logits = ntdot(wh, xh) + ntdot(wh, xl) + ntdot(wl, xh) + br_ref[...]

    e_iota = lax.broadcasted_iota(I32, (E, tm), 0)
    rest = logits
    vals, sels = [], []
    for k in range(TOP_K):
        v = jnp.max(rest, axis=0, keepdims=True)
        ik = jnp.min(jnp.where(rest == v, e_iota, E), axis=0, keepdims=True)
        sel = e_iota == ik
        rest = jnp.where(sel, -jnp.inf, rest)
        vals.append(v)
        sels.append(sel)
        idx_ref[k:k + 1, :] = ik
    ex = [jnp.exp(v - vals[0]) for v in vals]
    inv = 1.0 / sum(ex)
    for k in range(TOP_K):
        gate_ref[k:k + 1, :] = ex[k] * inv

    chosen = functools.reduce(jnp.logical_or, sels)
    onehot = chosen.astype(BF16)
    r = lax.broadcasted_iota(I32, (tm, tm), 0)
    c = lax.broadcasted_iota(I32, (tm, tm), 1)
    before = jnp.dot(onehot, (r < c).astype(BF16), preferred_element_type=F32)
    run = run_ref[:, 0:1]
    rank = before + run
    for k in range(TOP_K):
        pos_ref[k:k + 1, :] = jnp.sum(jnp.where(sels[k], rank, 0.0), axis=0,
                                      keepdims=True).astype(I32)
    run_new = run + jnp.sum(chosen.astype(F32), axis=1, keepdims=True)
    run_ref[...] = jnp.broadcast_to(run_new, run_ref.shape)
    cnt_ref[...] = jnp.broadcast_to(run_new, cnt_ref.shape)


def _post_mixer(h, x2d, w_out, ln_g, ln_b, w_router, b_router):
    T, D = x2d.shape
    nch = D // SC_COLS
    E = N_EXPERTS
    tm = TOKEN_TILE
    wo = w_out.astype(BF16)
    wrt = w_router.T
    wrh = wrt.astype(BF16)
    wr = jnp.stack([wrh, (wrt - wrh.astype(F32)).astype(BF16)])
    full = lambda a: pl.BlockSpec(a.shape, lambda i: (0,) * a.ndim)
    g, b, br = ln_g.reshape(1, D), ln_b.reshape(1, D), b_router.reshape(E, 1)
    rows = TOP_K
    return pl.pallas_call(
        _post_kernel,
        grid=(T // tm,),
        in_specs=[pl.BlockSpec((tm, h.shape[1]), lambda i: (i, 0)),
                  pl.BlockSpec((tm, D), lambda i: (i, 0)),
                  full(wo), full(g), full(b), full(wr), full(br)],
        out_specs=[pl.BlockSpec((nch, tm, SC_COLS), lambda i: (0, i, 0)),
                   pl.BlockSpec((rows, tm), lambda i: (0, i)),
                   pl.BlockSpec((rows, tm), lambda i: (0, i)),
                   pl.BlockSpec((rows, tm), lambda i: (0, i)),
                   pl.BlockSpec((E, LANES), lambda i: (0, 0))],
        out_shape=[jax.ShapeDtypeStruct((nch, T, SC_COLS), F32),
                   jax.ShapeDtypeStruct((rows, T), I32),
                   jax.ShapeDtypeStruct((rows, T), F32),
                   jax.ShapeDtypeStruct((rows, T), I32),
                   jax.ShapeDtypeStruct((E, LANES), F32)],
        scratch_shapes=[pltpu.VMEM((E, LANES), F32)],
        compiler_params=_params("arbitrary"),
        name="post_mixer_router",
    )(h, x2d, wo, g, b, wr, br)


def _sc_mesh():
    return plsc.VectorSubcoreMesh(core_axis_name="core", subcore_axis_name="subcore")


def _sc_scatter_rows(xc, dest, n_rows):
    nch, T, C = xc.shape
    K = dest.shape[0]
    W = SC_WINDOW
    xs = xc.reshape(nch * T, C)
    nb = (nch * T) // W
    offs = (jnp.arange(nch, dtype=I32) * n_rows)[None, :, None]
    idx = (dest[:, None, :] + offs).reshape(1, K * nch * T)

    @functools.partial(pl.kernel, out_type=jax.ShapeDtypeStruct((nch * n_rows, C), xc.dtype),
                       mesh=_sc_mesh(), scratch_types=[], name="moe_dispatch_scatter")
    def scatter(x_hbm, i_hbm, o_hbm):
        def body(x_vmem, i_vmem):
            pltpu.sync_copy(x_vmem, o_hbm.at[i_vmem.at[0]])

        pltpu.emit_pipeline(
            body,
            grid=(K * nb,),
            in_specs=[pl.BlockSpec((W, C), lambda g: (g % nb, 0)),
                      pl.BlockSpec((1, W), lambda g: (0, g))],
            out_specs=[],
            core_axis_name=("core", "subcore"),
            dimension_semantics=(pltpu.PARALLEL,),
        )(x_hbm, i_hbm)

    return scatter(xs, idx).reshape(nch, n_rows, C)


def _sc_gather_rows(yc, dest):
    nch, n_rows, C = yc.shape
    K, T = dest.shape
    W = SC_WINDOW
    ys = yc.reshape(nch * n_rows, C)
    offs = (jnp.arange(nch, dtype=I32) * n_rows)[:, None, None]
    idx = (dest[None, :, :] + offs).reshape(1, nch * K * T)
    n_sub = nch * K * T

    @functools.partial(pl.kernel, out_type=jax.ShapeDtypeStruct((n_sub, C), yc.dtype),
                       mesh=_sc_mesh(), scratch_types=[], name="moe_combine_gather")
    def gather(t_hbm, i_hbm, o_hbm):
        def body(i_vmem, o_vmem):
            pltpu.sync_copy(t_hbm.at[i_vmem.at[0]], o_vmem)

        pltpu.emit_pipeline(
            body,
            grid=(n_sub // W,),
            in_specs=[pl.BlockSpec((1, W), lambda g: (0, g))],
            out_specs=[pl.BlockSpec((W, C), lambda g: (g, 0))],
            core_axis_name=("core", "subcore"),
            dimension_semantics=(pltpu.PARALLEL,),
        )(i_hbm, o_hbm)

    return gather(ys, idx).reshape(nch, K, T, C)


def _expert_kernel(te_ref, nu_ref, x_ref, wgu_ref, bg_ref, bl_ref, wd_ref, bd_ref, perm_ref,
                   y_ref, wg_s, wl_s, wd_s):
    i = pl.program_id(0)
    active = i < nu_ref[0]
    e = te_ref[i]
    changed = jnp.logical_or(i == 0, e != te_ref[jnp.maximum(i - 1, 0)])
    dot = functools.partial(jnp.dot, preferred_element_type=F32)

    @pl.when(jnp.logical_and(active, changed))
    def _():
        n_blocks = wgu_ref.shape[3] // MXU_DIM
        for blk in range(n_blocks):
            wb = wgu_ref[0, 0, :, blk * MXU_DIM:(blk + 1) * MXU_DIM].astype(BF16)
            wp = dot(wb, perm_ref[...]).astype(BF16)
            wg_s[:, blk * LANES:(blk + 1) * LANES] = wp[:, :LANES]
            wl_s[:, blk * LANES:(blk + 1) * LANES] = wp[:, LANES:]
        wd_s[...] = wd_ref[0, 0].astype(BF16)

    @pl.when(active)
    def _():
        nch = x_ref.shape[0]
        xb = jnp.concatenate([x_ref[c] for c in range(nch)], axis=1).astype(BF16)
        g = jnp.minimum(dot(xb, wg_s[...]) + bg_ref[0], SWIGLU_LIMIT)
        lin = jnp.clip(dot(xb, wl_s[...]) + bl_ref[0], -SWIGLU_LIMIT, SWIGLU_LIMIT)
        act = (lin + 1.0) * g * jax.nn.sigmoid(SWIGLU_ALPHA * g)
        y = dot(act.astype(BF16), wd_s[...]) + bd_ref[0]
        for c in range(nch):
            y_ref[c] = y[:, c * SC_COLS:(c + 1) * SC_COLS]


def _combine_kernel(y0_ref, y1_ref, y2_ref, y3_ref, gate_ref, x_ref, g_ref, b_ref, o_ref):
    nch = x_ref.shape[0]
    cat = lambda ref: jnp.concatenate([ref[c] for c in range(nch)], axis=1)
    gate = gate_ref[...]
    ff = (gate[:, 0:1] * cat(y0_ref) + gate[:, 1:2] * cat(y1_ref)
          + gate[:, 2:3] * cat(y2_ref) + gate[:, 3:4] * cat(y3_ref))
    o_ref[...] = _layer_norm(DEEPNORM_ALPHA * cat(x_ref) + ff, g_ref[...], b_ref[...])


def _moe_and_norm(x1c, idx, gate, pos, cnt, layer, w_gu, b_gu, w_down, b_down, ln_g, ln_b):
    nch, T, C = x1c.shape
    D = nch * C
    E, K = N_EXPERTS, TOP_K
    dff = w_down.shape[2]
    te = EXPERT_TILE
    n_tiles = (T * K) // te + E
    n_rows = n_tiles * te

    counts = cnt[:, 0].astype(I32)
    padded = (counts + te - 1) // te * te
    pad_end = jnp.cumsum(padded)
    pad_start = pad_end - padded
    experts = jnp.arange(E, dtype=I32)
    start_of = jnp.sum(jnp.where(idx[None] == experts[:, None, None],
                                 pad_start[:, None, None], 0), axis=0)
    dest = start_of + pos
    tile_start = jnp.arange(n_tiles, dtype=I32) * te
    tile_expert = jnp.minimum(
        jnp.sum((pad_end[None, :] <= tile_start[:, None]).astype(I32), axis=1), E - 1)
    n_used = (pad_end[-1] // te).astype(I32).reshape(1)

    x_rows = _sc_scatter_rows(x1c, dest, n_rows)

    half = MXU_DIM // 2
    src = jnp.arange(MXU_DIM)
    perm = (src[:, None] == jnp.where(src < half, 2 * src, 2 * (src - half) + 1)[None, :])
    perm = perm.astype(BF16)
    bg = b_gu[:, 0::2].reshape(E, 1, dff)
    bl = b_gu[:, 1::2].reshape(E, 1, dff)
    bd = b_down.reshape(E, 1, D)

    def row_map(i, te_ref, nu_ref):
        return (0, jnp.minimum(i, nu_ref[0] - 1), 0)

    def exp_map(i, te_ref, nu_ref):
        return (te_ref[i], 0, 0)

    def stacked_map(i, te_ref, nu_ref):
        return (layer, te_ref[i], 0, 0)

    y_rows = pl.pallas_call(
        _expert_kernel,
        grid_spec=pltpu.PrefetchScalarGridSpec(
            num_scalar_prefetch=2,
            grid=(n_tiles,),
            in_specs=[pl.BlockSpec((nch, te, C), row_map),
                      pl.BlockSpec((1, 1, D, 2 * dff), stacked_map),
                      pl.BlockSpec((1, 1, dff), exp_map),
                      pl.BlockSpec((1, 1, dff), exp_map),
                      pl.BlockSpec((1, 1, dff, D), stacked_map),
                      pl.BlockSpec((1, 1, D), exp_map),
                      pl.BlockSpec((MXU_DIM, MXU_DIM), lambda i, a, b: (0, 0))],
            out_specs=pl.BlockSpec((nch, te, C), row_map),
            scratch_shapes=[pltpu.VMEM((D, dff), BF16), pltpu.VMEM((D, dff), BF16),
                            pltpu.VMEM((dff, D), BF16)],
        ),
        out_shape=jax.ShapeDtypeStruct((nch, n_rows, C), F32),
        compiler_params=_params("arbitrary"),
        name="moe_experts",
    )(tile_expert, n_used, x_rows, w_gu, bg, bl, w_down, bd, perm)

    y_tok = _sc_gather_rows(y_rows, dest)

    tm = TOKEN_TILE
    gate_col = gate.T
    slot = lambda k: pl.BlockSpec((nch, None, tm, C), lambda i, k=k: (0, k, i, 0))
    vec = pl.BlockSpec((1, D), lambda i: (0, 0))
    return pl.pallas_call(
        _combine_kernel,
        grid=(T // tm,),
        in_specs=[slot(0), slot(1), slot(2), slot(3),
                  pl.BlockSpec((tm, K), lambda i: (i, 0)),
                  pl.BlockSpec((nch, tm, C), lambda i: (0, i, 0)), vec, vec],
        out_specs=pl.BlockSpec((tm, D), lambda i: (i, 0)),
        out_shape=jax.ShapeDtypeStruct((T, D), F32),
        compiler_params=_params("parallel"),
        name="moe_combine_norm",
    )(y_tok, y_tok, y_tok, y_tok, gate_col, x1c, ln_g.reshape(1, D), ln_b.reshape(1, D))


def kernel(x, positions, ln_gain, ln_bias, mlstm_w_in, mlstm_b_gates, mlstm_norm_gain,
           mlstm_w_out, mla_w_in, mla_q_norm, mla_kv_norm, mla_w_qb, mla_w_kvb, mla_w_out,
           moe_w_router, moe_b_router, moe_w_gate_up, moe_b_gate_up, moe_w_down, moe_b_down):
    B, S, D = x.shape
    T = B * S
    x2d = x.reshape(T, D)

    inv_freq = ROPE_THETA ** (-jnp.arange(0, MLA_ROPE, 2, dtype=F32) / MLA_ROPE)
    ang = positions.astype(F32).reshape(T, 1) * inv_freq
    reps = LANES // ang.shape[1]
    cos_t = jnp.tile(jnp.cos(ang), (1, reps))
    sin_t = jnp.tile(jnp.sin(ang), (1, reps))

    for layer in range(DEPTH):
        j = layer // 2
        if layer % 2 == 0:
            h = _mlstm_mixer(x2d, B, mlstm_w_in[j], mlstm_b_gates[j], mlstm_norm_gain[j])
            w_out = mlstm_w_out[j]
        else:
            h = _mla_mixer(x2d, B, cos_t, sin_t, mla_w_in[j], mla_q_norm[j], mla_kv_norm[j],
                           mla_w_qb[j], mla_w_kvb[j])
            w_out = mla_w_out[j]
        x1c, idx, gate, pos, cnt = _post_mixer(h, x2d, w_out, ln_gain[layer, 0], ln_bias[layer, 0],
                                               moe_w_router[layer], moe_b_router[layer])
        x2d = _moe_and_norm(x1c, idx, gate, pos, cnt, layer, moe_w_gate_up, moe_b_gate_up[layer],
                            moe_w_down, moe_b_down[layer], ln_gain[layer, 1], ln_bias[layer, 1])
    return x2d.reshape(B, S, D)
```

```python
import functools

import jax
import jax.numpy as jnp
from jax import lax
from jax.experimental import pallas as pl
from jax.experimental.pallas import tpu as pltpu
from jax.experimental.pallas import tpu_sc as plsc

F32 = jnp.float32
BF16 = jnp.bfloat16
I32 = jnp.int32
U32 = jnp.uint32

DEPTH = 4
MLSTM_HEADS = 8
MLSTM_DQK = 64
MLSTM_DV = 128
MLA_HEADS = 8
MLA_Q_LORA = 384
MLA_KV_LORA = 256
MLA_NOPE = 128
MLA_ROPE = 64
MLA_V = 128
ROPE_THETA = 10000.0
N_EXPERTS = 32
TOP_K = 4
SWIGLU_LIMIT = 7.0
SWIGLU_ALPHA = 1.702
DEEPNORM_ALPHA = (2.0 * DEPTH) ** 0.25
LN_EPS = 1e-5
LOG2_E = 1.4426950408889634
RMS_EPS = 1e-6

LANES = 128
MXU_DIM = 256
VMEM_LIMIT_BYTES = 56 * 1024 * 1024

TOKEN_TILE = 512
MLSTM_SEQ_BLOCK = 1024
MLSTM_CHUNK = 256
ATTN_TILE = 512
ATTN_HEADS_PER_STEP = 2
ATTN_QUERY_TILES_PER_STEP = 2
EXPERT_TILE = 512
SC_WINDOW = 128
SC_COLS = 256

_NT = (((1,), (1,)), ((), ()))


def _params(*sem):
    return pltpu.CompilerParams(dimension_semantics=sem, vmem_limit_bytes=VMEM_LIMIT_BYTES)


def _layer_norm(z, g, b):
    mu = jnp.mean(z, axis=-1, keepdims=True)
    zc = z - mu
    var = jnp.mean(zc * zc, axis=-1, keepdims=True)
    return zc * lax.rsqrt(var + LN_EPS) * g + b


def _rms_norm(z, g):
    return z * lax.rsqrt(jnp.mean(z * z, axis=-1, keepdims=True) + RMS_EPS) * g


def _store_packed(ref, a):
    half = a.shape[1] // 2

    def bf16_bits(v):
        u = lax.bitcast_convert_type(v, U32)
        return (u + jnp.uint32(0x7FFF) + ((u >> 16) & jnp.uint32(1))) >> 16

    words = bf16_bits(a[:, :half]) | (bf16_bits(a[:, half:]) << 16)
    for c in range(ref.shape[0]):
        ref[c] = words[:, c * SC_COLS:(c + 1) * SC_COLS]


def _load_packed(ref):
    chunks = [ref[c] for c in range(ref.shape[0])]
    lo = [lax.bitcast_convert_type(w << 16, F32) for w in chunks]
    hi = [lax.bitcast_convert_type(w & jnp.uint32(0xFFFF0000), F32) for w in chunks]
    return jnp.concatenate(lo + hi, axis=1)


def _split3(a):
    hi = a.astype(BF16)
    r = a - hi.astype(F32)
    mid = r.astype(BF16)
    lo = (r - mid.astype(F32)).astype(BF16)
    return hi, mid, lo


def _mlstm_inproj_kernel(x_ref, wq_ref, wkt_ref, wv_ref, wo_ref, wg_ref, bg_ref,
                         q_ref, kt_ref, v_ref, o_ref, gc_ref, gr_ref):
    xb = x_ref[...].astype(BF16)
    dot = functools.partial(jnp.dot, preferred_element_type=F32)
    q_ref[...] = (dot(xb, wq_ref[...]) * (MLSTM_DQK ** -0.5)).astype(BF16)
    kt_ref[...] = lax.dot_general(wkt_ref[...], xb, _NT, preferred_element_type=F32).astype(BF16)
    v_ref[...] = dot(xb, wv_ref[...]).astype(BF16)
    o_ref[...] = dot(xb, wo_ref[...])
    z = dot(xb, wg_ref[...]) + bg_ref[...]
    lane = lax.broadcasted_iota(I32, z.shape, 1)
    log_sig = jnp.minimum(z, 0.0) - jnp.log1p(jnp.exp(-jnp.abs(z)))
    g = jnp.where(lane < MLSTM_HEADS, z, log_sig)
    gc_ref[...] = g
    gr_ref[...] = g.T[:2 * MLSTM_HEADS, :]


def _mlstm_cell_kernel(q_ref, kt_ref, v_ref, o_ref, gc_ref, gr_ref, ng_ref, out_ref,
                       c_ref, m_ref, *, chunk, n_chunks):
    H, dk, dv = MLSTM_HEADS, MLSTM_DQK, MLSTM_DV

    @pl.when(pl.program_id(1) == 0)
    def _():
        c_ref[...] = jnp.zeros_like(c_ref)
        m_ref[...] = jnp.zeros_like(m_ref)

    row = lax.broadcasted_iota(I32, (chunk, chunk), 0)
    col = lax.broadcasted_iota(I32, (chunk, chunk), 1)
    causal = col <= row
    tri_lower = causal.astype(BF16)
    tri_upper = (row <= col).astype(BF16)
    ones_col = (lax.broadcasted_iota(I32, (chunk, LANES), 1) == 0).astype(BF16)
    dot = functools.partial(jnp.dot, preferred_element_type=F32)

    def chunk_body(c, carry):
        r0 = pl.multiple_of(c * chunk, chunk)
        rows = pl.ds(r0, chunk)
        gc = gc_ref[rows, :]
        gr = gr_ref[:, rows]
        bc = sum(dot(tri_lower, p) for p in _split3(gc))
        br = sum(dot(p, tri_upper) for p in _split3(gr))
        hs = range(H)
        b_c = [bc[:, H + h:H + h + 1] for h in hs]
        ig_r = [gr[h:h + 1, :] for h in hs]
        b_r = [br[H + h:H + h + 1, :] for h in hs]
        b_last = [b[chunk - 1:chunk, :] for b in b_c]
        m_lanes = [m_ref[h:h + 1, :] for h in hs]
        m_old = [m[:, 0:1] for m in m_lanes]
        qh = [q_ref[rows, h * dk:(h + 1) * dk] for h in hs]
        kth = [kt_ref[h * dk:(h + 1) * dk, rows] for h in hs]
        vh = [v_ref[rows, h * dv:(h + 1) * dv] for h in hs]
        ct = [c_ref[h] for h in hs]

        qk = [dot(qh[h], kth[h]) for h in hs]
        qc = [dot(qh[h], ct[h].astype(BF16)) for h in hs]

        m_new = [jnp.maximum(b_last[h] + m_old[h],
                             jnp.max(b_last[h] - b_r[h] + ig_r[h], axis=1, keepdims=True)) for h in hs]
        ktw = [(kth[h].astype(F32) * jnp.exp(b_last[h] - b_r[h] + ig_r[h] - m_new[h])).astype(BF16)
               for h in hs]
        for h in hs:
            vaug = jnp.concatenate([vh[h], ones_col], axis=1)
            c_ref[h] = jnp.exp(b_last[h] + m_old[h] - m_new[h]) * ct[h] + dot(ktw[h], vaug)
            m_ref[h:h + 1, :] = jnp.broadcast_to(m_new[h], (1, LANES))

        a_mat = [jnp.where(causal, ig_r[h] - b_r[h], -jnp.inf) for h in hs]
        m_rep = [jnp.broadcast_to(m_lanes[h], (chunk, LANES)) for h in hs]
        g = [jnp.maximum(m_rep[h], jnp.max(a_mat[h], axis=1, keepdims=True)) for h in hs]
        s = [qk[h] * jnp.exp(a_mat[h] - jnp.concatenate([g[h]] * (chunk // LANES), axis=1))
             for h in hs]
        w_inter = [jnp.exp(m_rep[h] - g[h]) for h in hs]
        nd = [dot(s[h].astype(BF16), jnp.concatenate([vh[h], ones_col], axis=1))
              + jnp.concatenate([w_inter[h]] * (2 * dv // LANES), axis=1) * qc[h]
              for h in hs]
        for h in hs:
            num, den = nd[h][:, :dv], nd[h][:, dv:dv + 1]
            r = 1.0 / jnp.maximum(jnp.abs(den), jnp.exp(-(b_c[h] + g[h][:, 0:1])))
            scale = r * lax.rsqrt(r * r * jnp.mean(num * num, axis=1, keepdims=True) + RMS_EPS)
            og = o_ref[rows, h * dv:(h + 1) * dv]
            out_ref[rows, h * dv:(h + 1) * dv] = (
                jax.nn.sigmoid(og) * (num * scale * ng_ref[:, h * dv:(h + 1) * dv])
            ).astype(out_ref.dtype)
        return carry

    lax.fori_loop(0, n_chunks, chunk_body, 0)


def _mlstm_mixer(x2d, batch, w_in, b_gates, norm_gain):
    T, D = x2d.shape
    H, dk, dv = MLSTM_HEADS, MLSTM_DQK, MLSTM_DV
    seq = T // batch
    tm = TOKEN_TILE
    cq, ck, cv, co = H * dk, 2 * H * dk, 2 * H * dk + H * dv, 2 * H * dk + 2 * H * dv
    wq = w_in[:, :cq].astype(BF16)
    wkt = w_in[:, cq:ck].T.astype(BF16)
    wv = w_in[:, ck:cv].astype(BF16)
    wo = w_in[:, cv:co].astype(BF16)
    wg = jnp.pad(w_in[:, co:], ((0, 0), (0, LANES - 2 * H))).astype(BF16)
    bg = jnp.pad(b_gates, (0, LANES - 2 * H)).reshape(1, LANES)

    full = lambda a: pl.BlockSpec(a.shape, lambda i: (0,) * a.ndim)
    q, kt, v, o, gc, gr = pl.pallas_call(
        _mlstm_inproj_kernel,
        grid=(T // tm,),
        in_specs=[pl.BlockSpec((tm, D), lambda i: (i, 0)), full(wq), full(wkt), full(wv), full(wo),
                  full(wg), full(bg)],
        out_specs=[pl.BlockSpec((tm, H * dk), lambda i: (i, 0)),
                   pl.BlockSpec((H * dk, tm), lambda i: (0, i)),
                   pl.BlockSpec((tm, H * dv), lambda i: (i, 0)),
                   pl.BlockSpec((tm, H * dv), lambda i: (i, 0)),
                   pl.BlockSpec((tm, LANES), lambda i: (i, 0)),
                   pl.BlockSpec((2 * H, tm), lambda i: (0, i))],
        out_shape=[jax.ShapeDtypeStruct((T, H * dk), BF16),
                   jax.ShapeDtypeStruct((H * dk, T), BF16),
                   jax.ShapeDtypeStruct((T, H * dv), BF16),
                   jax.ShapeDtypeStruct((T, H * dv), F32),
                   jax.ShapeDtypeStruct((T, LANES), F32),
                   jax.ShapeDtypeStruct((2 * H, T), F32)],
        compiler_params=_params("parallel"),
        name="mlstm_inproj",
    )(x2d, wq, wkt, wv, wo, wg, bg)

    ts = min(MLSTM_SEQ_BLOCK, seq)
    chunk = min(MLSTM_CHUNK, ts)
    nsb = seq // ts
    ng = norm_gain.reshape(1, H * dv)
    return pl.pallas_call(
        functools.partial(_mlstm_cell_kernel, chunk=chunk, n_chunks=ts // chunk),
        grid=(batch, nsb),
        in_specs=[pl.BlockSpec((ts, H * dk), lambda b, s: (b * nsb + s, 0)),
                  pl.BlockSpec((H * dk, ts), lambda b, s: (0, b * nsb + s)),
                  pl.BlockSpec((ts, H * dv), lambda b, s: (b * nsb + s, 0)),
                  pl.BlockSpec((ts, H * dv), lambda b, s: (b * nsb + s, 0)),
                  pl.BlockSpec((ts, LANES), lambda b, s: (b * nsb + s, 0)),
                  pl.BlockSpec((2 * H, ts), lambda b, s: (0, b * nsb + s)),
                  pl.BlockSpec((1, H * dv), lambda b, s: (0, 0))],
        out_specs=pl.BlockSpec((ts, H * dv), lambda b, s: (b * nsb + s, 0)),
        out_shape=jax.ShapeDtypeStruct((T, H * dv), BF16),
        scratch_shapes=[pltpu.VMEM((H, dk, 2 * dv), F32), pltpu.VMEM((H, LANES), F32)],
        compiler_params=_params("parallel", "arbitrary"),
        name="mlstm_cell",
    )(q, kt, v, o, gc, gr, ng)


def _mla_proj_kernel(x_ref, cos_ref, sin_ref, wcq_ref, wckv_ref, wkr_ref, qn_ref, kvn_ref,
                     wq_ref, wqr_ref, wkn_ref, wvt_ref, q_ref, k_ref, vt_ref):
    H, dn = MLA_HEADS, MLA_NOPE
    dot = functools.partial(jnp.dot, preferred_element_type=F32)
    xb = x_ref[...].astype(BF16)
    cos, sin = cos_ref[...], sin_ref[...]
    c_q = _rms_norm(dot(xb, wcq_ref[...]), qn_ref[...]).astype(BF16)
    c_kv = _rms_norm(dot(xb, wckv_ref[...]), kvn_ref[...]).astype(BF16)
    kr2 = dot(xb, wkr_ref[...])
    kr = (kr2[:, :LANES] * cos + kr2[:, LANES:] * sin).astype(BF16)
    scale = (MLA_NOPE + MLA_ROPE) ** -0.5 * LOG2_E
    qa = dot(c_q, wq_ref[...])
    qr = dot(c_q, wqr_ref[...])
    kn = dot(c_kv, wkn_ref[...])
    for h in range(H):
        base = h * 2 * LANES
        q_ref[h, :, :dn] = (qa[:, base:base + dn] * scale).astype(BF16)
        rope = qa[:, base + dn:base + 2 * LANES] * cos + qr[:, h * LANES:(h + 1) * LANES] * sin
        q_ref[h, :, dn:] = (rope * scale).astype(BF16)
        k_ref[h, :, :dn] = kn[:, h * dn:(h + 1) * dn].astype(BF16)
        k_ref[h, :, dn:] = kr
    vt_ref[...] = lax.dot_general(wvt_ref[...], c_kv, _NT, preferred_element_type=F32).astype(BF16)


def _attn_kernel(q_ref, k_ref, vt_ref, o_ref, m_ref, l_ref, acc_ref, *, tile, heads, qsub):
    dv = MLA_V
    qi = pl.program_id(2)
    m_ref[...] = jnp.full_like(m_ref, -jnp.inf)
    l_ref[...] = jnp.zeros_like(l_ref)
    acc_ref[...] = jnp.zeros_like(acc_ref)

    def step(j, diag_sub):
        keys = pl.ds(pl.multiple_of(j * tile, tile), tile)
        first = 0 if diag_sub is None else diag_sub
        chains = [(g, u) for g in range(heads) for u in range(first, qsub)]
        slot = lambda c: c[0] * qsub + c[1]
        sts = {}
        for g, u in chains:
            st = lax.dot_general(k_ref[g, keys, :], q_ref[g, u * tile:(u + 1) * tile, :], _NT,
                                 preferred_element_type=F32)
            if u == diag_sub:
                kpos = lax.broadcasted_iota(I32, st.shape, 0)
                qpos = lax.broadcasted_iota(I32, st.shape, 1)
                st = jnp.where(kpos <= qpos, st, -jnp.inf)
            sts[g, u] = st
        m_old = {c: m_ref[slot(c)] for c in chains}
        m_new = {c: jnp.maximum(m_old[c], jnp.max(sts[c], axis=0, keepdims=True)) for c in chains}
        ps = {c: jnp.exp2(sts[c] - m_new[c]) for c in chains}
        alpha = {c: jnp.exp2(m_old[c] - m_new[c]) for c in chains}
        for c in chains:
            l_ref[slot(c)] = alpha[c] * l_ref[slot(c)] + jnp.sum(ps[c], axis=0, keepdims=True)
            m_ref[slot(c)] = m_new[c]
        for c in chains:
            vt = vt_ref[c[0] * dv:(c[0] + 1) * dv, keys]
            acc_ref[slot(c)] = alpha[c] * acc_ref[slot(c)] + jnp.dot(
                vt, ps[c].astype(BF16), preferred_element_type=F32)

    def body(j, carry):
        step(j, None)
        return carry

    lax.fori_loop(0, qsub * qi, body, 0)
    for u in range(qsub):
        step(qsub * qi + u, u)
    for g in range(heads):
        for u in range(qsub):
            c = g * qsub + u
            o_ref[u * tile:(u + 1) * tile, g * dv:(g + 1) * dv] = (
                acc_ref[c] * (1.0 / l_ref[c])).T.astype(o_ref.dtype)


def _mla_mixer(x2d, batch, cos_t, sin_t, w_in, q_norm, kv_norm, w_qb, w_kvb):
    T, D = x2d.shape
    H, dn, dr, dv = MLA_HEADS, MLA_NOPE, MLA_ROPE, MLA_V
    seq = T // batch
    tm = TOKEN_TILE
    ql, kl = MLA_Q_LORA, MLA_KV_LORA
    half = dr // 2

    def rot(w):
        return jnp.concatenate([-w[..., half:], w[..., :half]], axis=-1)

    wcq = w_in[:, :ql].astype(BF16)
    wckv = w_in[:, ql:ql + kl].astype(BF16)
    wr = w_in[:, ql + kl:]
    zr = jnp.zeros((D, LANES - dr), F32)
    wkr = jnp.concatenate([wr, zr, rot(wr), zr], axis=1).astype(BF16)
    wq3 = w_qb.reshape(ql, H, dn + dr)
    zq = jnp.zeros((ql, H, LANES - dr), F32)
    wq = jnp.concatenate([wq3, zq], axis=2).reshape(ql, H * 2 * LANES).astype(BF16)
    wqr = jnp.concatenate([rot(wq3[:, :, dn:]), zq], axis=2).reshape(ql, H * LANES).astype(BF16)
    wkv3 = w_kvb.reshape(kl, H, dn + dv)
    wkn = wkv3[:, :, :dn].reshape(kl, H * dn).astype(BF16)
    wvt = wkv3[:, :, dn:].reshape(kl, H * dv).T.astype(BF16)
    qn = q_norm.reshape(1, ql)
    kvn = kv_norm.reshape(1, kl)

    full = lambda a: pl.BlockSpec(a.shape, lambda i: (0,) * a.ndim)
    q, k, vt = pl.pallas_call(
        _mla_proj_kernel,
        grid=(T // tm,),
        in_specs=[pl.BlockSpec((tm, D), lambda i: (i, 0)),
                  pl.BlockSpec((tm, LANES), lambda i: (i, 0)),
                  pl.BlockSpec((tm, LANES), lambda i: (i, 0)),
                  full(wcq), full(wckv), full(wkr), full(qn), full(kvn),
                  full(wq), full(wqr), full(wkn), full(wvt)],
        out_specs=[pl.BlockSpec((H, tm, 2 * LANES), lambda i: (0, i, 0)),
                   pl.BlockSpec((H, tm, 2 * LANES), lambda i: (0, i, 0)),
                   pl.BlockSpec((H * dv, tm), lambda i: (0, i))],
        out_shape=[jax.ShapeDtypeStruct((H, T, 2 * LANES), BF16),
                   jax.ShapeDtypeStruct((H, T, 2 * LANES), BF16),
                   jax.ShapeDtypeStruct((H * dv, T), BF16)],
        compiler_params=_params("parallel"),
        name="mla_proj",
    )(x2d, cos_t, sin_t, wcq, wckv, wkr, qn, kvn, wq, wqr, wkn, wvt)

    tile = min(ATTN_TILE, seq)
    hp = ATTN_HEADS_PER_STEP
    qsub = min(ATTN_QUERY_TILES_PER_STEP, seq // tile)
    tq = tile * qsub
    nq = seq // tq
    return pl.pallas_call(
        functools.partial(_attn_kernel, tile=tile, heads=hp, qsub=qsub),
        grid=(batch, H // hp, nq),
        in_specs=[pl.BlockSpec((hp, tq, 2 * LANES), lambda b, h, i: (h, b * nq + i, 0)),
                  pl.BlockSpec((hp, seq, 2 * LANES), lambda b, h, i: (h, b, 0)),
                  pl.BlockSpec((hp * dv, seq), lambda b, h, i: (h, b))],
        out_specs=pl.BlockSpec((tq, hp * dv), lambda b, h, i: (b * nq + i, h)),
        out_shape=jax.ShapeDtypeStruct((T, H * dv), BF16),
        scratch_shapes=[pltpu.VMEM((hp * qsub, 1, tile), F32), pltpu.VMEM((hp * qsub, 1, tile), F32),
                        pltpu.VMEM((hp * qsub, dv, tile), F32)],
        compiler_params=_params("parallel", "parallel", "arbitrary"),
        name="mla_attention",
    )(q, k, vt)


def _post_kernel(h_ref, x_ref, w_ref, g_ref, b_ref, wr_ref, br_ref,
                 xo_ref, xp_ref, idx_ref, gate_ref, pos_ref, cnt_ref, run_ref):
    E = N_EXPERTS
    tm = x_ref.shape[0]

    @pl.when(pl.program_id(0) == 0)
    def _():
        run_ref[...] = jnp.zeros_like(run_ref)

    mix = jnp.dot(h_ref[...], w_ref[...], preferred_element_type=F32)
    x1 = _layer_norm(DEEPNORM_ALPHA * x_ref[...] + mix, g_ref[...], b_ref[...])
    xo_ref[...] = x1
    _store_packed(xp_ref, x1)

    xh = x1.astype(BF16)
    xl = (x1 - xh.astype(F32)).astype(BF16)
    wh, wl = wr_ref[0], wr_ref[1]
    ntdot = lambda a, b: lax.dot_general(a, b, _NT, preferred_element_type=F32)
    logits = ntdot(wh, xh) + ntdot(wh, xl) + ntdot(wl, xh) + br_ref[...]

    e_iota = lax.broadcasted_iota(I32, (E, tm), 0)
    rest = logits
    vals, sels = [], []
    for k in range(TOP_K):
        v = jnp.max(rest, axis=0, keepdims=True)
        ik = jnp.min(jnp.where(rest == v, e_iota, E), axis=0, keepdims=True)
        sel = e_iota == ik
        rest = jnp.where(sel, -jnp.inf, rest)
        vals.append(v)
        sels.append(sel)
        idx_ref[k:k + 1, :] = ik
    ex = [jnp.exp(v - vals[0]) for v in vals]
    inv = 1.0 / sum(ex)
    for k in range(TOP_K):
        gate_ref[k:k + 1, :] = ex[k] * inv

    chosen = functools.reduce(jnp.logical_or, sels)
    onehot = chosen.astype(BF16)
    r = lax.broadcasted_iota(I32, (tm, tm), 0)
    c = lax.broadcasted_iota(I32, (tm, tm), 1)
    before = jnp.dot(onehot, (r < c).astype(BF16), preferred_element_type=F32)
    run = run_ref[:, 0:1]
    rank = before + run
    for k in range(TOP_K):
        pos_ref[k:k + 1, :] = jnp.sum(jnp.where(sels[k], rank, 0.0), axis=0,
                                      keepdims=True).astype(I32)
    run_new = run + jnp.sum(chosen.astype(F32), axis=1, keepdims=True)
    run_ref[...] = jnp.broadcast_to(run_new, run_ref.shape)
    cnt_ref[...] = jnp.broadcast_to(run_new, cnt_ref.shape)


def _post_mixer(h, x2d, w_out, ln_g, ln_b, w_router, b_router):
    T, D = x2d.shape
    nch = D // (2 * SC_COLS)
    E = N_EXPERTS
    tm = TOKEN_TILE
    wo = w_out.astype(BF16)
    wrt = w_router.T
    wrh = wrt.astype(BF16)
    wr = jnp.stack([wrh, (wrt - wrh.astype(F32)).astype(BF16)])
    full = lambda a: pl.BlockSpec(a.shape, lambda i: (0,) * a.ndim)
    g, b, br = ln_g.reshape(1, D), ln_b.reshape(1, D), b_router.reshape(E, 1)
    rows = TOP_K
    return pl.pallas_call(
        _post_kernel,
        grid=(T // tm,),
        in_specs=[pl.BlockSpec((tm, h.shape[1]), lambda i: (i, 0)),
                  pl.BlockSpec((tm, D), lambda i: (i, 0)),
                  full(wo), full(g), full(b), full(wr), full(br)],
        out_specs=[pl.BlockSpec((tm, D), lambda i: (i, 0)),
                   pl.BlockSpec((nch, tm, SC_COLS), lambda i: (0, i, 0)),
                   pl.BlockSpec((rows, tm), lambda i: (0, i)),
                   pl.BlockSpec((rows, tm), lambda i: (0, i)),
                   pl.BlockSpec((rows, tm), lambda i: (0, i)),
                   pl.BlockSpec((E, LANES), lambda i: (0, 0))],
        out_shape=[jax.ShapeDtypeStruct((T, D), F32),
                   jax.ShapeDtypeStruct((nch, T, SC_COLS), U32),
                   jax.ShapeDtypeStruct((rows, T), I32),
                   jax.ShapeDtypeStruct((rows, T), F32),
                   jax.ShapeDtypeStruct((rows, T), I32),
                   jax.ShapeDtypeStruct((E, LANES), F32)],
        scratch_shapes=[pltpu.VMEM((E, LANES), F32)],
        compiler_params=_params("arbitrary"),
        name="post_mixer_router",
    )(h, x2d, wo, g, b, wr, br)


def _sc_mesh():
    return plsc.VectorSubcoreMesh(core_axis_name="core", subcore_axis_name="subcore")


def _sc_scatter_rows(xc, dest, n_rows):
    nch, T, C = xc.shape
    K = dest.shape[0]
    W = SC_WINDOW
    xs = xc.reshape(nch * T, C)
    nb = (nch * T) // W
    offs = (jnp.arange(nch, dtype=I32) * n_rows)[None, :, None]
    idx = (dest[:, None, :] + offs).reshape(K, nch * T)

    @functools.partial(pl.kernel, out_type=jax.ShapeDtypeStruct((nch * n_rows, C), xc.dtype),
                       mesh=_sc_mesh(), scratch_types=[], name="moe_dispatch_scatter")
    def scatter(x_hbm, i_hbm, o_hbm):
        def body(x_vmem, i_vmem):
            for k in range(K):
                pltpu.sync_copy(x_vmem, o_hbm.at[i_vmem.at[k]])

        pltpu.emit_pipeline(
            body,
            grid=(nb,),
            in_specs=[pl.BlockSpec((W, C), lambda g: (g, 0)),
                      pl.BlockSpec((K, W), lambda g: (0, g))],
            out_specs=[],
            core_axis_name=("core", "subcore"),
            dimension_semantics=(pltpu.PARALLEL,),
        )(x_hbm, i_hbm)

    return scatter(xs, idx).reshape(nch, n_rows, C)


def _sc_gather_rows(yc, dest):
    nch, n_rows, C = yc.shape
    K, T = dest.shape
    W = SC_WINDOW
    ys = yc.reshape(nch * n_rows, C)
    offs = (jnp.arange(nch, dtype=I32) * n_rows)[:, None, None]
    idx = (dest[None, :, :] + offs).reshape(1, nch * K * T)
    n_sub = nch * K * T

    @functools.partial(pl.kernel, out_type=jax.ShapeDtypeStruct((n_sub, C), yc.dtype),
                       mesh=_sc_mesh(), scratch_types=[], name="moe_combine_gather")
    def gather(t_hbm, i_hbm, o_hbm):
        def body(i_vmem, o_vmem):
            pltpu.sync_copy(t_hbm.at[i_vmem.at[0]], o_vmem)

        pltpu.emit_pipeline(
            body,
            grid=(n_sub // W,),
            in_specs=[pl.BlockSpec((1, W), lambda g: (0, g))],
            out_specs=[pl.BlockSpec((W, C), lambda g: (g, 0))],
            core_axis_name=("core", "subcore"),
            dimension_semantics=(pltpu.PARALLEL,),
        )(i_hbm, o_hbm)

    return gather(ys, idx).reshape(nch, K, T, C)


def _expert_kernel(te_ref, nu_ref, x_ref, wgu_ref, bg_ref, bl_ref, wd_ref, bd_ref, perm_ref,
                   y_ref, wg_s, wl_s, wd_s):
    i = pl.program_id(0)
    active = i < nu_ref[0]
    e = te_ref[i]
    changed = jnp.logical_or(i == 0, e != te_ref[jnp.maximum(i - 1, 0)])
    dot = functools.partial(jnp.dot, preferred_element_type=F32)

    @pl.when(jnp.logical_and(active, changed))
    def _():
        n_blocks = wgu_ref.shape[3] // MXU_DIM
        for blk in range(n_blocks):
            wb = wgu_ref[0, 0, :, blk * MXU_DIM:(blk + 1) * MXU_DIM].astype(BF16)
            wp = dot(wb, perm_ref[...]).astype(BF16)
            wg_s[:, blk * LANES:(blk + 1) * LANES] = wp[:, :LANES]
            wl_s[:, blk * LANES:(blk + 1) * LANES] = wp[:, LANES:]
        wd_s[...] = wd_ref[0, 0].astype(BF16)

    @pl.when(active)
    def _():
        xb = _load_packed(x_ref).astype(BF16)
        g = jnp.minimum(dot(xb, wg_s[...]) + bg_ref[0], SWIGLU_LIMIT)
        lin = jnp.clip(dot(xb, wl_s[...]) + bl_ref[0], -SWIGLU_LIMIT, SWIGLU_LIMIT)
        act = (lin + 1.0) * g * jax.nn.sigmoid(SWIGLU_ALPHA * g)
        _store_packed(y_ref, dot(act.astype(BF16), wd_s[...]) + bd_ref[0])


def _combine_kernel(y0_ref, y1_ref, y2_ref, y3_ref, gate_ref, x_ref, g_ref, b_ref, o_ref):
    gate = gate_ref[...]
    ff = (gate[:, 0:1] * _load_packed(y0_ref) + gate[:, 1:2] * _load_packed(y1_ref)
          + gate[:, 2:3] * _load_packed(y2_ref) + gate[:, 3:4] * _load_packed(y3_ref))
    o_ref[...] = _layer_norm(DEEPNORM_ALPHA * x_ref[...] + ff, g_ref[...], b_ref[...])


def _moe_and_norm(x1, x1p, idx, gate, pos, cnt, layer, w_gu, b_gu, w_down, b_down, ln_g, ln_b):
    nch, T, C = x1p.shape
    D = x1.shape[1]
    E, K = N_EXPERTS, TOP_K
    dff = w_down.shape[2]
    te = EXPERT_TILE
    n_tiles = (T * K) // te + E
    n_rows = n_tiles * te

    counts = cnt[:, 0].astype(I32)
    padded = (counts + te - 1) // te * te
    pad_end = jnp.cumsum(padded)
    pad_start = pad_end - padded
    experts = jnp.arange(E, dtype=I32)
    start_of = jnp.sum(jnp.where(idx[None] == experts[:, None, None],
                                 pad_start[:, None, None], 0), axis=0)
    dest = start_of + pos
    tile_start = jnp.arange(n_tiles, dtype=I32) * te
    tile_expert = jnp.minimum(
        jnp.sum((pad_end[None, :] <= tile_start[:, None]).astype(I32), axis=1), E - 1)
    n_used = (pad_end[-1] // te).astype(I32).reshape(1)

    x_rows = _sc_scatter_rows(x1p, dest, n_rows)

    half = MXU_DIM // 2
    src = jnp.arange(MXU_DIM)
    perm = (src[:, None] == jnp.where(src < half, 2 * src, 2 * (src - half) + 1)[None, :])
    perm = perm.astype(BF16)
    bg = b_gu[:, 0::2].reshape(E, 1, dff)
    bl = b_gu[:, 1::2].reshape(E, 1, dff)
    bd = b_down.reshape(E, 1, D)

    def row_map(i, te_ref, nu_ref):
        return (0, jnp.minimum(i, nu_ref[0] - 1), 0)

    def exp_map(i, te_ref, nu_ref):
        return (te_ref[i], 0, 0)

    def stacked_map(i, te_ref, nu_ref):
        return (layer, te_ref[i], 0, 0)

    y_rows = pl.pallas_call(
        _expert_kernel,
        grid_spec=pltpu.PrefetchScalarGridSpec(
            num_scalar_prefetch=2,
            grid=(n_tiles,),
            in_specs=[pl.BlockSpec((nch, te, C), row_map),
                      pl.BlockSpec((1, 1, D, 2 * dff), stacked_map),
                      pl.BlockSpec((1, 1, dff), exp_map),
                      pl.BlockSpec((1, 1, dff), exp_map),
                      pl.BlockSpec((1, 1, dff, D), stacked_map),
                      pl.BlockSpec((1, 1, D), exp_map),
                      pl.BlockSpec((MXU_DIM, MXU_DIM), lambda i, a, b: (0, 0))],
            out_specs=pl.BlockSpec((nch, te, C), row_map),
            scratch_shapes=[pltpu.VMEM((D, dff), BF16), pltpu.VMEM((D, dff), BF16),
                            pltpu.VMEM((dff, D), BF16)],
        ),
        out_shape=jax.ShapeDtypeStruct((nch, n_rows, C), U32),
        compiler_params=_params("arbitrary"),
        name="moe_experts",
    )(tile_expert, n_used, x_rows, w_gu, bg, bl, w_down, bd, perm)

    y_tok = _sc_gather_rows(y_rows, dest)

    tm = TOKEN_TILE
    gate_col = gate.T
    slot = lambda k: pl.BlockSpec((nch, None, tm, C), lambda i, k=k: (0, k, i, 0))
    vec = pl.BlockSpec((1, D), lambda i: (0, 0))
    return pl.pallas_call(
        _combine_kernel,
        grid=(T // tm,),
        in_specs=[slot(0), slot(1), slot(2), slot(3),
                  pl.BlockSpec((tm, K), lambda i: (i, 0)),
                  pl.BlockSpec((tm, D), lambda i: (i, 0)), vec, vec],
        out_specs=pl.BlockSpec((tm, D), lambda i: (i, 0)),
        out_shape=jax.ShapeDtypeStruct((T, D), F32),
        compiler_params=_params("parallel"),
        name="moe_combine_norm",
    )(y_tok, y_tok, y_tok, y_tok, gate_col, x1, ln_g.reshape(1, D), ln_b.reshape(1, D))


def kernel(x, positions, ln_gain, ln_bias, mlstm_w_in, mlstm_b_gates, mlstm_norm_gain,
           mlstm_w_out, mla_w_in, mla_q_norm, mla_kv_norm, mla_w_qb, mla_w_kvb, mla_w_out,
           moe_w_router, moe_b_router, moe_w_gate_up, moe_b_gate_up, moe_w_down, moe_b_down):
    B, S, D = x.shape
    T = B * S
    x2d = x.reshape(T, D)

    inv_freq = ROPE_THETA ** (-jnp.arange(0, MLA_ROPE, 2, dtype=F32) / MLA_ROPE)
    ang = positions.astype(F32).reshape(T, 1) * inv_freq
    reps = LANES // ang.shape[1]
    cos_t = jnp.tile(jnp.cos(ang), (1, reps))
    sin_t = jnp.tile(jnp.sin(ang), (1, reps))

    for layer in range(DEPTH):
        j = layer // 2
        if layer % 2 == 0:
            h = _mlstm_mixer(x2d, B, mlstm_w_in[j], mlstm_b_gates[j], mlstm_norm_gain[j])
            w_out = mlstm_w_out[j]
        else:
            h = _mla_mixer(x2d, B, cos_t, sin_t, mla_w_in[j], mla_q_norm[j], mla_kv_norm[j],
                           mla_w_qb[j], mla_w_kvb[j])
            w_out = mla_w_out[j]
        x1, x1p, idx, gate, pos, cnt = _post_mixer(h, x2d, w_out, ln_gain[layer, 0], ln_bias[layer, 0],
                                                   moe_w_router[layer], moe_b_router[layer])
        x2d = _moe_and_norm(x1, x1p, idx, gate, pos, cnt, layer, moe_w_gate_up, moe_b_gate_up[layer],
                            moe_w_down, moe_b_down[layer], ln_gain[layer, 1], ln_bias[layer, 1])
    return x2d.reshape(B, S, D)
```

```python
import functools

import jax
import jax.numpy as jnp
from jax import lax
from jax.experimental import pallas as pl
from jax.experimental.pallas import tpu as pltpu
from jax.experimental.pallas import tpu_sc as plsc

F32 = jnp.float32
BF16 = jnp.bfloat16
I32 = jnp.int32
U32 = jnp.uint32

DEPTH = 4
MLSTM_HEADS = 8
MLSTM_DQK = 64
MLSTM_DV = 128
MLA_HEADS = 8
MLA_Q_LORA = 384
MLA_KV_LORA = 256
MLA_NOPE = 128
MLA_ROPE = 64
MLA_V = 128
ROPE_THETA = 10000.0
N_EXPERTS = 32
TOP_K = 4
SWIGLU_LIMIT = 7.0
SWIGLU_ALPHA = 1.702
DEEPNORM_ALPHA = (2.0 * DEPTH) ** 0.25
LN_EPS = 1e-5
LOG2_E = 1.4426950408889634
RMS_EPS = 1e-6

LANES = 128
BF16_SUBLANES = 16
MXU_DIM = 256
VMEM_LIMIT_BYTES = 56 * 1024 * 1024

TOKEN_TILE = 512
MLSTM_SEQ_BLOCK = 1024
MLSTM_CHUNK = 256
ATTN_TILE = 512
ATTN_HEADS_PER_STEP = 2
ATTN_QUERY_TILES_PER_STEP = 2
EXPERT_TILE = 512
TOKEN_STREAMS = 2
SC_WINDOW = 128
SC_COLS = 256

_NT = (((1,), (1,)), ((), ()))


def _params(*sem):
    return pltpu.CompilerParams(dimension_semantics=sem, vmem_limit_bytes=VMEM_LIMIT_BYTES)


def _layer_norm(z, g, b):
    mu = jnp.mean(z, axis=-1, keepdims=True)
    zc = z - mu
    var = jnp.mean(zc * zc, axis=-1, keepdims=True)
    return zc * lax.rsqrt(var + LN_EPS) * g + b


def _rms_norm(z, g):
    return z * lax.rsqrt(jnp.mean(z * z, axis=-1, keepdims=True) + RMS_EPS) * g


def _store_packed(ref, a):
    half = a.shape[1] // 2
    rounded = lambda v: lax.bitcast_convert_type(v.astype(BF16).astype(F32), U32)
    words = (rounded(a[:, :half]) >> 16) | (rounded(a[:, half:]) & jnp.uint32(0xFFFF0000))
    for c in range(ref.shape[0]):
        ref[c] = words[:, c * SC_COLS:(c + 1) * SC_COLS]


def _load_packed(ref):
    chunks = [ref[c] for c in range(ref.shape[0])]
    lo = [lax.bitcast_convert_type(w << 16, F32) for w in chunks]
    hi = [lax.bitcast_convert_type(w & jnp.uint32(0xFFFF0000), F32) for w in chunks]
    return jnp.concatenate(lo + hi, axis=1)


def _split3(a):
    hi = a.astype(BF16)
    r = a - hi.astype(F32)
    mid = r.astype(BF16)
    lo = (r - mid.astype(F32)).astype(BF16)
    return hi, mid, lo


def _mlstm_inproj_kernel(x_ref, wq_ref, wkt_ref, wv_ref, wo_ref, wg_ref, bg_ref,
                         q_ref, kt_ref, v_ref, o_ref, gc_ref, gr_ref):
    xb = x_ref[...].astype(BF16)
    dot = functools.partial(jnp.dot, preferred_element_type=F32)
    q_ref[...] = (dot(xb, wq_ref[...]) * (MLSTM_DQK ** -0.5)).astype(BF16)
    kt_ref[...] = lax.dot_general(wkt_ref[...], xb, _NT, preferred_element_type=F32).astype(BF16)
    v_ref[...] = dot(xb, wv_ref[...]).astype(BF16)
    o_ref[...] = dot(xb, wo_ref[...])
    z = dot(xb, wg_ref[...]) + bg_ref[...]
    lane = lax.broadcasted_iota(I32, z.shape, 1)
    log_sig = jnp.minimum(z, 0.0) - jnp.log1p(jnp.exp(-jnp.abs(z)))
    g = jnp.where(lane < MLSTM_HEADS, z, log_sig)
    gc_ref[...] = g
    gr_ref[...] = g.T[:2 * MLSTM_HEADS, :]


def _mlstm_cell_kernel(q_ref, kt_ref, v_ref, o_ref, gc_ref, gr_ref, ng_ref, out_ref,
                       c_ref, m_ref, *, chunk, n_chunks):
    H, dk, dv = MLSTM_HEADS, MLSTM_DQK, MLSTM_DV

    @pl.when(pl.program_id(1) == 0)
    def _():
        c_ref[...] = jnp.zeros_like(c_ref)
        m_ref[...] = jnp.zeros_like(m_ref)

    row = lax.broadcasted_iota(I32, (chunk, chunk), 0)
    col = lax.broadcasted_iota(I32, (chunk, chunk), 1)
    causal = col <= row
    tri_lower = causal.astype(BF16)
    tri_upper = (row <= col).astype(BF16)
    ones_col = (lax.broadcasted_iota(I32, (chunk, LANES), 1) == 0).astype(BF16)
    dot = functools.partial(jnp.dot, preferred_element_type=F32)

    def chunk_body(c, carry):
        r0 = pl.multiple_of(c * chunk, chunk)
        rows = pl.ds(r0, chunk)
        gc = gc_ref[rows, :]
        gr = gr_ref[:, rows]
        bc = sum(dot(tri_lower, p) for p in _split3(gc))
        br = sum(dot(p, tri_upper) for p in _split3(gr))
        hs = range(H)
        b_c = [bc[:, H + h:H + h + 1] for h in hs]
        ig_r = [gr[h:h + 1, :] for h in hs]
        b_r = [br[H + h:H + h + 1, :] for h in hs]
        b_last = [b[chunk - 1:chunk, :] for b in b_c]
        m_lanes = [m_ref[h:h + 1, :] for h in hs]
        m_old = [m[:, 0:1] for m in m_lanes]
        qh = [q_ref[rows, h * dk:(h + 1) * dk] for h in hs]
        kth = [kt_ref[h * dk:(h + 1) * dk, rows] for h in hs]
        vh = [v_ref[rows, h * dv:(h + 1) * dv] for h in hs]
        ct = [c_ref[h] for h in hs]

        qk = [dot(qh[h], kth[h]) for h in hs]
        qc = [dot(qh[h], ct[h].astype(BF16)) for h in hs]

        m_new = [jnp.maximum(b_last[h] + m_old[h],
                             jnp.max(b_last[h] - b_r[h] + ig_r[h], axis=1, keepdims=True)) for h in hs]
        ktw = [(kth[h].astype(F32) * jnp.exp(b_last[h] - b_r[h] + ig_r[h] - m_new[h])).astype(BF16)
               for h in hs]
        for h in hs:
            vaug = jnp.concatenate([vh[h], ones_col], axis=1)
            c_ref[h] = jnp.exp(b_last[h] + m_old[h] - m_new[h]) * ct[h] + dot(ktw[h], vaug)
            m_ref[h:h + 1, :] = jnp.broadcast_to(m_new[h], (1, LANES))

        a_mat = [jnp.where(causal, ig_r[h] - b_r[h], -jnp.inf) for h in hs]
        m_rep = [jnp.broadcast_to(m_lanes[h], (chunk, LANES)) for h in hs]
        g = [jnp.maximum(m_rep[h], jnp.max(a_mat[h], axis=1, keepdims=True)) for h in hs]
        s = [qk[h] * jnp.exp(a_mat[h] - jnp.concatenate([g[h]] * (chunk // LANES), axis=1))
             for h in hs]
        w_inter = [jnp.exp(m_rep[h] - g[h]) for h in hs]
        nd = [dot(s[h].astype(BF16), jnp.concatenate([vh[h], ones_col], axis=1))
              + jnp.concatenate([w_inter[h]] * (2 * dv // LANES), axis=1) * qc[h]
              for h in hs]
        for h in hs:
            num, den = nd[h][:, :dv], nd[h][:, dv:dv + 1]
            r = 1.0 / jnp.maximum(jnp.abs(den), jnp.exp(-(b_c[h] + g[h][:, 0:1])))
            scale = r * lax.rsqrt(r * r * jnp.mean(num * num, axis=1, keepdims=True) + RMS_EPS)
            og = o_ref[rows, h * dv:(h + 1) * dv]
            out_ref[rows, h * dv:(h + 1) * dv] = (
                jax.nn.sigmoid(og) * (num * scale * ng_ref[:, h * dv:(h + 1) * dv])
            ).astype(out_ref.dtype)
        return carry

    lax.fori_loop(0, n_chunks, chunk_body, 0)


def _mlstm_mixer(x2d, batch, w_in, b_gates, norm_gain):
    T, D = x2d.shape
    H, dk, dv = MLSTM_HEADS, MLSTM_DQK, MLSTM_DV
    seq = T // batch
    tm = TOKEN_TILE
    cq, ck, cv, co = H * dk, 2 * H * dk, 2 * H * dk + H * dv, 2 * H * dk + 2 * H * dv
    wq = w_in[:, :cq].astype(BF16)
    wkt = w_in[:, cq:ck].T.astype(BF16)
    wv = w_in[:, ck:cv].astype(BF16)
    wo = w_in[:, cv:co].astype(BF16)
    wg = jnp.pad(w_in[:, co:], ((0, 0), (0, LANES - 2 * H))).astype(BF16)
    bg = jnp.pad(b_gates, (0, LANES - 2 * H)).reshape(1, LANES)

    full = lambda a: pl.BlockSpec(a.shape, lambda i: (0,) * a.ndim)
    q, kt, v, o, gc, gr = pl.pallas_call(
        _mlstm_inproj_kernel,
        grid=(T // tm,),
        in_specs=[pl.BlockSpec((tm, D), lambda i: (i, 0)), full(wq), full(wkt), full(wv), full(wo),
                  full(wg), full(bg)],
        out_specs=[pl.BlockSpec((tm, H * dk), lambda i: (i, 0)),
                   pl.BlockSpec((H * dk, tm), lambda i: (0, i)),
                   pl.BlockSpec((tm, H * dv), lambda i: (i, 0)),
                   pl.BlockSpec((tm, H * dv), lambda i: (i, 0)),
                   pl.BlockSpec((tm, LANES), lambda i: (i, 0)),
                   pl.BlockSpec((2 * H, tm), lambda i: (0, i))],
        out_shape=[jax.ShapeDtypeStruct((T, H * dk), BF16),
                   jax.ShapeDtypeStruct((H * dk, T), BF16),
                   jax.ShapeDtypeStruct((T, H * dv), BF16),
                   jax.ShapeDtypeStruct((T, H * dv), F32),
                   jax.ShapeDtypeStruct((T, LANES), F32),
                   jax.ShapeDtypeStruct((2 * H, T), F32)],
        compiler_params=_params("parallel"),
        name="mlstm_inproj",
    )(x2d, wq, wkt, wv, wo, wg, bg)

    ts = min(MLSTM_SEQ_BLOCK, seq)
    chunk = min(MLSTM_CHUNK, ts)
    nsb = seq // ts
    ng = norm_gain.reshape(1, H * dv)
    return pl.pallas_call(
        functools.partial(_mlstm_cell_kernel, chunk=chunk, n_chunks=ts // chunk),
        grid=(batch, nsb),
        in_specs=[pl.BlockSpec((ts, H * dk), lambda b, s: (b * nsb + s, 0)),
                  pl.BlockSpec((H * dk, ts), lambda b, s: (0, b * nsb + s)),
                  pl.BlockSpec((ts, H * dv), lambda b, s: (b * nsb + s, 0)),
                  pl.BlockSpec((ts, H * dv), lambda b, s: (b * nsb + s, 0)),
                  pl.BlockSpec((ts, LANES), lambda b, s: (b * nsb + s, 0)),
                  pl.BlockSpec((2 * H, ts), lambda b, s: (0, b * nsb + s)),
                  pl.BlockSpec((1, H * dv), lambda b, s: (0, 0))],
        out_specs=pl.BlockSpec((ts, H * dv), lambda b, s: (b * nsb + s, 0)),
        out_shape=jax.ShapeDtypeStruct((T, H * dv), BF16),
        scratch_shapes=[pltpu.VMEM((H, dk, 2 * dv), F32), pltpu.VMEM((H, LANES), F32)],
        compiler_params=_params("parallel", "arbitrary"),
        name="mlstm_cell",
    )(q, kt, v, o, gc, gr, ng)


def _mla_proj_kernel(x_ref, cos_ref, sin_ref, wcq_ref, wckv_ref, wkr_ref, qn_ref, kvn_ref,
                     wq_ref, wqr_ref, wkn_ref, wvt_ref, q_ref, k_ref, vt_ref):
    H, dn = MLA_HEADS, MLA_NOPE
    dot = functools.partial(jnp.dot, preferred_element_type=F32)
    xb = x_ref[...].astype(BF16)
    cos, sin = cos_ref[...], sin_ref[...]
    c_q = _rms_norm(dot(xb, wcq_ref[...]), qn_ref[...]).astype(BF16)
    c_kv = _rms_norm(dot(xb, wckv_ref[...]), kvn_ref[...]).astype(BF16)
    kr2 = dot(xb, wkr_ref[...])
    kr = (kr2[:, :LANES] * cos + kr2[:, LANES:] * sin).astype(BF16)
    scale = (MLA_NOPE + MLA_ROPE) ** -0.5 * LOG2_E
    qa = dot(c_q, wq_ref[...])
    qr = dot(c_q, wqr_ref[...])
    kn = dot(c_kv, wkn_ref[...])
    for h in range(H):
        base = h * 2 * LANES
        q_ref[h, :, :dn] = (qa[:, base:base + dn] * scale).astype(BF16)
        rope = qa[:, base + dn:base + 2 * LANES] * cos + qr[:, h * LANES:(h + 1) * LANES] * sin
        q_ref[h, :, dn:] = (rope * scale).astype(BF16)
        k_ref[h, :, :dn] = kn[:, h * dn:(h + 1) * dn].astype(BF16)
        k_ref[h, :, dn:] = kr
    vt_ref[...] = lax.dot_general(wvt_ref[...], c_kv, _NT, preferred_element_type=F32).astype(BF16)


def _attn_kernel(q_ref, k_ref, vt_ref, o_ref, m_ref, acc_ref, *, tile, heads, qsub):
    dv = MLA_V
    qi = pl.program_id(2)
    m_ref[...] = jnp.full_like(m_ref, -jnp.inf)
    acc_ref[...] = jnp.zeros_like(acc_ref)
    ones_rows = jnp.ones((acc_ref.shape[1] - dv, tile), BF16)

    def emit(tiles):
        units = [(j, ds, g, u) for j, ds in tiles for g in range(heads) for u in range(qsub)
                 if ds is None or u >= ds]
        keys_of = lambda j: pl.ds(pl.multiple_of(j * tile, tile), tile)
        sts = {}

        def score(i):
            j, ds, g, u = units[i]
            st = lax.dot_general(k_ref[g, keys_of(j), :], q_ref[g, u * tile:(u + 1) * tile, :], _NT,
                                 preferred_element_type=F32)
            if u == ds:
                kpos = lax.broadcasted_iota(I32, st.shape, 0)
                qpos = lax.broadcasted_iota(I32, st.shape, 1)
                st = jnp.where(kpos <= qpos, st, -jnp.inf)
            sts[i] = st

        def softmax_value(i):
            j, ds, g, u = units[i]
            c = g * qsub + u
            st = sts.pop(i)
            m_old = m_ref[c]
            m_new = jnp.maximum(m_old, jnp.max(st, axis=0, keepdims=True))
            p = jnp.exp2((st - m_new).astype(BF16))
            m_ref[c] = m_new
            vt = jnp.concatenate([vt_ref[g * dv:(g + 1) * dv, keys_of(j)], ones_rows], axis=0)
            acc_ref[c] = jnp.exp2(m_old - m_new) * acc_ref[c] + jnp.dot(
                vt, p, preferred_element_type=F32)

        score(0)
        for i in range(len(units)):
            if i + 1 < len(units):
                score(i + 1)
            softmax_value(i)

    def body(i, carry):
        emit([(2 * i, None), (2 * i + 1, None)])
        return carry

    lax.fori_loop(0, qi, body, 0)
    emit([(qsub * qi + u, u) for u in range(qsub)])
    for g in range(heads):
        for u in range(qsub):
            acc = acc_ref[g * qsub + u]
            o_ref[u * tile:(u + 1) * tile, g * dv:(g + 1) * dv] = (
                acc[:dv] * (1.0 / acc[dv:dv + 1])).T.astype(o_ref.dtype)


def _mla_mixer(x2d, batch, cos_t, sin_t, w_in, q_norm, kv_norm, w_qb, w_kvb):
    T, D = x2d.shape
    H, dn, dr, dv = MLA_HEADS, MLA_NOPE, MLA_ROPE, MLA_V
    seq = T // batch
    tm = TOKEN_TILE
    ql, kl = MLA_Q_LORA, MLA_KV_LORA
    half = dr // 2

    def rot(w):
        return jnp.concatenate([-w[..., half:], w[..., :half]], axis=-1)

    wcq = w_in[:, :ql].astype(BF16)
    wckv = w_in[:, ql:ql + kl].astype(BF16)
    wr = w_in[:, ql + kl:]
    zr = jnp.zeros((D, LANES - dr), F32)
    wkr = jnp.concatenate([wr, zr, rot(wr), zr], axis=1).astype(BF16)
    wq3 = w_qb.reshape(ql, H, dn + dr)
    zq = jnp.zeros((ql, H, LANES - dr), F32)
    wq = jnp.concatenate([wq3, zq], axis=2).reshape(ql, H * 2 * LANES).astype(BF16)
    wqr = jnp.concatenate([rot(wq3[:, :, dn:]), zq], axis=2).reshape(ql, H * LANES).astype(BF16)
    wkv3 = w_kvb.reshape(kl, H, dn + dv)
    wkn = wkv3[:, :, :dn].reshape(kl, H * dn).astype(BF16)
    wvt = wkv3[:, :, dn:].reshape(kl, H * dv).T.astype(BF16)
    qn = q_norm.reshape(1, ql)
    kvn = kv_norm.reshape(1, kl)

    full = lambda a: pl.BlockSpec(a.shape, lambda i: (0,) * a.ndim)
    q, k, vt = pl.pallas_call(
        _mla_proj_kernel,
        grid=(T // tm,),
        in_specs=[pl.BlockSpec((tm, D), lambda i: (i, 0)),
                  pl.BlockSpec((tm, LANES), lambda i: (i, 0)),
                  pl.BlockSpec((tm, LANES), lambda i: (i, 0)),
                  full(wcq), full(wckv), full(wkr), full(qn), full(kvn),
                  full(wq), full(wqr), full(wkn), full(wvt)],
        out_specs=[pl.BlockSpec((H, tm, 2 * LANES), lambda i: (0, i, 0)),
                   pl.BlockSpec((H, tm, 2 * LANES), lambda i: (0, i, 0)),
                   pl.BlockSpec((H * dv, tm), lambda i: (0, i))],
        out_shape=[jax.ShapeDtypeStruct((H, T, 2 * LANES), BF16),
                   jax.ShapeDtypeStruct((H, T, 2 * LANES), BF16),
                   jax.ShapeDtypeStruct((H * dv, T), BF16)],
        compiler_params=_params("parallel"),
        name="mla_proj",
    )(x2d, cos_t, sin_t, wcq, wckv, wkr, qn, kvn, wq, wqr, wkn, wvt)

    tile = min(ATTN_TILE, seq)
    hp = ATTN_HEADS_PER_STEP
    qsub = min(ATTN_QUERY_TILES_PER_STEP, seq // tile)
    tq = tile * qsub
    nq = seq // tq
    return pl.pallas_call(
        functools.partial(_attn_kernel, tile=tile, heads=hp, qsub=qsub),
        grid=(batch, H // hp, nq),
        in_specs=[pl.BlockSpec((hp, tq, 2 * LANES), lambda b, h, i: (h, b * nq + i, 0)),
                  pl.BlockSpec((hp, seq, 2 * LANES), lambda b, h, i: (h, b, 0)),
                  pl.BlockSpec((hp * dv, seq), lambda b, h, i: (h, b))],
        out_specs=pl.BlockSpec((tq, hp * dv), lambda b, h, i: (b * nq + i, h)),
        out_shape=jax.ShapeDtypeStruct((T, H * dv), BF16),
        scratch_shapes=[pltpu.VMEM((hp * qsub, 1, tile), F32),
                        pltpu.VMEM((hp * qsub, dv + BF16_SUBLANES, tile), F32)],
        compiler_params=_params("parallel", "parallel", "arbitrary"),
        name="mla_attention",
    )(q, k, vt)


def _post_kernel(h_ref, x_ref, w_ref, g_ref, b_ref, wr_ref, br_ref, tri_ref,
                 xo_ref, xp_ref, idx_ref, gate_ref, pos_ref, cnt_ref, run_ref):
    E = N_EXPERTS
    tm = x_ref.shape[0]

    @pl.when(pl.program_id(0) == 0)
    def _():
        run_ref[...] = jnp.zeros_like(run_ref)

    mix = jnp.dot(h_ref[...], w_ref[...], preferred_element_type=F32)
    x1 = _layer_norm(DEEPNORM_ALPHA * x_ref[...] + mix, g_ref[...], b_ref[...])
    xo_ref[...] = x1
    _store_packed(xp_ref, x1)

    xh = x1.astype(BF16)
    xl = (x1 - xh.astype(F32)).astype(BF16)
    wh, wl = wr_ref[0], wr_ref[1]
    ntdot = lambda a, b: lax.dot_general(a, b, _NT, preferred_element_type=F32)
    logits = ntdot(wh, xh) + ntdot(wh, xl) + ntdot(wl, xh) + br_ref[...]

    e_iota = lax.broadcasted_iota(I32, (E, tm), 0)
    rest = logits
    vals, sels = [], []
    for k in range(TOP_K):
        v = jnp.max(rest, axis=0, keepdims=True)
        ik = jnp.min(jnp.where(rest == v, e_iota, E), axis=0, keepdims=True)
        sel = e_iota == ik
        rest = jnp.where(sel, -jnp.inf, rest)
        vals.append(v)
        sels.append(sel)
        idx_ref[k:k + 1, :] = ik
    ex = [jnp.exp(v - vals[0]) for v in vals]
    inv = 1.0 / sum(ex)
    for k in range(TOP_K):
        gate_ref[k:k + 1, :] = ex[k] * inv

    chosen = functools.reduce(jnp.logical_or, sels)
    onehot = chosen.astype(BF16)
    before = jnp.dot(onehot, tri_ref[...], preferred_element_type=F32)
    run = run_ref[:, 0:1]
    rank = before + run
    for k in range(TOP_K):
        pos_ref[k:k + 1, :] = jnp.sum(jnp.where(sels[k], rank, 0.0), axis=0,
                                      keepdims=True).astype(I32)
    run_new = run + jnp.sum(chosen.astype(F32), axis=1, keepdims=True)
    run_ref[...] = jnp.broadcast_to(run_new, run_ref.shape)
    cnt_ref[...] = jnp.broadcast_to(run_new, cnt_ref.shape)


def _post_mixer(h, x2d, w_out, ln_g, ln_b, w_router, b_router):
    T, D = x2d.shape
    nch = D // (2 * SC_COLS)
    E = N_EXPERTS
    tm = TOKEN_TILE
    wo = w_out.astype(BF16)
    wrt = w_router.T
    wrh = wrt.astype(BF16)
    wr = jnp.stack([wrh, (wrt - wrh.astype(F32)).astype(BF16)])
    full = lambda a: pl.BlockSpec(a.shape, lambda i: (0,) * a.ndim)
    g, b, br = ln_g.reshape(1, D), ln_b.reshape(1, D), b_router.reshape(E, 1)
    pos_ids = jnp.arange(tm, dtype=I32)
    tri = (pos_ids[:, None] < pos_ids[None, :]).astype(BF16)
    rows = TOP_K
    return pl.pallas_call(
        _post_kernel,
        grid=(T // tm,),
        in_specs=[pl.BlockSpec((tm, h.shape[1]), lambda i: (i, 0)),
                  pl.BlockSpec((tm, D), lambda i: (i, 0)),
                  full(wo), full(g), full(b), full(wr), full(br), full(tri)],
        out_specs=[pl.BlockSpec((tm, D), lambda i: (i, 0)),
                   pl.BlockSpec((nch, tm, SC_COLS), lambda i: (0, i, 0)),
                   pl.BlockSpec((rows, tm), lambda i: (0, i)),
                   pl.BlockSpec((rows, tm), lambda i: (0, i)),
                   pl.BlockSpec((rows, tm), lambda i: (0, i)),
                   pl.BlockSpec((E, LANES), lambda i: (0, 0))],
        out_shape=[jax.ShapeDtypeStruct((T, D), F32),
                   jax.ShapeDtypeStruct((nch, T, SC_COLS), U32),
                   jax.ShapeDtypeStruct((rows, T), I32),
                   jax.ShapeDtypeStruct((rows, T), F32),
                   jax.ShapeDtypeStruct((rows, T), I32),
                   jax.ShapeDtypeStruct((E, LANES), F32)],
        scratch_shapes=[pltpu.VMEM((E, LANES), F32)],
        compiler_params=_params("arbitrary"),
        name="post_mixer_router",
    )(h, x2d, wo, g, b, wr, br, tri)


def _sc_mesh():
    return plsc.VectorSubcoreMesh(core_axis_name="core", subcore_axis_name="subcore")


def _sc_scatter_rows(xc, dest, n_rows):
    nch, T, C = xc.shape
    K = dest.shape[0]
    W = SC_WINDOW
    xs = xc.reshape(nch * T, C)
    nb = (nch * T) // W
    offs = (jnp.arange(nch, dtype=I32) * n_rows)[None, :, None]
    idx = (dest[:, None, :] + offs).reshape(K, nch * T)

    @functools.partial(pl.kernel, out_type=jax.ShapeDtypeStruct((nch * n_rows, C), xc.dtype),
                       mesh=_sc_mesh(), scratch_types=[], name="moe_dispatch_scatter")
    def scatter(x_hbm, i_hbm, o_hbm):
        def body(x_vmem, i_vmem):
            for k in range(K):
                pltpu.sync_copy(x_vmem, o_hbm.at[i_vmem.at[k]])

        pltpu.emit_pipeline(
            body,
            grid=(nb,),
            in_specs=[pl.BlockSpec((W, C), lambda g: (g, 0)),
                      pl.BlockSpec((K, W), lambda g: (0, g))],
            out_specs=[],
            core_axis_name=("core", "subcore"),
            dimension_semantics=(pltpu.PARALLEL,),
        )(x_hbm, i_hbm)

    return scatter(xs, idx).reshape(nch, n_rows, C)


def _sc_gather_rows(yc, dest):
    nch, n_rows, C = yc.shape
    K, T = dest.shape
    W = SC_WINDOW
    ys = yc.reshape(nch * n_rows, C)
    offs = (jnp.arange(nch, dtype=I32) * n_rows)[:, None, None]
    idx = (dest[None, :, :] + offs).reshape(1, nch * K * T)
    n_sub = nch * K * T

    @functools.partial(pl.kernel, out_type=jax.ShapeDtypeStruct((n_sub, C), yc.dtype),
                       mesh=_sc_mesh(), scratch_types=[], name="moe_combine_gather")
    def gather(t_hbm, i_hbm, o_hbm):
        def body(i_vmem, o_vmem):
            pltpu.sync_copy(t_hbm.at[i_vmem.at[0]], o_vmem)

        pltpu.emit_pipeline(
            body,
            grid=(n_sub // W,),
            in_specs=[pl.BlockSpec((1, W), lambda g: (0, g))],
            out_specs=[pl.BlockSpec((W, C), lambda g: (g, 0))],
            core_axis_name=("core", "subcore"),
            dimension_semantics=(pltpu.PARALLEL,),
        )(i_hbm, o_hbm)

    return gather(ys, idx).reshape(nch, K, T, C)


def _expert_kernel(te_ref, nu_ref, x_ref, wgu_ref, bg_ref, bl_ref, wd_ref, bd_ref, perm_ref,
                   y_ref, wg_s, wl_s, wd_s):
    i = pl.program_id(0)
    active = i < nu_ref[0]
    e = te_ref[i]
    changed = jnp.logical_or(i == 0, e != te_ref[jnp.maximum(i - 1, 0)])
    dot = functools.partial(jnp.dot, preferred_element_type=F32)

    @pl.when(jnp.logical_and(active, changed))
    def _():
        n_blocks = wgu_ref.shape[3] // MXU_DIM
        for blk in range(n_blocks):
            wb = wgu_ref[0, 0, :, blk * MXU_DIM:(blk + 1) * MXU_DIM].astype(BF16)
            wp = dot(wb, perm_ref[...]).astype(BF16)
            wg_s[:, blk * LANES:(blk + 1) * LANES] = wp[:, :LANES]
            wl_s[:, blk * LANES:(blk + 1) * LANES] = wp[:, LANES:]
        wd_s[...] = wd_ref[0, 0].astype(BF16)

    @pl.when(active)
    def _():
        xb = _load_packed(x_ref).astype(BF16)
        g = jnp.minimum(dot(xb, wg_s[...]) + bg_ref[0], SWIGLU_LIMIT)
        lin = jnp.clip(dot(xb, wl_s[...]) + bl_ref[0], -SWIGLU_LIMIT, SWIGLU_LIMIT)
        act = (lin + 1.0) * g * jax.nn.sigmoid(SWIGLU_ALPHA * g)
        _store_packed(y_ref, dot(act.astype(BF16), wd_s[...]) + bd_ref[0])


def _combine_kernel(y0_ref, y1_ref, y2_ref, y3_ref, gate_ref, x_ref, g_ref, b_ref, o_ref):
    gate = gate_ref[...]
    ff = (gate[:, 0:1] * _load_packed(y0_ref) + gate[:, 1:2] * _load_packed(y1_ref)
          + gate[:, 2:3] * _load_packed(y2_ref) + gate[:, 3:4] * _load_packed(y3_ref))
    o_ref[...] = _layer_norm(DEEPNORM_ALPHA * x_ref[...] + ff, g_ref[...], b_ref[...])


def _moe_dispatch(x1p, idx, pos, cnt):
    nch, T, C = x1p.shape
    E, K = N_EXPERTS, TOP_K
    te = EXPERT_TILE
    n_tiles = (T * K) // te + E
    n_rows = n_tiles * te

    counts = cnt[:, 0].astype(I32)
    padded = (counts + te - 1) // te * te
    pad_end = jnp.cumsum(padded)
    pad_start = pad_end - padded
    experts = jnp.arange(E, dtype=I32)
    start_of = jnp.sum(jnp.where(idx[None] == experts[:, None, None],
                                 pad_start[:, None, None], 0), axis=0)
    dest = start_of + pos
    tile_start = jnp.arange(n_tiles, dtype=I32) * te
    tile_expert = jnp.minimum(
        jnp.sum((pad_end[None, :] <= tile_start[:, None]).astype(I32), axis=1), E - 1)
    n_used = (pad_end[-1] // te).astype(I32).reshape(1)

    x_rows = _sc_scatter_rows(x1p, dest, n_rows)
    return x_rows, dest, tile_expert, n_used


def _moe_experts(x_rows, tile_expert, n_used, layer, w_gu, b_gu, w_down, b_down):
    nch, n_rows, C = x_rows.shape
    E = N_EXPERTS
    D, dff = w_down.shape[3], w_down.shape[2]
    te = EXPERT_TILE
    n_tiles = n_rows // te
    half = MXU_DIM // 2
    src = jnp.arange(MXU_DIM)
    perm = (src[:, None] == jnp.where(src < half, 2 * src, 2 * (src - half) + 1)[None, :])
    perm = perm.astype(BF16)
    bg = b_gu[:, 0::2].reshape(E, 1, dff)
    bl = b_gu[:, 1::2].reshape(E, 1, dff)
    bd = b_down.reshape(E, 1, D)

    def row_map(i, te_ref, nu_ref):
        return (0, jnp.minimum(i, nu_ref[0] - 1), 0)

    def exp_map(i, te_ref, nu_ref):
        return (te_ref[i], 0, 0)

    def stacked_map(i, te_ref, nu_ref):
        return (layer, te_ref[i], 0, 0)

    return pl.pallas_call(
        _expert_kernel,
        grid_spec=pltpu.PrefetchScalarGridSpec(
            num_scalar_prefetch=2,
            grid=(n_tiles,),
            in_specs=[pl.BlockSpec((nch, te, C), row_map),
                      pl.BlockSpec((1, 1, D, 2 * dff), stacked_map),
                      pl.BlockSpec((1, 1, dff), exp_map),
                      pl.BlockSpec((1, 1, dff), exp_map),
                      pl.BlockSpec((1, 1, dff, D), stacked_map),
                      pl.BlockSpec((1, 1, D), exp_map),
                      pl.BlockSpec((MXU_DIM, MXU_DIM), lambda i, a, b: (0, 0))],
            out_specs=pl.BlockSpec((nch, te, C), row_map),
            scratch_shapes=[pltpu.VMEM((D, dff), BF16), pltpu.VMEM((D, dff), BF16),
                            pltpu.VMEM((dff, D), BF16)],
        ),
        out_shape=jax.ShapeDtypeStruct((nch, n_rows, C), U32),
        compiler_params=_params("arbitrary"),
        name="moe_experts",
    )(tile_expert, n_used, x_rows, w_gu, bg, bl, w_down, bd, perm)


def _moe_combine(y_tok, gate, x1, ln_g, ln_b):
    nch, K, T, C = y_tok.shape
    D = x1.shape[1]
    tm = TOKEN_TILE
    gate_col = gate.T
    slot = lambda k: pl.BlockSpec((nch, None, tm, C), lambda i, k=k: (0, k, i, 0))
    vec = pl.BlockSpec((1, D), lambda i: (0, 0))
    return pl.pallas_call(
        _combine_kernel,
        grid=(T // tm,),
        in_specs=[slot(0), slot(1), slot(2), slot(3),
                  pl.BlockSpec((tm, K), lambda i: (i, 0)),
                  pl.BlockSpec((tm, D), lambda i: (i, 0)), vec, vec],
        out_specs=pl.BlockSpec((tm, D), lambda i: (i, 0)),
        out_shape=jax.ShapeDtypeStruct((T, D), F32),
        compiler_params=_params("parallel"),
        name="moe_combine_norm",
    )(y_tok, y_tok, y_tok, y_tok, gate_col, x1, ln_g.reshape(1, D), ln_b.reshape(1, D))


def kernel(x, positions, ln_gain, ln_bias, mlstm_w_in, mlstm_b_gates, mlstm_norm_gain,
           mlstm_w_out, mla_w_in, mla_q_norm, mla_kv_norm, mla_w_qb, mla_w_kvb, mla_w_out,
           moe_w_router, moe_b_router, moe_w_gate_up, moe_b_gate_up, moe_w_down, moe_b_down):
    B, S, D = x.shape
    T = B * S
    x2d = x.reshape(T, D)

    inv_freq = ROPE_THETA ** (-jnp.arange(0, MLA_ROPE, 2, dtype=F32) / MLA_ROPE)
    ang = positions.astype(F32).reshape(T, 1) * inv_freq
    reps = LANES // ang.shape[1]
    cos_t = jnp.tile(jnp.cos(ang), (1, reps))
    sin_t = jnp.tile(jnp.sin(ang), (1, reps))

    ns = TOKEN_STREAMS if B % TOKEN_STREAMS == 0 else 1
    bs, ts = B // ns, T // ns
    streams = range(ns)
    xs = [x2d[i * ts:(i + 1) * ts] for i in streams]
    cos_s = [cos_t[i * ts:(i + 1) * ts] for i in streams]
    sin_s = [sin_t[i * ts:(i + 1) * ts] for i in streams]

    for layer in range(DEPTH):
        j = layer // 2
        if layer % 2 == 0:
            hs = [_mlstm_mixer(xs[i], bs, mlstm_w_in[j], mlstm_b_gates[j], mlstm_norm_gain[j])
                  for i in streams]
            w_out = mlstm_w_out[j]
        else:
            hs = [_mla_mixer(xs[i], bs, cos_s[i], sin_s[i], mla_w_in[j], mla_q_norm[j],
                             mla_kv_norm[j], mla_w_qb[j], mla_w_kvb[j]) for i in streams]
            w_out = mla_w_out[j]
        routed, sent = [], []
        for i in streams:
            routed.append(_post_mixer(hs[i], xs[i], w_out, ln_gain[layer, 0], ln_bias[layer, 0],
                                      moe_w_router[layer], moe_b_router[layer]))
            x1, x1p, idx, gate, pos, cnt = routed[i]
            sent.append(_moe_dispatch(x1p, idx, pos, cnt))
        y_tok = []
        for i in streams:
            x_rows, dest, tile_expert, n_used = sent[i]
            y_rows = _moe_experts(x_rows, tile_expert, n_used, layer, moe_w_gate_up,
                                  moe_b_gate_up[layer], moe_w_down, moe_b_down[layer])
            y_tok.append(_sc_gather_rows(y_rows, dest))
        xs = [_moe_combine(y_tok[i], routed[i][3], routed[i][0], ln_gain[layer, 1], ln_bias[layer, 1])
              for i in streams]
    return jnp.concatenate(xs, axis=0).reshape(B, S, D)
```

```python
import functools

import jax
import jax.numpy as jnp
from jax import lax
from jax.experimental import pallas as pl
from jax.experimental.pallas import tpu as pltpu
from jax.experimental.pallas import tpu_sc as plsc

F32 = jnp.float32
BF16 = jnp.bfloat16
I32 = jnp.int32
U32 = jnp.uint32

DEPTH = 4
MLSTM_HEADS = 8
MLSTM_DQK = 64
MLSTM_DV = 128
MLA_HEADS = 8
MLA_Q_LORA = 384
MLA_KV_LORA = 256
MLA_NOPE = 128
MLA_ROPE = 64
MLA_V = 128
ROPE_THETA = 10000.0
N_EXPERTS = 32
TOP_K = 4
SWIGLU_LIMIT = 7.0
SWIGLU_ALPHA = 1.702
DEEPNORM_ALPHA = (2.0 * DEPTH) ** 0.25
LN_EPS = 1e-5
LOG2_E = 1.4426950408889634
RMS_EPS = 1e-6

LANES = 128
BF16_SUBLANES = 16
MXU_DIM = 256
VMEM_LIMIT_BYTES = 56 * 1024 * 1024

TOKEN_TILE = 512
MLSTM_SEQ_BLOCK = 1024
MLSTM_CHUNK = 256
ATTN_TILE = 512
ATTN_HEADS_PER_STEP = 2
ATTN_QUERY_TILES_PER_STEP = 2
ATTN_SCORE_LOOKAHEAD = 1
EXPERT_TILE = 512
TOKEN_STREAMS = 1
SC_WINDOW = 128
SC_COLS = 256

_NT = (((1,), (1,)), ((), ()))


def _params(*sem):
    return pltpu.CompilerParams(dimension_semantics=sem, vmem_limit_bytes=VMEM_LIMIT_BYTES)


def _layer_norm(z, g, b):
    mu = jnp.mean(z, axis=-1, keepdims=True)
    zc = z - mu
    var = jnp.mean(zc * zc, axis=-1, keepdims=True)
    return zc * lax.rsqrt(var + LN_EPS) * g + b


def _rms_norm(z, g):
    return z * lax.rsqrt(jnp.mean(z * z, axis=-1, keepdims=True) + RMS_EPS) * g


def _store_packed(ref, a):
    half = a.shape[1] // 2
    rounded = lambda v: lax.bitcast_convert_type(v.astype(BF16).astype(F32), U32)
    words = (rounded(a[:, :half]) >> 16) | (rounded(a[:, half:]) & jnp.uint32(0xFFFF0000))
    for c in range(ref.shape[0]):
        ref[c] = words[:, c * SC_COLS:(c + 1) * SC_COLS]


def _load_packed(ref):
    chunks = [ref[c] for c in range(ref.shape[0])]
    lo = [lax.bitcast_convert_type(w << 16, F32) for w in chunks]
    hi = [lax.bitcast_convert_type(w & jnp.uint32(0xFFFF0000), F32) for w in chunks]
    return jnp.concatenate(lo + hi, axis=1)


def _split3(a):
    hi = a.astype(BF16)
    r = a - hi.astype(F32)
    mid = r.astype(BF16)
    lo = (r - mid.astype(F32)).astype(BF16)
    return hi, mid, lo


def _mlstm_inproj_kernel(x_ref, wq_ref, wkt_ref, wv_ref, wo_ref, wg_ref, bg_ref,
                         q_ref, kt_ref, v_ref, o_ref, gc_ref, gr_ref):
    xb = x_ref[...].astype(BF16)
    dot = functools.partial(jnp.dot, preferred_element_type=F32)
    q_ref[...] = (dot(xb, wq_ref[...]) * (MLSTM_DQK ** -0.5)).astype(BF16)
    kt_ref[...] = lax.dot_general(wkt_ref[...], xb, _NT, preferred_element_type=F32).astype(BF16)
    v_ref[...] = dot(xb, wv_ref[...]).astype(BF16)
    o_ref[...] = dot(xb, wo_ref[...])
    z = dot(xb, wg_ref[...]) + bg_ref[...]
    lane = lax.broadcasted_iota(I32, z.shape, 1)
    log_sig = jnp.minimum(z, 0.0) - jnp.log1p(jnp.exp(-jnp.abs(z)))
    g = jnp.where(lane < MLSTM_HEADS, z, log_sig)
    gc_ref[...] = g
    gr_ref[...] = g.T[:2 * MLSTM_HEADS, :]


def _mlstm_cell_kernel(q_ref, kt_ref, v_ref, o_ref, gc_ref, gr_ref, ng_ref, out_ref,
                       c_ref, m_ref, *, chunk, n_chunks):
    H, dk, dv = MLSTM_HEADS, MLSTM_DQK, MLSTM_DV

    @pl.when(pl.program_id(1) == 0)
    def _():
        c_ref[...] = jnp.zeros_like(c_ref)
        m_ref[...] = jnp.zeros_like(m_ref)

    row = lax.broadcasted_iota(I32, (chunk, chunk), 0)
    col = lax.broadcasted_iota(I32, (chunk, chunk), 1)
    causal = col <= row
    tri_lower = causal.astype(BF16)
    tri_upper = (row <= col).astype(BF16)
    ones_col = (lax.broadcasted_iota(I32, (chunk, LANES), 1) == 0).astype(BF16)
    dot = functools.partial(jnp.dot, preferred_element_type=F32)

    def chunk_body(c, carry):
        r0 = pl.multiple_of(c * chunk, chunk)
        rows = pl.ds(r0, chunk)
        gc = gc_ref[rows, :]
        gr = gr_ref[:, rows]
        bc = sum(dot(tri_lower, p) for p in _split3(gc))
        br = sum(dot(p, tri_upper) for p in _split3(gr))
        hs = range(H)
        b_c = [bc[:, H + h:H + h + 1] for h in hs]
        ig_r = [gr[h:h + 1, :] for h in hs]
        b_r = [br[H + h:H + h + 1, :] for h in hs]
        b_last = [b[chunk - 1:chunk, :] for b in b_c]
        m_lanes = [m_ref[h:h + 1, :] for h in hs]
        m_old = [m[:, 0:1] for m in m_lanes]
        qh = [q_ref[rows, h * dk:(h + 1) * dk] for h in hs]
        kth = [kt_ref[h * dk:(h + 1) * dk, rows] for h in hs]
        vh = [v_ref[rows, h * dv:(h + 1) * dv] for h in hs]
        ct = [c_ref[h] for h in hs]

        qk = [dot(qh[h], kth[h]) for h in hs]
        qc = [dot(qh[h], ct[h].astype(BF16)) for h in hs]

        m_new = [jnp.maximum(b_last[h] + m_old[h],
                             jnp.max(b_last[h] - b_r[h] + ig_r[h], axis=1, keepdims=True)) for h in hs]
        ktw = [(kth[h].astype(F32) * jnp.exp(b_last[h] - b_r[h] + ig_r[h] - m_new[h])).astype(BF16)
               for h in hs]
        for h in hs:
            vaug = jnp.concatenate([vh[h], ones_col], axis=1)
            c_ref[h] = jnp.exp(b_last[h] + m_old[h] - m_new[h]) * ct[h] + dot(ktw[h], vaug)
            m_ref[h:h + 1, :] = jnp.broadcast_to(m_new[h], (1, LANES))

        a_mat = [jnp.where(causal, ig_r[h] - b_r[h], -jnp.inf) for h in hs]
        m_rep = [jnp.broadcast_to(m_lanes[h], (chunk, LANES)) for h in hs]
        g = [jnp.maximum(m_rep[h], jnp.max(a_mat[h], axis=1, keepdims=True)) for h in hs]
        s = [qk[h] * jnp.exp(a_mat[h] - jnp.concatenate([g[h]] * (chunk // LANES), axis=1))
             for h in hs]
        w_inter = [jnp.exp(m_rep[h] - g[h]) for h in hs]
        nd = [dot(s[h].astype(BF16), jnp.concatenate([vh[h], ones_col], axis=1))
              + jnp.concatenate([w_inter[h]] * (2 * dv // LANES), axis=1) * qc[h]
              for h in hs]
        for h in hs:
            num, den = nd[h][:, :dv], nd[h][:, dv:dv + 1]
            r = 1.0 / jnp.maximum(jnp.abs(den), jnp.exp(-(b_c[h] + g[h][:, 0:1])))
            scale = r * lax.rsqrt(r * r * jnp.mean(num * num, axis=1, keepdims=True) + RMS_EPS)
            og = o_ref[rows, h * dv:(h + 1) * dv]
            out_ref[rows, h * dv:(h + 1) * dv] = (
                jax.nn.sigmoid(og) * (num * scale * ng_ref[:, h * dv:(h + 1) * dv])
            ).astype(out_ref.dtype)
        return carry

    lax.fori_loop(0, n_chunks, chunk_body, 0)


def _mlstm_mixer(x2d, batch, w_in, b_gates, norm_gain):
    T, D = x2d.shape
    H, dk, dv = MLSTM_HEADS, MLSTM_DQK, MLSTM_DV
    seq = T // batch
    tm = TOKEN_TILE
    cq, ck, cv, co = H * dk, 2 * H * dk, 2 * H * dk + H * dv, 2 * H * dk + 2 * H * dv
    wq = w_in[:, :cq].astype(BF16)
    wkt = w_in[:, cq:ck].T.astype(BF16)
    wv = w_in[:, ck:cv].astype(BF16)
    wo = w_in[:, cv:co].astype(BF16)
    wg = jnp.pad(w_in[:, co:], ((0, 0), (0, LANES - 2 * H))).astype(BF16)
    bg = jnp.pad(b_gates, (0, LANES - 2 * H)).reshape(1, LANES)

    full = lambda a: pl.BlockSpec(a.shape, lambda i: (0,) * a.ndim)
    q, kt, v, o, gc, gr = pl.pallas_call(
        _mlstm_inproj_kernel,
        grid=(T // tm,),
        in_specs=[pl.BlockSpec((tm, D), lambda i: (i, 0)), full(wq), full(wkt), full(wv), full(wo),
                  full(wg), full(bg)],
        out_specs=[pl.BlockSpec((tm, H * dk), lambda i: (i, 0)),
                   pl.BlockSpec((H * dk, tm), lambda i: (0, i)),
                   pl.BlockSpec((tm, H * dv), lambda i: (i, 0)),
                   pl.BlockSpec((tm, H * dv), lambda i: (i, 0)),
                   pl.BlockSpec((tm, LANES), lambda i: (i, 0)),
                   pl.BlockSpec((2 * H, tm), lambda i: (0, i))],
        out_shape=[jax.ShapeDtypeStruct((T, H * dk), BF16),
                   jax.ShapeDtypeStruct((H * dk, T), BF16),
                   jax.ShapeDtypeStruct((T, H * dv), BF16),
                   jax.ShapeDtypeStruct((T, H * dv), F32),
                   jax.ShapeDtypeStruct((T, LANES), F32),
                   jax.ShapeDtypeStruct((2 * H, T), F32)],
        compiler_params=_params("parallel"),
        name="mlstm_inproj",
    )(x2d, wq, wkt, wv, wo, wg, bg)

    ts = min(MLSTM_SEQ_BLOCK, seq)
    chunk = min(MLSTM_CHUNK, ts)
    nsb = seq // ts
    ng = norm_gain.reshape(1, H * dv)
    return pl.pallas_call(
        functools.partial(_mlstm_cell_kernel, chunk=chunk, n_chunks=ts // chunk),
        grid=(batch, nsb),
        in_specs=[pl.BlockSpec((ts, H * dk), lambda b, s: (b * nsb + s, 0)),
                  pl.BlockSpec((H * dk, ts), lambda b, s: (0, b * nsb + s)),
                  pl.BlockSpec((ts, H * dv), lambda b, s: (b * nsb + s, 0)),
                  pl.BlockSpec((ts, H * dv), lambda b, s: (b * nsb + s, 0)),
                  pl.BlockSpec((ts, LANES), lambda b, s: (b * nsb + s, 0)),
                  pl.BlockSpec((2 * H, ts), lambda b, s: (0, b * nsb + s)),
                  pl.BlockSpec((1, H * dv), lambda b, s: (0, 0))],
        out_specs=pl.BlockSpec((ts, H * dv), lambda b, s: (b * nsb + s, 0)),
        out_shape=jax.ShapeDtypeStruct((T, H * dv), BF16),
        scratch_shapes=[pltpu.VMEM((H, dk, 2 * dv), F32), pltpu.VMEM((H, LANES), F32)],
        compiler_params=_params("parallel", "arbitrary"),
        name="mlstm_cell",
    )(q, kt, v, o, gc, gr, ng)


def _mla_proj_kernel(x_ref, cos_ref, sin_ref, wcq_ref, wckv_ref, wkr_ref, qn_ref, kvn_ref,
                     wq_ref, wqr_ref, wkn_ref, wvt_ref, q_ref, k_ref, vt_ref):
    H, dn = MLA_HEADS, MLA_NOPE
    dot = functools.partial(jnp.dot, preferred_element_type=F32)
    xb = x_ref[...].astype(BF16)
    cos, sin = cos_ref[...], sin_ref[...]
    c_q = _rms_norm(dot(xb, wcq_ref[...]), qn_ref[...]).astype(BF16)
    c_kv = _rms_norm(dot(xb, wckv_ref[...]), kvn_ref[...]).astype(BF16)
    kr2 = dot(xb, wkr_ref[...])
    kr = (kr2[:, :LANES] * cos + kr2[:, LANES:] * sin).astype(BF16)
    scale = (MLA_NOPE + MLA_ROPE) ** -0.5 * LOG2_E
    qa = dot(c_q, wq_ref[...])
    qr = dot(c_q, wqr_ref[...])
    kn = dot(c_kv, wkn_ref[...])
    for h in range(H):
        base = h * 2 * LANES
        q_ref[h, :, :dn] = (qa[:, base:base + dn] * scale).astype(BF16)
        rope = qa[:, base + dn:base + 2 * LANES] * cos + qr[:, h * LANES:(h + 1) * LANES] * sin
        q_ref[h, :, dn:] = (rope * scale).astype(BF16)
        k_ref[h, :, :dn] = kn[:, h * dn:(h + 1) * dn].astype(BF16)
        k_ref[h, :, dn:] = kr
    vt_ref[...] = lax.dot_general(wvt_ref[...], c_kv, _NT, preferred_element_type=F32).astype(BF16)


def _attn_kernel(q_ref, k_ref, vt_ref, o_ref, ok_ref, m_ref, mu_ref, acc_ref,
                 *, tile, heads, qsub, lagged):
    dv = MLA_V
    qi = pl.program_id(2)
    m_ref[...] = jnp.full_like(m_ref, -jnp.inf)
    mu_ref[...] = jnp.full_like(mu_ref, -jnp.inf)
    acc_ref[...] = jnp.zeros_like(acc_ref)
    ones_rows = jnp.ones((acc_ref.shape[1] - dv, tile), BF16)

    def emit(tiles):
        units = [(j, ds, ex, g, u) for j, ds, ex in tiles for g in range(heads)
                 for u in range(qsub) if ds is None or u >= ds]
        keys_of = lambda j: pl.ds(pl.multiple_of(j * tile, tile), tile)
        sts = {}

        def score(i):
            j, ds, ex, g, u = units[i]
            st = lax.dot_general(k_ref[g, keys_of(j), :], q_ref[g, u * tile:(u + 1) * tile, :], _NT,
                                 preferred_element_type=F32)
            if u == ds:
                kpos = lax.broadcasted_iota(I32, st.shape, 0)
                qpos = lax.broadcasted_iota(I32, st.shape, 1)
                st = jnp.where(kpos <= qpos, st, -jnp.inf)
            sts[i] = st

        def softmax_value(i):
            j, ds, ex, g, u = units[i]
            c = g * qsub + u
            st = sts.pop(i)
            tile_max = jnp.max(st, axis=0, keepdims=True)
            seen = m_ref[c]
            stab = jnp.maximum(seen, tile_max) if ex else seen
            p = jnp.exp2((st - stab).astype(BF16))
            vt = jnp.concatenate([vt_ref[g * dv:(g + 1) * dv, keys_of(j)], ones_rows], axis=0)
            acc_ref[c] = jnp.exp2(mu_ref[c] - stab) * acc_ref[c] + jnp.dot(
                vt, p, preferred_element_type=F32)
            mu_ref[c] = stab
            m_ref[c] = jnp.maximum(seen, tile_max)

        ahead = ATTN_SCORE_LOOKAHEAD
        for i in range(min(ahead, len(units))):
            score(i)
        for i in range(len(units)):
            if i + ahead < len(units):
                score(i + ahead)
            softmax_value(i)

    ex = not lagged

    @pl.when(qi > 0)
    def _():
        emit([(0, None, True), (1, None, ex)])

    def body(i, carry):
        emit([(2 * i, None, ex), (2 * i + 1, None, ex)])
        return carry

    lax.fori_loop(1, qi, body, 0)

    @pl.when(qi > 0)
    def _():
        emit([(qsub * qi + u, u, ex) for u in range(qsub)])

    @pl.when(qi == 0)
    def _():
        emit([(0, 0, True), (1, 1, ex)])

    for g in range(heads):
        for u in range(qsub):
            c = g * qsub + u
            acc = acc_ref[c]
            o_ref[u * tile:(u + 1) * tile, g * dv:(g + 1) * dv] = (
                acc[:dv] * (1.0 / acc[dv:dv + 1])).T.astype(o_ref.dtype)
            ok_ref[0, c] = jnp.max(jnp.where(acc - acc == 0.0, 0.0, 1.0), axis=0, keepdims=True)


def _mla_mixer(x2d, batch, cos_t, sin_t, w_in, q_norm, kv_norm, w_qb, w_kvb):
    T, D = x2d.shape
    H, dn, dr, dv = MLA_HEADS, MLA_NOPE, MLA_ROPE, MLA_V
    seq = T // batch
    tm = TOKEN_TILE
    ql, kl = MLA_Q_LORA, MLA_KV_LORA
    half = dr // 2

    def rot(w):
        return jnp.concatenate([-w[..., half:], w[..., :half]], axis=-1)

    wcq = w_in[:, :ql].astype(BF16)
    wckv = w_in[:, ql:ql + kl].astype(BF16)
    wr = w_in[:, ql + kl:]
    zr = jnp.zeros((D, LANES - dr), F32)
    wkr = jnp.concatenate([wr, zr, rot(wr), zr], axis=1).astype(BF16)
    wq3 = w_qb.reshape(ql, H, dn + dr)
    zq = jnp.zeros((ql, H, LANES - dr), F32)
    wq = jnp.concatenate([wq3, zq], axis=2).reshape(ql, H * 2 * LANES).astype(BF16)
    wqr = jnp.concatenate([rot(wq3[:, :, dn:]), zq], axis=2).reshape(ql, H * LANES).astype(BF16)
    wkv3 = w_kvb.reshape(kl, H, dn + dv)
    wkn = wkv3[:, :, :dn].reshape(kl, H * dn).astype(BF16)
    wvt = wkv3[:, :, dn:].reshape(kl, H * dv).T.astype(BF16)
    qn = q_norm.reshape(1, ql)
    kvn = kv_norm.reshape(1, kl)

    full = lambda a: pl.BlockSpec(a.shape, lambda i: (0,) * a.ndim)
    q, k, vt = pl.pallas_call(
        _mla_proj_kernel,
        grid=(T // tm,),
        in_specs=[pl.BlockSpec((tm, D), lambda i: (i, 0)),
                  pl.BlockSpec((tm, LANES), lambda i: (i, 0)),
                  pl.BlockSpec((tm, LANES), lambda i: (i, 0)),
                  full(wcq), full(wckv), full(wkr), full(qn), full(kvn),
                  full(wq), full(wqr), full(wkn), full(wvt)],
        out_specs=[pl.BlockSpec((H, tm, 2 * LANES), lambda i: (0, i, 0)),
                   pl.BlockSpec((H, tm, 2 * LANES), lambda i: (0, i, 0)),
                   pl.BlockSpec((H * dv, tm), lambda i: (0, i))],
        out_shape=[jax.ShapeDtypeStruct((H, T, 2 * LANES), BF16),
                   jax.ShapeDtypeStruct((H, T, 2 * LANES), BF16),
                   jax.ShapeDtypeStruct((H * dv, T), BF16)],
        compiler_params=_params("parallel"),
        name="mla_proj",
    )(x2d, cos_t, sin_t, wcq, wckv, wkr, qn, kvn, wq, wqr, wkn, wvt)

    tile = min(ATTN_TILE, seq)
    hp = ATTN_HEADS_PER_STEP
    qsub = min(ATTN_QUERY_TILES_PER_STEP, seq // tile)
    tq = tile * qsub
    nq = seq // tq
    nhp = H // hp
    nc = hp * qsub

    def attention(lagged):
        return pl.pallas_call(
            functools.partial(_attn_kernel, tile=tile, heads=hp, qsub=qsub, lagged=lagged),
            grid=(batch, nhp, nq),
            in_specs=[pl.BlockSpec((hp, tq, 2 * LANES), lambda b, h, i: (h, b * nq + i, 0)),
                      pl.BlockSpec((hp, seq, 2 * LANES), lambda b, h, i: (h, b, 0)),
                      pl.BlockSpec((hp * dv, seq), lambda b, h, i: (h, b))],
            out_specs=[pl.BlockSpec((tq, hp * dv), lambda b, h, i: (b * nq + i, h)),
                       pl.BlockSpec((1, nc, 1, tile), lambda b, h, i: ((b * nhp + h) * nq + i, 0, 0, 0))],
            out_shape=[jax.ShapeDtypeStruct((T, H * dv), BF16),
                       jax.ShapeDtypeStruct((batch * nhp * nq, nc, 1, tile), F32)],
            scratch_shapes=[pltpu.VMEM((nc, 1, tile), F32), pltpu.VMEM((nc, 1, tile), F32),
                            pltpu.VMEM((nc, dv + BF16_SUBLANES, tile), F32)],
            compiler_params=_params("parallel", "parallel", "arbitrary"),
            name="mla_attention" if lagged else "mla_attention_exact",
        )(q, k, vt)

    out, not_finite = attention(lagged=True)
    return lax.cond(jnp.any(not_finite > 0.0), lambda: attention(lagged=False)[0], lambda: out)


def _post_kernel(h_ref, x_ref, w_ref, g_ref, b_ref, wr_ref, br_ref, tri_ref,
                 xo_ref, xp_ref, idx_ref, gate_ref, pos_ref, cnt_ref, run_ref):
    E = N_EXPERTS
    tm = x_ref.shape[0]

    @pl.when(pl.program_id(0) == 0)
    def _():
        run_ref[...] = jnp.zeros_like(run_ref)

    mix = jnp.dot(h_ref[...], w_ref[...], preferred_element_type=F32)
    x1 = _layer_norm(DEEPNORM_ALPHA * x_ref[...] + mix, g_ref[...], b_ref[...])
    xo_ref[...] = x1
    _store_packed(xp_ref, x1)

    xh = x1.astype(BF16)
    xl = (x1 - xh.astype(F32)).astype(BF16)
    wh, wl = wr_ref[0], wr_ref[1]
    ntdot = lambda a, b: lax.dot_general(a, b, _NT, preferred_element_type=F32)
    logits = ntdot(wh, xh) + ntdot(wh, xl) + ntdot(wl, xh) + br_ref[...]

    e_iota = lax.broadcasted_iota(I32, (E, tm), 0)
    rest = logits
    vals, sels = [], []
    for k in range(TOP_K):
        v = jnp.max(rest, axis=0, keepdims=True)
        ik = jnp.min(jnp.where(rest == v, e_iota, E), axis=0, keepdims=True)
        sel = e_iota == ik
        rest = jnp.where(sel, -jnp.inf, rest)
        vals.append(v)
        sels.append(sel)
        idx_ref[k:k + 1, :] = ik
    ex = [jnp.exp(v - vals[0]) for v in vals]
    inv = 1.0 / sum(ex)
    for k in range(TOP_K):
        gate_ref[k:k + 1, :] = ex[k] * inv

    chosen = functools.reduce(jnp.logical_or, sels)
    onehot = chosen.astype(BF16)
    before = jnp.dot(onehot, tri_ref[...], preferred_element_type=F32)
    run = run_ref[:, 0:1]
    rank = before + run
    for k in range(TOP_K):
        pos_ref[k:k + 1, :] = jnp.sum(jnp.where(sels[k], rank, 0.0), axis=0,
                                      keepdims=True).astype(I32)
    run_new = run + jnp.sum(chosen.astype(F32), axis=1, keepdims=True)
    run_ref[...] = jnp.broadcast_to(run_new, run_ref.shape)
    cnt_ref[...] = jnp.broadcast_to(run_new, cnt_ref.shape)


def _post_mixer(h, x2d, w_out, ln_g, ln_b, w_router, b_router):
    T, D = x2d.shape
    nch = D // (2 * SC_COLS)
    E = N_EXPERTS
    tm = TOKEN_TILE
    wo = w_out.astype(BF16)
    wrt = w_router.T
    wrh = wrt.astype(BF16)
    wr = jnp.stack([wrh, (wrt - wrh.astype(F32)).astype(BF16)])
    full = lambda a: pl.BlockSpec(a.shape, lambda i: (0,) * a.ndim)
    g, b, br = ln_g.reshape(1, D), ln_b.reshape(1, D), b_router.reshape(E, 1)
    pos_ids = jnp.arange(tm, dtype=I32)
    tri = (pos_ids[:, None] < pos_ids[None, :]).astype(BF16)
    rows = TOP_K
    return pl.pallas_call(
        _post_kernel,
        grid=(T // tm,),
        in_specs=[pl.BlockSpec((tm, h.shape[1]), lambda i: (i, 0)),
                  pl.BlockSpec((tm, D), lambda i: (i, 0)),
                  full(wo), full(g), full(b), full(wr), full(br), full(tri)],
        out_specs=[pl.BlockSpec((tm, D), lambda i: (i, 0)),
                   pl.BlockSpec((nch, tm, SC_COLS), lambda i: (0, i, 0)),
                   pl.BlockSpec((rows, tm), lambda i: (0, i)),
                   pl.BlockSpec((rows, tm), lambda i: (0, i)),
                   pl.BlockSpec((rows, tm), lambda i: (0, i)),
                   pl.BlockSpec((E, LANES), lambda i: (0, 0))],
        out_shape=[jax.ShapeDtypeStruct((T, D), F32),
                   jax.ShapeDtypeStruct((nch, T, SC_COLS), U32),
                   jax.ShapeDtypeStruct((rows, T), I32),
                   jax.ShapeDtypeStruct((rows, T), F32),
                   jax.ShapeDtypeStruct((rows, T), I32),
                   jax.ShapeDtypeStruct((E, LANES), F32)],
        scratch_shapes=[pltpu.VMEM((E, LANES), F32)],
        compiler_params=_params("arbitrary"),
        name="post_mixer_router",
    )(h, x2d, wo, g, b, wr, br, tri)


def _sc_mesh():
    return plsc.VectorSubcoreMesh(core_axis_name="core", subcore_axis_name="subcore")


def _sc_scatter_rows(xc, dest, n_rows):
    nch, T, C = xc.shape
    K = dest.shape[0]
    W = SC_WINDOW
    xs = xc.reshape(nch * T, C)
    nb = (nch * T) // W
    offs = (jnp.arange(nch, dtype=I32) * n_rows)[None, :, None]
    idx = (dest[:, None, :] + offs).reshape(K, nch * T)

    @functools.partial(pl.kernel, out_type=jax.ShapeDtypeStruct((nch * n_rows, C), xc.dtype),
                       mesh=_sc_mesh(), scratch_types=[], name="moe_dispatch_scatter")
    def scatter(x_hbm, i_hbm, o_hbm):
        def body(x_vmem, i_vmem):
            for k in range(K):
                pltpu.sync_copy(x_vmem, o_hbm.at[i_vmem.at[k]])

        pltpu.emit_pipeline(
            body,
            grid=(nb,),
            in_specs=[pl.BlockSpec((W, C), lambda g: (g, 0)),
                      pl.BlockSpec((K, W), lambda g: (0, g))],
            out_specs=[],
            core_axis_name=("core", "subcore"),
            dimension_semantics=(pltpu.PARALLEL,),
        )(x_hbm, i_hbm)

    return scatter(xs, idx).reshape(nch, n_rows, C)


def _sc_gather_rows(yc, dest):
    nch, n_rows, C = yc.shape
    K, T = dest.shape
    W = SC_WINDOW
    ys = yc.reshape(nch * n_rows, C)
    offs = (jnp.arange(nch, dtype=I32) * n_rows)[:, None, None]
    idx = (dest[None, :, :] + offs).reshape(1, nch * K * T)
    n_sub = nch * K * T

    @functools.partial(pl.kernel, out_type=jax.ShapeDtypeStruct((n_sub, C), yc.dtype),
                       mesh=_sc_mesh(), scratch_types=[], name="moe_combine_gather")
    def gather(t_hbm, i_hbm, o_hbm):
        def body(i_vmem, o_vmem):
            pltpu.sync_copy(t_hbm.at[i_vmem.at[0]], o_vmem)

        pltpu.emit_pipeline(
            body,
            grid=(n_sub // W,),
            in_specs=[pl.BlockSpec((1, W), lambda g: (0, g))],
            out_specs=[pl.BlockSpec((W, C), lambda g: (g, 0))],
            core_axis_name=("core", "subcore"),
            dimension_semantics=(pltpu.PARALLEL,),
        )(i_hbm, o_hbm)

    return gather(ys, idx).reshape(nch, K, T, C)


def _expert_kernel(te_ref, nu_ref, x_ref, wgu_ref, bg_ref, bl_ref, wd_ref, bd_ref, perm_ref,
                   y_ref, wg_s, wl_s, wd_s):
    i = pl.program_id(0)
    active = i < nu_ref[0]
    e = te_ref[i]
    changed = jnp.logical_or(i == 0, e != te_ref[jnp.maximum(i - 1, 0)])
    dot = functools.partial(jnp.dot, preferred_element_type=F32)

    @pl.when(jnp.logical_and(active, changed))
    def _():
        n_blocks = wgu_ref.shape[3] // MXU_DIM
        for blk in range(n_blocks):
            wb = wgu_ref[0, 0, :, blk * MXU_DIM:(blk + 1) * MXU_DIM].astype(BF16)
            wp = dot(wb, perm_ref[...]).astype(BF16)
            wg_s[:, blk * LANES:(blk + 1) * LANES] = wp[:, :LANES]
            wl_s[:, blk * LANES:(blk + 1) * LANES] = wp[:, LANES:]
        wd_s[...] = wd_ref[0, 0].astype(BF16)

    @pl.when(active)
    def _():
        xb = _load_packed(x_ref).astype(BF16)
        g = jnp.minimum(dot(xb, wg_s[...]) + bg_ref[0], SWIGLU_LIMIT)
        lin = jnp.clip(dot(xb, wl_s[...]) + bl_ref[0], -SWIGLU_LIMIT, SWIGLU_LIMIT)
        act = (lin + 1.0) * g * jax.nn.sigmoid(SWIGLU_ALPHA * g)
        _store_packed(y_ref, dot(act.astype(BF16), wd_s[...]) + bd_ref[0])


def _combine_kernel(y0_ref, y1_ref, y2_ref, y3_ref, gate_ref, x_ref, g_ref, b_ref, o_ref):
    gate = gate_ref[...]
    ff = (gate[:, 0:1] * _load_packed(y0_ref) + gate[:, 1:2] * _load_packed(y1_ref)
          + gate[:, 2:3] * _load_packed(y2_ref) + gate[:, 3:4] * _load_packed(y3_ref))
    o_ref[...] = _layer_norm(DEEPNORM_ALPHA * x_ref[...] + ff, g_ref[...], b_ref[...])


def _moe_dispatch(x1p, idx, pos, cnt):
    nch, T, C = x1p.shape
    E, K = N_EXPERTS, TOP_K
    te = EXPERT_TILE
    n_tiles = (T * K) // te + E
    n_rows = n_tiles * te

    counts = cnt[:, 0].astype(I32)
    padded = (counts + te - 1) // te * te
    pad_end = jnp.cumsum(padded)
    pad_start = pad_end - padded
    experts = jnp.arange(E, dtype=I32)
    start_of = jnp.sum(jnp.where(idx[None] == experts[:, None, None],
                                 pad_start[:, None, None], 0), axis=0)
    dest = start_of + pos
    tile_start = jnp.arange(n_tiles, dtype=I32) * te
    tile_expert = jnp.minimum(
        jnp.sum((pad_end[None, :] <= tile_start[:, None]).astype(I32), axis=1), E - 1)
    n_used = (pad_end[-1] // te).astype(I32).reshape(1)

    x_rows = _sc_scatter_rows(x1p, dest, n_rows)
    return x_rows, dest, tile_expert, n_used


def _moe_experts(x_rows, tile_expert, n_used, layer, w_gu, b_gu, w_down, b_down):
    nch, n_rows, C = x_rows.shape
    E = N_EXPERTS
    D, dff = w_down.shape[3], w_down.shape[2]
    te = EXPERT_TILE
    n_tiles = n_rows // te
    half = MXU_DIM // 2
    src = jnp.arange(MXU_DIM)
    perm = (src[:, None] == jnp.where(src < half, 2 * src, 2 * (src - half) + 1)[None, :])
    perm = perm.astype(BF16)
    bg = b_gu[:, 0::2].reshape(E, 1, dff)
    bl = b_gu[:, 1::2].reshape(E, 1, dff)
    bd = b_down.reshape(E, 1, D)

    def row_map(i, te_ref, nu_ref):
        return (0, jnp.minimum(i, nu_ref[0] - 1), 0)

    def exp_map(i, te_ref, nu_ref):
        return (te_ref[i], 0, 0)

    def stacked_map(i, te_ref, nu_ref):
        return (layer, te_ref[i], 0, 0)

    return pl.pallas_call(
        _expert_kernel,
        grid_spec=pltpu.PrefetchScalarGridSpec(
            num_scalar_prefetch=2,
            grid=(n_tiles,),
            in_specs=[pl.BlockSpec((nch, te, C), row_map),
                      pl.BlockSpec((1, 1, D, 2 * dff), stacked_map),
                      pl.BlockSpec((1, 1, dff), exp_map),
                      pl.BlockSpec((1, 1, dff), exp_map),
                      pl.BlockSpec((1, 1, dff, D), stacked_map),
                      pl.BlockSpec((1, 1, D), exp_map),
                      pl.BlockSpec((MXU_DIM, MXU_DIM), lambda i, a, b: (0, 0))],
            out_specs=pl.BlockSpec((nch, te, C), row_map),
            scratch_shapes=[pltpu.VMEM((D, dff), BF16), pltpu.VMEM((D, dff), BF16),
                            pltpu.VMEM((dff, D), BF16)],
        ),
        out_shape=jax.ShapeDtypeStruct((nch, n_rows, C), U32),
        compiler_params=_params("arbitrary"),
        name="moe_experts",
    )(tile_expert, n_used, x_rows, w_gu, bg, bl, w_down, bd, perm)


def _moe_combine(y_tok, gate, x1, ln_g, ln_b):
    nch, K, T, C = y_tok.shape
    D = x1.shape[1]
    tm = TOKEN_TILE
    gate_col = gate.T
    slot = lambda k: pl.BlockSpec((nch, None, tm, C), lambda i, k=k: (0, k, i, 0))
    vec = pl.BlockSpec((1, D), lambda i: (0, 0))
    return pl.pallas_call(
        _combine_kernel,
        grid=(T // tm,),
        in_specs=[slot(0), slot(1), slot(2), slot(3),
                  pl.BlockSpec((tm, K), lambda i: (i, 0)),
                  pl.BlockSpec((tm, D), lambda i: (i, 0)), vec, vec],
        out_specs=pl.BlockSpec((tm, D), lambda i: (i, 0)),
        out_shape=jax.ShapeDtypeStruct((T, D), F32),
        compiler_params=_params("parallel"),
        name="moe_combine_norm",
    )(y_tok, y_tok, y_tok, y_tok, gate_col, x1, ln_g.reshape(1, D), ln_b.reshape(1, D))


def kernel(x, positions, ln_gain, ln_bias, mlstm_w_in, mlstm_b_gates, mlstm_norm_gain,
           mlstm_w_out, mla_w_in, mla_q_norm, mla_kv_norm, mla_w_qb, mla_w_kvb, mla_w_out,
           moe_w_router, moe_b_router, moe_w_gate_up, moe_b_gate_up, moe_w_down, moe_b_down):
    B, S, D = x.shape
    T = B * S
    x2d = x.reshape(T, D)

    inv_freq = ROPE_THETA ** (-jnp.arange(0, MLA_ROPE, 2, dtype=F32) / MLA_ROPE)
    ang = positions.astype(F32).reshape(T, 1) * inv_freq
    reps = LANES // ang.shape[1]
    cos_t = jnp.tile(jnp.cos(ang), (1, reps))
    sin_t = jnp.tile(jnp.sin(ang), (1, reps))

    ns = TOKEN_STREAMS if B % TOKEN_STREAMS == 0 else 1
    bs, ts = B // ns, T // ns
    streams = range(ns)
    xs = [x2d[i * ts:(i + 1) * ts] for i in streams]
    cos_s = [cos_t[i * ts:(i + 1) * ts] for i in streams]
    sin_s = [sin_t[i * ts:(i + 1) * ts] for i in streams]

    for layer in range(DEPTH):
        j = layer // 2
        if layer % 2 == 0:
            hs = [_mlstm_mixer(xs[i], bs, mlstm_w_in[j], mlstm_b_gates[j], mlstm_norm_gain[j])
                  for i in streams]
            w_out = mlstm_w_out[j]
        else:
            hs = [_mla_mixer(xs[i], bs, cos_s[i], sin_s[i], mla_w_in[j], mla_q_norm[j],
                             mla_kv_norm[j], mla_w_qb[j], mla_w_kvb[j]) for i in streams]
            w_out = mla_w_out[j]
        routed, sent = [], []
        for i in streams:
            routed.append(_post_mixer(hs[i], xs[i], w_out, ln_gain[layer, 0], ln_bias[layer, 0],
                                      moe_w_router[layer], moe_b_router[layer]))
            x1, x1p, idx, gate, pos, cnt = routed[i]
            sent.append(_moe_dispatch(x1p, idx, pos, cnt))
        y_tok = []
        for i in streams:
            x_rows, dest, tile_expert, n_used = sent[i]
            y_rows = _moe_experts(x_rows, tile_expert, n_used, layer, moe_w_gate_up,
                                  moe_b_gate_up[layer], moe_w_down, moe_b_down[layer])
            y_tok.append(_sc_gather_rows(y_rows, dest))
        xs = [_moe_combine(y_tok[i], routed[i][3], routed[i][0], ln_gain[layer, 1], ln_bias[layer, 1])
              for i in streams]
    return jnp.concatenate(xs, axis=0).reshape(B, S, D)
```

```python
import functools

import jax
import jax.numpy as jnp
from jax import lax
from jax.experimental import pallas as pl
from jax.experimental.pallas import tpu as pltpu
from jax.experimental.pallas import tpu_sc as plsc

F32 = jnp.float32
BF16 = jnp.bfloat16
I32 = jnp.int32
U32 = jnp.uint32

DEPTH = 4
MLSTM_HEADS = 8
MLSTM_DQK = 64
MLSTM_DV = 128
MLA_HEADS = 8
MLA_Q_LORA = 384
MLA_KV_LORA = 256
MLA_NOPE = 128
MLA_ROPE = 64
MLA_V = 128
ROPE_THETA = 10000.0
N_EXPERTS = 32
TOP_K = 4
SWIGLU_LIMIT = 7.0
SWIGLU_ALPHA = 1.702
DEEPNORM_ALPHA = (2.0 * DEPTH) ** 0.25
LN_EPS = 1e-5
LOG2_E = 1.4426950408889634
RMS_EPS = 1e-6

LANES = 128
BF16_SUBLANES = 16
MXU_DIM = 256
VMEM_LIMIT_BYTES = 56 * 1024 * 1024

TOKEN_TILE = 512
MLSTM_SEQ_BLOCK = 1024
MLSTM_CHUNK = 256
ATTN_TILE = 512
ATTN_HEADS_PER_STEP = 2
ATTN_QUERY_TILES_PER_STEP = 2
ATTN_SCORE_LOOKAHEAD = 1
EXPERT_TILE = 512
TOKEN_STREAMS = 1
SC_WINDOW = 128
SC_COLS = 256

_NT = (((1,), (1,)), ((), ()))


def _params(*sem):
    return pltpu.CompilerParams(dimension_semantics=sem, vmem_limit_bytes=VMEM_LIMIT_BYTES)


def _layer_norm(z, g, b):
    mu = jnp.mean(z, axis=-1, keepdims=True)
    zc = z - mu
    var = jnp.mean(zc * zc, axis=-1, keepdims=True)
    return zc * lax.rsqrt(var + LN_EPS) * g + b


def _rms_norm(z, g):
    return z * lax.rsqrt(jnp.mean(z * z, axis=-1, keepdims=True) + RMS_EPS) * g


def _store_packed(ref, a):
    half = a.shape[1] // 2
    rounded = lambda v: lax.bitcast_convert_type(v.astype(BF16).astype(F32), U32)
    words = (rounded(a[:, :half]) >> 16) | (rounded(a[:, half:]) & jnp.uint32(0xFFFF0000))
    for c in range(ref.shape[0]):
        ref[c] = words[:, c * SC_COLS:(c + 1) * SC_COLS]


def _load_packed(ref):
    chunks = [ref[c] for c in range(ref.shape[0])]
    lo = [lax.bitcast_convert_type(w << 16, F32) for w in chunks]
    hi = [lax.bitcast_convert_type(w & jnp.uint32(0xFFFF0000), F32) for w in chunks]
    return jnp.concatenate(lo + hi, axis=1)


def _split3(a):
    hi = a.astype(BF16)
    r = a - hi.astype(F32)
    mid = r.astype(BF16)
    lo = (r - mid.astype(F32)).astype(BF16)
    return hi, mid, lo


def _mlstm_inproj_kernel(x_ref, wq_ref, wkt_ref, wv_ref, wo_ref, wg_ref, bg_ref,
                         q_ref, kt_ref, v_ref, o_ref, gc_ref, gr_ref):
    xb = x_ref[...].astype(BF16)
    dot = functools.partial(jnp.dot, preferred_element_type=F32)
    q_ref[...] = (dot(xb, wq_ref[...]) * (MLSTM_DQK ** -0.5)).astype(BF16)
    kt_ref[...] = lax.dot_general(wkt_ref[...], xb, _NT, preferred_element_type=F32).astype(BF16)
    v_ref[...] = dot(xb, wv_ref[...]).astype(BF16)
    o_ref[...] = dot(xb, wo_ref[...])
    z = dot(xb, wg_ref[...]) + bg_ref[...]
    lane = lax.broadcasted_iota(I32, z.shape, 1)
    log_sig = jnp.minimum(z, 0.0) - jnp.log1p(jnp.exp(-jnp.abs(z)))
    g = jnp.where(lane < MLSTM_HEADS, z, log_sig)
    gc_ref[...] = g
    gr_ref[...] = g.T[:2 * MLSTM_HEADS, :]


def _mlstm_cell_kernel(q_ref, kt_ref, v_ref, o_ref, gc_ref, gr_ref, ng_ref, out_ref,
                       c_ref, m_ref, *, chunk, n_chunks):
    H, dk, dv = MLSTM_HEADS, MLSTM_DQK, MLSTM_DV

    @pl.when(pl.program_id(1) == 0)
    def _():
        c_ref[...] = jnp.zeros_like(c_ref)
        m_ref[...] = jnp.zeros_like(m_ref)

    row = lax.broadcasted_iota(I32, (chunk, chunk), 0)
    col = lax.broadcasted_iota(I32, (chunk, chunk), 1)
    causal = col <= row
    tri_lower = causal.astype(BF16)
    tri_upper = (row <= col).astype(BF16)
    ones_col = (lax.broadcasted_iota(I32, (chunk, LANES), 1) == 0).astype(BF16)
    dot = functools.partial(jnp.dot, preferred_element_type=F32)

    def chunk_body(c, carry):
        r0 = pl.multiple_of(c * chunk, chunk)
        rows = pl.ds(r0, chunk)
        gc = gc_ref[rows, :]
        gr = gr_ref[:, rows]
        bc = sum(dot(tri_lower, p) for p in _split3(gc))
        br = sum(dot(p, tri_upper) for p in _split3(gr))
        hs = range(H)
        b_c = [bc[:, H + h:H + h + 1] for h in hs]
        ig_r = [gr[h:h + 1, :] for h in hs]
        b_r = [br[H + h:H + h + 1, :] for h in hs]
        b_last = [b[chunk - 1:chunk, :] for b in b_c]
        m_lanes = [m_ref[h:h + 1, :] for h in hs]
        m_old = [m[:, 0:1] for m in m_lanes]
        qh = [q_ref[rows, h * dk:(h + 1) * dk] for h in hs]
        kth = [kt_ref[h * dk:(h + 1) * dk, rows] for h in hs]
        vh = [v_ref[rows, h * dv:(h + 1) * dv] for h in hs]
        ct = [c_ref[h] for h in hs]

        qk = [dot(qh[h], kth[h]) for h in hs]
        qc = [dot(qh[h], ct[h].astype(BF16)) for h in hs]

        m_new = [jnp.maximum(b_last[h] + m_old[h],
                             jnp.max(b_last[h] - b_r[h] + ig_r[h], axis=1, keepdims=True)) for h in hs]
        ktw = [(kth[h].astype(F32) * jnp.exp(b_last[h] - b_r[h] + ig_r[h] - m_new[h])).astype(BF16)
               for h in hs]
        for h in hs:
            vaug = jnp.concatenate([vh[h], ones_col], axis=1)
            c_ref[h] = jnp.exp(b_last[h] + m_old[h] - m_new[h]) * ct[h] + dot(ktw[h], vaug)
            m_ref[h:h + 1, :] = jnp.broadcast_to(m_new[h], (1, LANES))

        a_mat = [jnp.where(causal, ig_r[h] - b_r[h], -jnp.inf) for h in hs]
        m_rep = [jnp.broadcast_to(m_lanes[h], (chunk, LANES)) for h in hs]
        g = [jnp.maximum(m_rep[h], jnp.max(a_mat[h], axis=1, keepdims=True)) for h in hs]
        s = [qk[h] * jnp.exp(a_mat[h] - jnp.concatenate([g[h]] * (chunk // LANES), axis=1))
             for h in hs]
        w_inter = [jnp.exp(m_rep[h] - g[h]) for h in hs]
        nd = [dot(s[h].astype(BF16), jnp.concatenate([vh[h], ones_col], axis=1))
              + jnp.concatenate([w_inter[h]] * (2 * dv // LANES), axis=1) * qc[h]
              for h in hs]
        for h in hs:
            num, den = nd[h][:, :dv], nd[h][:, dv:dv + 1]
            r = 1.0 / jnp.maximum(jnp.abs(den), jnp.exp(-(b_c[h] + g[h][:, 0:1])))
            scale = r * lax.rsqrt(r * r * jnp.mean(num * num, axis=1, keepdims=True) + RMS_EPS)
            og = o_ref[rows, h * dv:(h + 1) * dv]
            out_ref[rows, h * dv:(h + 1) * dv] = (
                jax.nn.sigmoid(og) * (num * scale * ng_ref[:, h * dv:(h + 1) * dv])
            ).astype(out_ref.dtype)
        return carry

    lax.fori_loop(0, n_chunks, chunk_body, 0)


def _mlstm_mixer(x2d, batch, w_in, b_gates, norm_gain):
    T, D = x2d.shape
    H, dk, dv = MLSTM_HEADS, MLSTM_DQK, MLSTM_DV
    seq = T // batch
    tm = TOKEN_TILE
    cq, ck, cv, co = H * dk, 2 * H * dk, 2 * H * dk + H * dv, 2 * H * dk + 2 * H * dv
    wq = w_in[:, :cq].astype(BF16)
    wkt = w_in[:, cq:ck].T.astype(BF16)
    wv = w_in[:, ck:cv].astype(BF16)
    wo = w_in[:, cv:co].astype(BF16)
    wg = jnp.pad(w_in[:, co:], ((0, 0), (0, LANES - 2 * H))).astype(BF16)
    bg = jnp.pad(b_gates, (0, LANES - 2 * H)).reshape(1, LANES)

    full = lambda a: pl.BlockSpec(a.shape, lambda i: (0,) * a.ndim)
    q, kt, v, o, gc, gr = pl.pallas_call(
        _mlstm_inproj_kernel,
        grid=(T // tm,),
        in_specs=[pl.BlockSpec((tm, D), lambda i: (i, 0)), full(wq), full(wkt), full(wv), full(wo),
                  full(wg), full(bg)],
        out_specs=[pl.BlockSpec((tm, H * dk), lambda i: (i, 0)),
                   pl.BlockSpec((H * dk, tm), lambda i: (0, i)),
                   pl.BlockSpec((tm, H * dv), lambda i: (i, 0)),
                   pl.BlockSpec((tm, H * dv), lambda i: (i, 0)),
                   pl.BlockSpec((tm, LANES), lambda i: (i, 0)),
                   pl.BlockSpec((2 * H, tm), lambda i: (0, i))],
        out_shape=[jax.ShapeDtypeStruct((T, H * dk), BF16),
                   jax.ShapeDtypeStruct((H * dk, T), BF16),
                   jax.ShapeDtypeStruct((T, H * dv), BF16),
                   jax.ShapeDtypeStruct((T, H * dv), F32),
                   jax.ShapeDtypeStruct((T, LANES), F32),
                   jax.ShapeDtypeStruct((2 * H, T), F32)],
        compiler_params=_params("parallel"),
        name="mlstm_inproj",
    )(x2d, wq, wkt, wv, wo, wg, bg)

    ts = min(MLSTM_SEQ_BLOCK, seq)
    chunk = min(MLSTM_CHUNK, ts)
    nsb = seq // ts
    ng = norm_gain.reshape(1, H * dv)
    return pl.pallas_call(
        functools.partial(_mlstm_cell_kernel, chunk=chunk, n_chunks=ts // chunk),
        grid=(batch, nsb),
        in_specs=[pl.BlockSpec((ts, H * dk), lambda b, s: (b * nsb + s, 0)),
                  pl.BlockSpec((H * dk, ts), lambda b, s: (0, b * nsb + s)),
                  pl.BlockSpec((ts, H * dv), lambda b, s: (b * nsb + s, 0)),
                  pl.BlockSpec((ts, H * dv), lambda b, s: (b * nsb + s, 0)),
                  pl.BlockSpec((ts, LANES), lambda b, s: (b * nsb + s, 0)),
                  pl.BlockSpec((2 * H, ts), lambda b, s: (0, b * nsb + s)),
                  pl.BlockSpec((1, H * dv), lambda b, s: (0, 0))],
        out_specs=pl.BlockSpec((ts, H * dv), lambda b, s: (b * nsb + s, 0)),
        out_shape=jax.ShapeDtypeStruct((T, H * dv), BF16),
        scratch_shapes=[pltpu.VMEM((H, dk, 2 * dv), F32), pltpu.VMEM((H, LANES), F32)],
        compiler_params=_params("parallel", "arbitrary"),
        name="mlstm_cell",
    )(q, kt, v, o, gc, gr, ng)


def _mla_proj_kernel(x_ref, cos_ref, sin_ref, wcq_ref, wckv_ref, wkr_ref, qn_ref, kvn_ref,
                     wq_ref, wqr_ref, wkn_ref, wvt_ref, q_ref, k_ref, vt_ref):
    H, dn = MLA_HEADS, MLA_NOPE
    dot = functools.partial(jnp.dot, preferred_element_type=F32)
    xb = x_ref[...].astype(BF16)
    cos, sin = cos_ref[...], sin_ref[...]
    c_q = _rms_norm(dot(xb, wcq_ref[...]), qn_ref[...]).astype(BF16)
    c_kv = _rms_norm(dot(xb, wckv_ref[...]), kvn_ref[...]).astype(BF16)
    kr2 = dot(xb, wkr_ref[...])
    kr = (kr2[:, :LANES] * cos + kr2[:, LANES:] * sin).astype(BF16)
    scale = (MLA_NOPE + MLA_ROPE) ** -0.5 * LOG2_E
    qa = dot(c_q, wq_ref[...])
    qr = dot(c_q, wqr_ref[...])
    kn = dot(c_kv, wkn_ref[...])
    for h in range(H):
        base = h * 2 * LANES
        q_ref[h, :, :dn] = (qa[:, base:base + dn] * scale).astype(BF16)
        rope = qa[:, base + dn:base + 2 * LANES] * cos + qr[:, h * LANES:(h + 1) * LANES] * sin
        q_ref[h, :, dn:] = (rope * scale).astype(BF16)
        k_ref[h, :, :dn] = kn[:, h * dn:(h + 1) * dn].astype(BF16)
        k_ref[h, :, dn:] = kr
    vt_ref[...] = lax.dot_general(wvt_ref[...], c_kv, _NT, preferred_element_type=F32).astype(BF16)


def _attn_kernel(q_ref, k_ref, vt_ref, o_ref, ok_ref, m_ref, mu_ref, acc_ref,
                 *, tile, heads, qsub, lagged):
    dv = MLA_V
    qi = pl.program_id(2)
    m_ref[...] = jnp.full_like(m_ref, -jnp.inf)
    mu_ref[...] = jnp.full_like(mu_ref, -jnp.inf)
    acc_ref[...] = jnp.zeros_like(acc_ref)
    ones_rows = jnp.ones((acc_ref.shape[1] - dv, tile), BF16)

    def emit(tiles):
        units = [(j, ds, ex, g, u) for j, ds, ex in tiles for g in range(heads)
                 for u in range(qsub) if ds is None or u >= ds]
        keys_of = lambda j: pl.ds(pl.multiple_of(j * tile, tile), tile)
        sts = {}

        def score(i):
            j, ds, ex, g, u = units[i]
            st = lax.dot_general(k_ref[g, keys_of(j), :], q_ref[g, u * tile:(u + 1) * tile, :], _NT,
                                 preferred_element_type=F32)
            if u == ds:
                kpos = lax.broadcasted_iota(I32, st.shape, 0)
                qpos = lax.broadcasted_iota(I32, st.shape, 1)
                st = jnp.where(kpos <= qpos, st, -jnp.inf)
            sts[i] = st

        def softmax_value(i):
            j, ds, ex, g, u = units[i]
            c = g * qsub + u
            st = sts.pop(i)
            tile_max = jnp.max(st, axis=0, keepdims=True)
            seen = m_ref[c]
            if ex == "two_pass":
                stab = jnp.maximum(seen, tile_max)
            elif ex == "key0":
                stab = st[0:1, :]
            else:
                stab = seen
            p = jnp.exp2(st - stab).astype(BF16)
            vt = jnp.concatenate([vt_ref[g * dv:(g + 1) * dv, keys_of(j)], ones_rows], axis=0)
            acc_ref[c] = jnp.exp2(mu_ref[c] - stab) * acc_ref[c] + jnp.dot(
                vt, p, preferred_element_type=F32)
            mu_ref[c] = stab
            m_ref[c] = jnp.maximum(seen, tile_max)

        ahead = ATTN_SCORE_LOOKAHEAD
        for i in range(min(ahead, len(units))):
            score(i)
        for i in range(len(units)):
            if i + ahead < len(units):
                score(i + ahead)
            softmax_value(i)

    ex = "earlier" if lagged else "two_pass"
    ex0 = "key0" if lagged else "two_pass"

    @pl.when(qi > 0)
    def _():
        emit([(0, None, ex0), (1, None, ex)])

    n_pairs = jnp.maximum(qi - 1, 0)

    def body(i, carry):
        emit([(2 + 4 * i + d, None, ex) for d in range(4)])
        return carry

    lax.fori_loop(0, n_pairs // 2, body, 0)

    @pl.when(n_pairs % 2 == 1)
    def _():
        emit([(2 * qi - 2, None, ex), (2 * qi - 1, None, ex)])

    @pl.when(qi > 0)
    def _():
        emit([(qsub * qi + u, u, ex) for u in range(qsub)])

    @pl.when(qi == 0)
    def _():
        emit([(0, 0, ex0), (1, 1, ex)])

    for g in range(heads):
        for u in range(qsub):
            c = g * qsub + u
            acc = acc_ref[c]
            o_ref[u * tile:(u + 1) * tile, g * dv:(g + 1) * dv] = (
                acc[:dv] * (1.0 / acc[dv:dv + 1])).T.astype(o_ref.dtype)
            ok_ref[0, c] = jnp.max(jnp.where(acc - acc == 0.0, 0.0, 1.0), axis=0, keepdims=True)


def _mla_mixer(x2d, batch, cos_t, sin_t, w_in, q_norm, kv_norm, w_qb, w_kvb):
    T, D = x2d.shape
    H, dn, dr, dv = MLA_HEADS, MLA_NOPE, MLA_ROPE, MLA_V
    seq = T // batch
    tm = TOKEN_TILE
    ql, kl = MLA_Q_LORA, MLA_KV_LORA
    half = dr // 2

    def rot(w):
        return jnp.concatenate([-w[..., half:], w[..., :half]], axis=-1)

    wcq = w_in[:, :ql].astype(BF16)
    wckv = w_in[:, ql:ql + kl].astype(BF16)
    wr = w_in[:, ql + kl:]
    zr = jnp.zeros((D, LANES - dr), F32)
    wkr = jnp.concatenate([wr, zr, rot(wr), zr], axis=1).astype(BF16)
    wq3 = w_qb.reshape(ql, H, dn + dr)
    zq = jnp.zeros((ql, H, LANES - dr), F32)
    wq = jnp.concatenate([wq3, zq], axis=2).reshape(ql, H * 2 * LANES).astype(BF16)
    wqr = jnp.concatenate([rot(wq3[:, :, dn:]), zq], axis=2).reshape(ql, H * LANES).astype(BF16)
    wkv3 = w_kvb.reshape(kl, H, dn + dv)
    wkn = wkv3[:, :, :dn].reshape(kl, H * dn).astype(BF16)
    wvt = wkv3[:, :, dn:].reshape(kl, H * dv).T.astype(BF16)
    qn = q_norm.reshape(1, ql)
    kvn = kv_norm.reshape(1, kl)

    full = lambda a: pl.BlockSpec(a.shape, lambda i: (0,) * a.ndim)
    q, k, vt = pl.pallas_call(
        _mla_proj_kernel,
        grid=(T // tm,),
        in_specs=[pl.BlockSpec((tm, D), lambda i: (i, 0)),
                  pl.BlockSpec((tm, LANES), lambda i: (i, 0)),
                  pl.BlockSpec((tm, LANES), lambda i: (i, 0)),
                  full(wcq), full(wckv), full(wkr), full(qn), full(kvn),
                  full(wq), full(wqr), full(wkn), full(wvt)],
        out_specs=[pl.BlockSpec((H, tm, 2 * LANES), lambda i: (0, i, 0)),
                   pl.BlockSpec((H, tm, 2 * LANES), lambda i: (0, i, 0)),
                   pl.BlockSpec((H * dv, tm), lambda i: (0, i))],
        out_shape=[jax.ShapeDtypeStruct((H, T, 2 * LANES), BF16),
                   jax.ShapeDtypeStruct((H, T, 2 * LANES), BF16),
                   jax.ShapeDtypeStruct((H * dv, T), BF16)],
        compiler_params=_params("parallel"),
        name="mla_proj",
    )(x2d, cos_t, sin_t, wcq, wckv, wkr, qn, kvn, wq, wqr, wkn, wvt)

    tile = min(ATTN_TILE, seq)
    hp = ATTN_HEADS_PER_STEP
    qsub = min(ATTN_QUERY_TILES_PER_STEP, seq // tile)
    tq = tile * qsub
    nq = seq // tq
    nhp = H // hp
    nc = hp * qsub

    def attention(lagged):
        return pl.pallas_call(
            functools.partial(_attn_kernel, tile=tile, heads=hp, qsub=qsub, lagged=lagged),
            grid=(batch, nhp, nq),
            in_specs=[pl.BlockSpec((hp, tq, 2 * LANES), lambda b, h, i: (h, b * nq + i, 0)),
                      pl.BlockSpec((hp, seq, 2 * LANES), lambda b, h, i: (h, b, 0)),
                      pl.BlockSpec((hp * dv, seq), lambda b, h, i: (h, b))],
            out_specs=[pl.BlockSpec((tq, hp * dv), lambda b, h, i: (b * nq + i, h)),
                       pl.BlockSpec((1, nc, 1, tile), lambda b, h, i: ((b * nhp + h) * nq + i, 0, 0, 0))],
            out_shape=[jax.ShapeDtypeStruct((T, H * dv), BF16),
                       jax.ShapeDtypeStruct((batch * nhp * nq, nc, 1, tile), F32)],
            scratch_shapes=[pltpu.VMEM((nc, 1, tile), F32), pltpu.VMEM((nc, 1, tile), F32),
                            pltpu.VMEM((nc, dv + BF16_SUBLANES, tile), F32)],
            compiler_params=_params("parallel", "parallel", "arbitrary"),
            name="mla_attention" if lagged else "mla_attention_exact",
        )(q, k, vt)

    out, not_finite = attention(lagged=True)
    return lax.cond(jnp.any(not_finite > 0.0), lambda: attention(lagged=False)[0], lambda: out)


def _post_kernel(h_ref, x_ref, w_ref, g_ref, b_ref, wr_ref, br_ref, tri_ref,
                 xo_ref, xp_ref, idx_ref, gate_ref, pos_ref, cnt_ref, run_ref):
    E = N_EXPERTS
    tm = x_ref.shape[0]

    @pl.when(pl.program_id(0) == 0)
    def _():
        run_ref[...] = jnp.zeros_like(run_ref)

    mix = jnp.dot(h_ref[...], w_ref[...], preferred_element_type=F32)
    x1 = _layer_norm(DEEPNORM_ALPHA * x_ref[...] + mix, g_ref[...], b_ref[...])
    xo_ref[...] = x1
    _store_packed(xp_ref, x1)

    xh = x1.astype(BF16)
    xl = (x1 - xh.astype(F32)).astype(BF16)
    wh, wl = wr_ref[0], wr_ref[1]
    ntdot = lambda a, b: lax.dot_general(a, b, _NT, preferred_element_type=F32)
    logits = ntdot(wh, xh) + ntdot(wh, xl) + ntdot(wl, xh) + br_ref[...]

    e_iota = lax.broadcasted_iota(I32, (E, tm), 0)
    rest = logits
    vals, sels = [], []
    for k in range(TOP_K):
        v = jnp.max(rest, axis=0, keepdims=True)
        ik = jnp.min(jnp.where(rest == v, e_iota, E), axis=0, keepdims=True)
        sel = e_iota == ik
        rest = jnp.where(sel, -jnp.inf, rest)
        vals.append(v)
        sels.append(sel)
        idx_ref[k:k + 1, :] = ik
    ex = [jnp.exp(v - vals[0]) for v in vals]
    inv = 1.0 / sum(ex)
    for k in range(TOP_K):
        gate_ref[k:k + 1, :] = ex[k] * inv

    chosen = functools.reduce(jnp.logical_or, sels)
    onehot = chosen.astype(BF16)
    before = jnp.dot(onehot, tri_ref[...], preferred_element_type=F32)
    run = run_ref[:, 0:1]
    rank = before + run
    for k in range(TOP_K):
        pos_ref[k:k + 1, :] = jnp.sum(jnp.where(sels[k], rank, 0.0), axis=0,
                                      keepdims=True).astype(I32)
    run_new = run + jnp.sum(chosen.astype(F32), axis=1, keepdims=True)
    run_ref[...] = jnp.broadcast_to(run_new, run_ref.shape)
    cnt_ref[...] = jnp.broadcast_to(run_new, cnt_ref.shape)


def _post_mixer(h, x2d, w_out, ln_g, ln_b, w_router, b_router):
    T, D = x2d.shape
    nch = D // (2 * SC_COLS)
    E = N_EXPERTS
    tm = TOKEN_TILE
    wo = w_out.astype(BF16)
    wrt = w_router.T
    wrh = wrt.astype(BF16)
    wr = jnp.stack([wrh, (wrt - wrh.astype(F32)).astype(BF16)])
    full = lambda a: pl.BlockSpec(a.shape, lambda i: (0,) * a.ndim)
    g, b, br = ln_g.reshape(1, D), ln_b.reshape(1, D), b_router.reshape(E, 1)
    pos_ids = jnp.arange(tm, dtype=I32)
    tri = (pos_ids[:, None] < pos_ids[None, :]).astype(BF16)
    rows = TOP_K
    return pl.pallas_call(
        _post_kernel,
        grid=(T // tm,),
        in_specs=[pl.BlockSpec((tm, h.shape[1]), lambda i: (i, 0)),
                  pl.BlockSpec((tm, D), lambda i: (i, 0)),
                  full(wo), full(g), full(b), full(wr), full(br), full(tri)],
        out_specs=[pl.BlockSpec((tm, D), lambda i: (i, 0)),
                   pl.BlockSpec((nch, tm, SC_COLS), lambda i: (0, i, 0)),
                   pl.BlockSpec((rows, tm), lambda i: (0, i)),
                   pl.BlockSpec((rows, tm), lambda i: (0, i)),
                   pl.BlockSpec((rows, tm), lambda i: (0, i)),
                   pl.BlockSpec((E, LANES), lambda i: (0, 0))],
        out_shape=[jax.ShapeDtypeStruct((T, D), F32),
                   jax.ShapeDtypeStruct((nch, T, SC_COLS), U32),
                   jax.ShapeDtypeStruct((rows, T), I32),
                   jax.ShapeDtypeStruct((rows, T), F32),
                   jax.ShapeDtypeStruct((rows, T), I32),
                   jax.ShapeDtypeStruct((E, LANES), F32)],
        scratch_shapes=[pltpu.VMEM((E, LANES), F32)],
        compiler_params=_params("arbitrary"),
        name="post_mixer_router",
    )(h, x2d, wo, g, b, wr, br, tri)


def _sc_mesh():
    return plsc.VectorSubcoreMesh(core_axis_name="core", subcore_axis_name="subcore")


def _sc_scatter_rows(xc, dest, n_rows):
    nch, T, C = xc.shape
    K = dest.shape[0]
    W = SC_WINDOW
    xs = xc.reshape(nch * T, C)
    nb = (nch * T) // W
    offs = (jnp.arange(nch, dtype=I32) * n_rows)[None, :, None]
    idx = (dest[:, None, :] + offs).reshape(K, nch * T)

    @functools.partial(pl.kernel, out_type=jax.ShapeDtypeStruct((nch * n_rows, C), xc.dtype),
                       mesh=_sc_mesh(), scratch_types=[], name="moe_dispatch_scatter")
    def scatter(x_hbm, i_hbm, o_hbm):
        def body(x_vmem, i_vmem):
            for k in range(K):
                pltpu.sync_copy(x_vmem, o_hbm.at[i_vmem.at[k]])

        pltpu.emit_pipeline(
            body,
            grid=(nb,),
            in_specs=[pl.BlockSpec((W, C), lambda g: (g, 0)),
                      pl.BlockSpec((K, W), lambda g: (0, g))],
            out_specs=[],
            core_axis_name=("core", "subcore"),
            dimension_semantics=(pltpu.PARALLEL,),
        )(x_hbm, i_hbm)

    return scatter(xs, idx).reshape(nch, n_rows, C)


def _sc_gather_rows(yc, dest):
    nch, n_rows, C = yc.shape
    K, T = dest.shape
    W = SC_WINDOW
    ys = yc.reshape(nch * n_rows, C)
    offs = (jnp.arange(nch, dtype=I32) * n_rows)[:, None, None]
    idx = (dest[None, :, :] + offs).reshape(1, nch * K * T)
    n_sub = nch * K * T

    @functools.partial(pl.kernel, out_type=jax.ShapeDtypeStruct((n_sub, C), yc.dtype),
                       mesh=_sc_mesh(), scratch_types=[], name="moe_combine_gather")
    def gather(t_hbm, i_hbm, o_hbm):
        def body(i_vmem, o_vmem):
            pltpu.sync_copy(t_hbm.at[i_vmem.at[0]], o_vmem)

        pltpu.emit_pipeline(
            body,
            grid=(n_sub // W,),
            in_specs=[pl.BlockSpec((1, W), lambda g: (0, g))],
            out_specs=[pl.BlockSpec((W, C), lambda g: (g, 0))],
            core_axis_name=("core", "subcore"),
            dimension_semantics=(pltpu.PARALLEL,),
        )(i_hbm, o_hbm)

    return gather(ys, idx).reshape(nch, K, T, C)


def _expert_kernel(te_ref, nu_ref, x_ref, wgu_ref, bg_ref, bl_ref, wd_ref, bd_ref, perm_ref,
                   y_ref, wg_s, wl_s, wd_s):
    i = pl.program_id(0)
    active = i < nu_ref[0]
    e = te_ref[i]
    changed = jnp.logical_or(i == 0, e != te_ref[jnp.maximum(i - 1, 0)])
    dot = functools.partial(jnp.dot, preferred_element_type=F32)

    @pl.when(jnp.logical_and(active, changed))
    def _():
        n_blocks = wgu_ref.shape[3] // MXU_DIM
        for blk in range(n_blocks):
            wb = wgu_ref[0, 0, :, blk * MXU_DIM:(blk + 1) * MXU_DIM].astype(BF16)
            wp = dot(wb, perm_ref[...]).astype(BF16)
            wg_s[:, blk * LANES:(blk + 1) * LANES] = wp[:, :LANES]
            wl_s[:, blk * LANES:(blk + 1) * LANES] = wp[:, LANES:]
        wd_s[...] = wd_ref[0, 0].astype(BF16)

    @pl.when(active)
    def _():
        xb = _load_packed(x_ref).astype(BF16)
        g = jnp.minimum(dot(xb, wg_s[...]) + bg_ref[0], SWIGLU_LIMIT)
        lin = jnp.clip(dot(xb, wl_s[...]) + bl_ref[0], -SWIGLU_LIMIT, SWIGLU_LIMIT)
        act = (lin + 1.0) * g * jax.nn.sigmoid(SWIGLU_ALPHA * g)
        _store_packed(y_ref, dot(act.astype(BF16), wd_s[...]) + bd_ref[0])


def _combine_kernel(y0_ref, y1_ref, y2_ref, y3_ref, gate_ref, x_ref, g_ref, b_ref, o_ref):
    gate = gate_ref[...]
    ff = (gate[:, 0:1] * _load_packed(y0_ref) + gate[:, 1:2] * _load_packed(y1_ref)
          + gate[:, 2:3] * _load_packed(y2_ref) + gate[:, 3:4] * _load_packed(y3_ref))
    o_ref[...] = _layer_norm(DEEPNORM_ALPHA * x_ref[...] + ff, g_ref[...], b_ref[...])


def _moe_dispatch(x1p, idx, pos, cnt):
    nch, T, C = x1p.shape
    E, K = N_EXPERTS, TOP_K
    te = EXPERT_TILE
    n_tiles = (T * K) // te + E
    n_rows = n_tiles * te

    counts = cnt[:, 0].astype(I32)
    padded = (counts + te - 1) // te * te
    pad_end = jnp.cumsum(padded)
    pad_start = pad_end - padded
    experts = jnp.arange(E, dtype=I32)
    start_of = jnp.sum(jnp.where(idx[None] == experts[:, None, None],
                                 pad_start[:, None, None], 0), axis=0)
    dest = start_of + pos
    tile_start = jnp.arange(n_tiles, dtype=I32) * te
    tile_expert = jnp.minimum(
        jnp.sum((pad_end[None, :] <= tile_start[:, None]).astype(I32), axis=1), E - 1)
    n_used = (pad_end[-1] // te).astype(I32).reshape(1)

    x_rows = _sc_scatter_rows(x1p, dest, n_rows)
    return x_rows, dest, tile_expert, n_used


def _moe_experts(x_rows, tile_expert, n_used, layer, w_gu, b_gu, w_down, b_down):
    nch, n_rows, C = x_rows.shape
    E = N_EXPERTS
    D, dff = w_down.shape[3], w_down.shape[2]
    te = EXPERT_TILE
    n_tiles = n_rows // te
    half = MXU_DIM // 2
    src = jnp.arange(MXU_DIM)
    perm = (src[:, None] == jnp.where(src < half, 2 * src, 2 * (src - half) + 1)[None, :])
    perm = perm.astype(BF16)
    bg = b_gu[:, 0::2].reshape(E, 1, dff)
    bl = b_gu[:, 1::2].reshape(E, 1, dff)
    bd = b_down.reshape(E, 1, D)

    def row_map(i, te_ref, nu_ref):
        return (0, jnp.minimum(i, nu_ref[0] - 1), 0)

    def exp_map(i, te_ref, nu_ref):
        return (te_ref[i], 0, 0)

    def stacked_map(i, te_ref, nu_ref):
        return (layer, te_ref[i], 0, 0)

    return pl.pallas_call(
        _expert_kernel,
        grid_spec=pltpu.PrefetchScalarGridSpec(
            num_scalar_prefetch=2,
            grid=(n_tiles,),
            in_specs=[pl.BlockSpec((nch, te, C), row_map),
                      pl.BlockSpec((1, 1, D, 2 * dff), stacked_map),
                      pl.BlockSpec((1, 1, dff), exp_map),
                      pl.BlockSpec((1, 1, dff), exp_map),
                      pl.BlockSpec((1, 1, dff, D), stacked_map),
                      pl.BlockSpec((1, 1, D), exp_map),
                      pl.BlockSpec((MXU_DIM, MXU_DIM), lambda i, a, b: (0, 0))],
            out_specs=pl.BlockSpec((nch, te, C), row_map),
            scratch_shapes=[pltpu.VMEM((D, dff), BF16), pltpu.VMEM((D, dff), BF16),
                            pltpu.VMEM((dff, D), BF16)],
        ),
        out_shape=jax.ShapeDtypeStruct((nch, n_rows, C), U32),
        compiler_params=_params("arbitrary"),
        name="moe_experts",
    )(tile_expert, n_used, x_rows, w_gu, bg, bl, w_down, bd, perm)


def _moe_combine(y_tok, gate, x1, ln_g, ln_b):
    nch, K, T, C = y_tok.shape
    D = x1.shape[1]
    tm = TOKEN_TILE
    gate_col = gate.T
    slot = lambda k: pl.BlockSpec((nch, None, tm, C), lambda i, k=k: (0, k, i, 0))
    vec = pl.BlockSpec((1, D), lambda i: (0, 0))
    return pl.pallas_call(
        _combine_kernel,
        grid=(T // tm,),
        in_specs=[slot(0), slot(1), slot(2), slot(3),
                  pl.BlockSpec((tm, K), lambda i: (i, 0)),
                  pl.BlockSpec((tm, D), lambda i: (i, 0)), vec, vec],
        out_specs=pl.BlockSpec((tm, D), lambda i: (i, 0)),
        out_shape=jax.ShapeDtypeStruct((T, D), F32),
        compiler_params=_params("parallel"),
        name="moe_combine_norm",
    )(y_tok, y_tok, y_tok, y_tok, gate_col, x1, ln_g.reshape(1, D), ln_b.reshape(1, D))


def kernel(x, positions, ln_gain, ln_bias, mlstm_w_in, mlstm_b_gates, mlstm_norm_gain,
           mlstm_w_out, mla_w_in, mla_q_norm, mla_kv_norm, mla_w_qb, mla_w_kvb, mla_w_out,
           moe_w_router, moe_b_router, moe_w_gate_up, moe_b_gate_up, moe_w_down, moe_b_down):
    B, S, D = x.shape
    T = B * S
    x2d = x.reshape(T, D)

    inv_freq = ROPE_THETA ** (-jnp.arange(0, MLA_ROPE, 2, dtype=F32) / MLA_ROPE)
    ang = positions.astype(F32).reshape(T, 1) * inv_freq
    reps = LANES // ang.shape[1]
    cos_t = jnp.tile(jnp.cos(ang), (1, reps))
    sin_t = jnp.tile(jnp.sin(ang), (1, reps))

    ns = TOKEN_STREAMS if B % TOKEN_STREAMS == 0 else 1
    bs, ts = B // ns, T // ns
    streams = range(ns)
    xs = [x2d[i * ts:(i + 1) * ts] for i in streams]
    cos_s = [cos_t[i * ts:(i + 1) * ts] for i in streams]
    sin_s = [sin_t[i * ts:(i + 1) * ts] for i in streams]

    for layer in range(DEPTH):
        j = layer // 2
        if layer % 2 == 0:
            hs = [_mlstm_mixer(xs[i], bs, mlstm_w_in[j], mlstm_b_gates[j], mlstm_norm_gain[j])
                  for i in streams]
            w_out = mlstm_w_out[j]
        else:
            hs = [_mla_mixer(xs[i], bs, cos_s[i], sin_s[i], mla_w_in[j], mla_q_norm[j],
                             mla_kv_norm[j], mla_w_qb[j], mla_w_kvb[j]) for i in streams]
            w_out = mla_w_out[j]
        routed, sent = [], []
        for i in streams:
            routed.append(_post_mixer(hs[i], xs[i], w_out, ln_gain[layer, 0], ln_bias[layer, 0],
                                      moe_w_router[layer], moe_b_router[layer]))
            x1, x1p, idx, gate, pos, cnt = routed[i]
            sent.append(_moe_dispatch(x1p, idx, pos, cnt))
        y_tok = []
        for i in streams:
            x_rows, dest, tile_expert, n_used = sent[i]
            y_rows = _moe_experts(x_rows, tile_expert, n_used, layer, moe_w_gate_up,
                                  moe_b_gate_up[layer], moe_w_down, moe_b_down[layer])
            y_tok.append(_sc_gather_rows(y_rows, dest))
        xs = [_moe_combine(y_tok[i], routed[i][3], routed[i][0], ln_gain[layer, 1], ln_bias[layer, 1])
              for i in streams]
    return jnp.concatenate(xs, axis=0).reshape(B, S, D)
```

```python
import functools

import jax
import jax.numpy as jnp
from jax import lax
from jax.experimental import pallas as pl
from jax.experimental.pallas import tpu as pltpu
from jax.experimental.pallas import tpu_sc as plsc

F32 = jnp.float32
BF16 = jnp.bfloat16
I32 = jnp.int32
U32 = jnp.uint32

DEPTH = 4
MLSTM_HEADS = 8
MLSTM_DQK = 64
MLSTM_DV = 128
MLA_HEADS = 8
MLA_Q_LORA = 384
MLA_KV_LORA = 256
MLA_NOPE = 128
MLA_ROPE = 64
MLA_V = 128
ROPE_THETA = 10000.0
N_EXPERTS = 32
TOP_K = 4
SWIGLU_LIMIT = 7.0
SWIGLU_ALPHA = 1.702
DEEPNORM_ALPHA = (2.0 * DEPTH) ** 0.25
LN_EPS = 1e-5
LOG2_E = 1.4426950408889634
RMS_EPS = 1e-6

LANES = 128
BF16_SUBLANES = 16
MXU_DIM = 256
VMEM_LIMIT_BYTES = 56 * 1024 * 1024

TOKEN_TILE = 512
MLSTM_SEQ_BLOCK = 1024
MLSTM_CHUNK = 256
ATTN_TILE = 512
ATTN_HEADS_PER_STEP = 2
ATTN_QUERY_TILES_PER_STEP = 2
ATTN_SCORE_LOOKAHEAD = 1
EXPERT_TILE = 512
TOKEN_STREAMS = 1
SC_WINDOW = 128
SC_COLS = 256

_NT = (((1,), (1,)), ((), ()))


def _params(*sem):
    return pltpu.CompilerParams(dimension_semantics=sem, vmem_limit_bytes=VMEM_LIMIT_BYTES)


def _layer_norm(z, g, b):
    mu = jnp.mean(z, axis=-1, keepdims=True)
    zc = z - mu
    var = jnp.mean(zc * zc, axis=-1, keepdims=True)
    return zc * lax.rsqrt(var + LN_EPS) * g + b


def _rms_norm(z, g):
    return z * lax.rsqrt(jnp.mean(z * z, axis=-1, keepdims=True) + RMS_EPS) * g


def _store_packed(ref, a):
    half = a.shape[1] // 2
    rounded = lambda v: lax.bitcast_convert_type(v.astype(BF16).astype(F32), U32)
    words = (rounded(a[:, :half]) >> 16) | (rounded(a[:, half:]) & jnp.uint32(0xFFFF0000))
    for c in range(ref.shape[0]):
        ref[c] = words[:, c * SC_COLS:(c + 1) * SC_COLS]


def _load_packed(ref):
    chunks = [ref[c] for c in range(ref.shape[0])]
    lo = [lax.bitcast_convert_type(w << 16, F32) for w in chunks]
    hi = [lax.bitcast_convert_type(w & jnp.uint32(0xFFFF0000), F32) for w in chunks]
    return jnp.concatenate(lo + hi, axis=1)


def _split3(a):
    hi = a.astype(BF16)
    r = a - hi.astype(F32)
    mid = r.astype(BF16)
    lo = (r - mid.astype(F32)).astype(BF16)
    return hi, mid, lo


def _mlstm_inproj_kernel(x_ref, wq_ref, wkt_ref, wv_ref, wo_ref, wg_ref, bg_ref,
                         q_ref, kt_ref, v_ref, o_ref, gc_ref, gr_ref):
    xb = x_ref[...].astype(BF16)
    dot = functools.partial(jnp.dot, preferred_element_type=F32)
    q_ref[...] = (dot(xb, wq_ref[...]) * (MLSTM_DQK ** -0.5)).astype(BF16)
    kt_ref[...] = lax.dot_general(wkt_ref[...], xb, _NT, preferred_element_type=F32).astype(BF16)
    v_ref[...] = dot(xb, wv_ref[...]).astype(BF16)
    o_ref[...] = dot(xb, wo_ref[...])
    z = dot(xb, wg_ref[...]) + bg_ref[...]
    lane = lax.broadcasted_iota(I32, z.shape, 1)
    log_sig = jnp.minimum(z, 0.0) - jnp.log1p(jnp.exp(-jnp.abs(z)))
    g = jnp.where(lane < MLSTM_HEADS, z, log_sig)
    gc_ref[...] = g
    gr_ref[...] = g.T[:2 * MLSTM_HEADS, :]


def _mlstm_cell_kernel(q_ref, kt_ref, v_ref, o_ref, gc_ref, gr_ref, ng_ref, out_ref,
                       c_ref, m_ref, *, chunk, n_chunks):
    H, dk, dv = MLSTM_HEADS, MLSTM_DQK, MLSTM_DV

    @pl.when(pl.program_id(1) == 0)
    def _():
        c_ref[...] = jnp.zeros_like(c_ref)
        m_ref[...] = jnp.zeros_like(m_ref)

    row = lax.broadcasted_iota(I32, (chunk, chunk), 0)
    col = lax.broadcasted_iota(I32, (chunk, chunk), 1)
    causal = col <= row
    tri_lower = causal.astype(BF16)
    tri_upper = (row <= col).astype(BF16)
    ones_col = (lax.broadcasted_iota(I32, (chunk, LANES), 1) == 0).astype(BF16)
    dot = functools.partial(jnp.dot, preferred_element_type=F32)

    def chunk_body(c, carry):
        r0 = pl.multiple_of(c * chunk, chunk)
        rows = pl.ds(r0, chunk)
        gc = gc_ref[rows, :]
        gr = gr_ref[:, rows]
        bc = sum(dot(tri_lower, p) for p in _split3(gc))
        br = sum(dot(p, tri_upper) for p in _split3(gr))
        hs = range(H)
        b_c = [bc[:, H + h:H + h + 1] for h in hs]
        ig_r = [gr[h:h + 1, :] for h in hs]
        b_r = [br[H + h:H + h + 1, :] for h in hs]
        b_last = [b[chunk - 1:chunk, :] for b in b_c]
        m_lanes = [m_ref[h:h + 1, :] for h in hs]
        m_old = [m[:, 0:1] for m in m_lanes]
        qh = [q_ref[rows, h * dk:(h + 1) * dk] for h in hs]
        kth = [kt_ref[h * dk:(h + 1) * dk, rows] for h in hs]
        vh = [v_ref[rows, h * dv:(h + 1) * dv] for h in hs]
        ct = [c_ref[h] for h in hs]

        qk = [dot(qh[h], kth[h]) for h in hs]
        qc = [dot(qh[h], ct[h].astype(BF16)) for h in hs]

        m_new = [jnp.maximum(b_last[h] + m_old[h],
                             jnp.max(b_last[h] - b_r[h] + ig_r[h], axis=1, keepdims=True)) for h in hs]
        ktw = [(kth[h].astype(F32) * jnp.exp(b_last[h] - b_r[h] + ig_r[h] - m_new[h])).astype(BF16)
               for h in hs]
        for h in hs:
            vaug = jnp.concatenate([vh[h], ones_col], axis=1)
            c_ref[h] = jnp.exp(b_last[h] + m_old[h] - m_new[h]) * ct[h] + dot(ktw[h], vaug)
            m_ref[h:h + 1, :] = jnp.broadcast_to(m_new[h], (1, LANES))

        a_mat = [jnp.where(causal, ig_r[h] - b_r[h], -jnp.inf) for h in hs]
        m_rep = [jnp.broadcast_to(m_lanes[h], (chunk, LANES)) for h in hs]
        g = [jnp.maximum(m_rep[h], jnp.max(a_mat[h], axis=1, keepdims=True)) for h in hs]
        s = [qk[h] * jnp.exp(a_mat[h] - jnp.concatenate([g[h]] * (chunk // LANES), axis=1))
             for h in hs]
        w_inter = [jnp.exp(m_rep[h] - g[h]) for h in hs]
        nd = [dot(s[h].astype(BF16), jnp.concatenate([vh[h], ones_col], axis=1))
              + jnp.concatenate([w_inter[h]] * (2 * dv // LANES), axis=1) * qc[h]
              for h in hs]
        for h in hs:
            num, den = nd[h][:, :dv], nd[h][:, dv:dv + 1]
            r = 1.0 / jnp.maximum(jnp.abs(den), jnp.exp(-(b_c[h] + g[h][:, 0:1])))
            scale = r * lax.rsqrt(r * r * jnp.mean(num * num, axis=1, keepdims=True) + RMS_EPS)
            og = o_ref[rows, h * dv:(h + 1) * dv]
            out_ref[rows, h * dv:(h + 1) * dv] = (
                jax.nn.sigmoid(og) * (num * scale * ng_ref[:, h * dv:(h + 1) * dv])
            ).astype(out_ref.dtype)
        return carry

    lax.fori_loop(0, n_chunks, chunk_body, 0)


def _mlstm_mixer(x2d, batch, w_in, b_gates, norm_gain):
    T, D = x2d.shape
    H, dk, dv = MLSTM_HEADS, MLSTM_DQK, MLSTM_DV
    seq = T // batch
    tm = TOKEN_TILE
    cq, ck, cv, co = H * dk, 2 * H * dk, 2 * H * dk + H * dv, 2 * H * dk + 2 * H * dv
    wq = w_in[:, :cq].astype(BF16)
    wkt = w_in[:, cq:ck].T.astype(BF16)
    wv = w_in[:, ck:cv].astype(BF16)
    wo = w_in[:, cv:co].astype(BF16)
    wg = jnp.pad(w_in[:, co:], ((0, 0), (0, LANES - 2 * H))).astype(BF16)
    bg = jnp.pad(b_gates, (0, LANES - 2 * H)).reshape(1, LANES)

    full = lambda a: pl.BlockSpec(a.shape, lambda i: (0,) * a.ndim)
    q, kt, v, o, gc, gr = pl.pallas_call(
        _mlstm_inproj_kernel,
        grid=(T // tm,),
        in_specs=[pl.BlockSpec((tm, D), lambda i: (i, 0)), full(wq), full(wkt), full(wv), full(wo),
                  full(wg), full(bg)],
        out_specs=[pl.BlockSpec((tm, H * dk), lambda i: (i, 0)),
                   pl.BlockSpec((H * dk, tm), lambda i: (0, i)),
                   pl.BlockSpec((tm, H * dv), lambda i: (i, 0)),
                   pl.BlockSpec((tm, H * dv), lambda i: (i, 0)),
                   pl.BlockSpec((tm, LANES), lambda i: (i, 0)),
                   pl.BlockSpec((2 * H, tm), lambda i: (0, i))],
        out_shape=[jax.ShapeDtypeStruct((T, H * dk), BF16),
                   jax.ShapeDtypeStruct((H * dk, T), BF16),
                   jax.ShapeDtypeStruct((T, H * dv), BF16),
                   jax.ShapeDtypeStruct((T, H * dv), F32),
                   jax.ShapeDtypeStruct((T, LANES), F32),
                   jax.ShapeDtypeStruct((2 * H, T), F32)],
        compiler_params=_params("parallel"),
        name="mlstm_inproj",
    )(x2d, wq, wkt, wv, wo, wg, bg)

    ts = min(MLSTM_SEQ_BLOCK, seq)
    chunk = min(MLSTM_CHUNK, ts)
    nsb = seq // ts
    ng = norm_gain.reshape(1, H * dv)
    return pl.pallas_call(
        functools.partial(_mlstm_cell_kernel, chunk=chunk, n_chunks=ts // chunk),
        grid=(batch, nsb),
        in_specs=[pl.BlockSpec((ts, H * dk), lambda b, s: (b * nsb + s, 0)),
                  pl.BlockSpec((H * dk, ts), lambda b, s: (0, b * nsb + s)),
                  pl.BlockSpec((ts, H * dv), lambda b, s: (b * nsb + s, 0)),
                  pl.BlockSpec((ts, H * dv), lambda b, s: (b * nsb + s, 0)),
                  pl.BlockSpec((ts, LANES), lambda b, s: (b * nsb + s, 0)),
                  pl.BlockSpec((2 * H, ts), lambda b, s: (0, b * nsb + s)),
                  pl.BlockSpec((1, H * dv), lambda b, s: (0, 0))],
        out_specs=pl.BlockSpec((ts, H * dv), lambda b, s: (b * nsb + s, 0)),
        out_shape=jax.ShapeDtypeStruct((T, H * dv), BF16),
        scratch_shapes=[pltpu.VMEM((H, dk, 2 * dv), F32), pltpu.VMEM((H, LANES), F32)],
        compiler_params=_params("parallel", "arbitrary"),
        name="mlstm_cell",
    )(q, kt, v, o, gc, gr, ng)


def _mla_proj_kernel(x_ref, cos_ref, sin_ref, wcq_ref, wckv_ref, wkr_ref, qn_ref, kvn_ref,
                     wq_ref, wqr_ref, wkn_ref, wvt_ref, q_ref, k_ref, vt_ref):
    H, dn = MLA_HEADS, MLA_NOPE
    dot = functools.partial(jnp.dot, preferred_element_type=F32)
    xb = x_ref[...].astype(BF16)
    cos, sin = cos_ref[...], sin_ref[...]
    c_q = _rms_norm(dot(xb, wcq_ref[...]), qn_ref[...]).astype(BF16)
    c_kv = _rms_norm(dot(xb, wckv_ref[...]), kvn_ref[...]).astype(BF16)
    kr2 = dot(xb, wkr_ref[...])
    kr = (kr2[:, :LANES] * cos + kr2[:, LANES:] * sin).astype(BF16)
    scale = (MLA_NOPE + MLA_ROPE) ** -0.5 * LOG2_E
    qa = dot(c_q, wq_ref[...])
    qr = dot(c_q, wqr_ref[...])
    kn = dot(c_kv, wkn_ref[...])
    for h in range(H):
        base = h * 2 * LANES
        q_ref[h, :, :dn] = (qa[:, base:base + dn] * scale).astype(BF16)
        rope = qa[:, base + dn:base + 2 * LANES] * cos + qr[:, h * LANES:(h + 1) * LANES] * sin
        q_ref[h, :, dn:] = (rope * scale).astype(BF16)
        k_ref[h, :, :dn] = kn[:, h * dn:(h + 1) * dn].astype(BF16)
        k_ref[h, :, dn:] = kr
    vt_ref[...] = lax.dot_general(wvt_ref[...], c_kv, _NT, preferred_element_type=F32).astype(BF16)


def _attn_kernel(q_ref, k_ref, vt_ref, o_ref, ok_ref, m_ref, mu_ref, acc_ref,
                 *, tile, heads, qsub, lagged):
    dv = MLA_V
    qi = pl.program_id(2)
    m_ref[...] = jnp.full_like(m_ref, -jnp.inf)
    mu_ref[...] = jnp.full_like(mu_ref, -jnp.inf)
    acc_ref[...] = jnp.zeros_like(acc_ref)
    ones_rows = jnp.ones((acc_ref.shape[1] - dv, tile), BF16)

    def emit(tiles):
        units = [(j, ds, ex, g, u) for j, ds, ex in tiles for g in range(heads)
                 for u in range(qsub) if ds is None or u >= ds]
        keys_of = lambda j: pl.ds(pl.multiple_of(j * tile, tile), tile)
        sts = {}

        def score(i):
            j, ds, ex, g, u = units[i]
            st = lax.dot_general(k_ref[g, keys_of(j), :], q_ref[g, u * tile:(u + 1) * tile, :], _NT,
                                 preferred_element_type=F32)
            if u == ds:
                kpos = lax.broadcasted_iota(I32, st.shape, 0)
                qpos = lax.broadcasted_iota(I32, st.shape, 1)
                st = jnp.where(kpos <= qpos, st, -jnp.inf)
            sts[i] = st

        def softmax_value(i):
            j, ds, ex, g, u = units[i]
            c = g * qsub + u
            st = sts.pop(i)
            tile_max = jnp.max(st, axis=0, keepdims=True)
            seen = m_ref[c]
            if ex == "two_pass":
                stab = jnp.maximum(seen, tile_max)
            elif ex == "key0":
                stab = st[0:1, :]
            else:
                stab = seen
            p = jnp.exp2(st - stab).astype(BF16)
            vt = jnp.concatenate([vt_ref[g * dv:(g + 1) * dv, keys_of(j)], ones_rows], axis=0)
            acc_ref[c] = jnp.exp2(mu_ref[c] - stab) * acc_ref[c] + jnp.dot(
                vt, p, preferred_element_type=F32)
            mu_ref[c] = stab
            m_ref[c] = jnp.maximum(seen, tile_max)

        ahead = ATTN_SCORE_LOOKAHEAD
        for i in range(min(ahead, len(units))):
            score(i)
        for i in range(len(units)):
            if i + ahead < len(units):
                score(i + ahead)
            softmax_value(i)

    ex = "earlier" if lagged else "two_pass"
    ex0 = "key0" if lagged else "two_pass"

    @pl.when(qi > 0)
    def _():
        emit([(0, None, ex0), (1, None, ex)])

    n_pairs = jnp.maximum(qi - 1, 0)

    def body(i, carry):
        emit([(2 + 4 * i + d, None, ex) for d in range(4)])
        return carry

    lax.fori_loop(0, n_pairs // 2, body, 0)

    @pl.when(n_pairs % 2 == 1)
    def _():
        emit([(2 * qi - 2, None, ex), (2 * qi - 1, None, ex)])

    @pl.when(qi > 0)
    def _():
        emit([(qsub * qi + u, u, ex) for u in range(qsub)])

    @pl.when(qi == 0)
    def _():
        emit([(0, 0, ex0), (1, 1, ex)])

    for g in range(heads):
        for u in range(qsub):
            c = g * qsub + u
            acc = acc_ref[c]
            o_ref[u * tile:(u + 1) * tile, g * dv:(g + 1) * dv] = (
                acc[:dv] * (1.0 / acc[dv:dv + 1])).T.astype(o_ref.dtype)
            ok_ref[0, c] = jnp.max(jnp.where(acc - acc == 0.0, 0.0, 1.0), axis=0, keepdims=True)


def _mla_mixer(x2d, batch, cos_t, sin_t, w_in, q_norm, kv_norm, w_qb, w_kvb):
    T, D = x2d.shape
    H, dn, dr, dv = MLA_HEADS, MLA_NOPE, MLA_ROPE, MLA_V
    seq = T // batch
    tm = TOKEN_TILE
    ql, kl = MLA_Q_LORA, MLA_KV_LORA
    half = dr // 2

    def rot(w):
        return jnp.concatenate([-w[..., half:], w[..., :half]], axis=-1)

    wcq = w_in[:, :ql].astype(BF16)
    wckv = w_in[:, ql:ql + kl].astype(BF16)
    wr = w_in[:, ql + kl:]
    zr = jnp.zeros((D, LANES - dr), F32)
    wkr = jnp.concatenate([wr, zr, rot(wr), zr], axis=1).astype(BF16)
    wq3 = w_qb.reshape(ql, H, dn + dr)
    zq = jnp.zeros((ql, H, LANES - dr), F32)
    wq = jnp.concatenate([wq3, zq], axis=2).reshape(ql, H * 2 * LANES).astype(BF16)
    wqr = jnp.concatenate([rot(wq3[:, :, dn:]), zq], axis=2).reshape(ql, H * LANES).astype(BF16)
    wkv3 = w_kvb.reshape(kl, H, dn + dv)
    wkn = wkv3[:, :, :dn].reshape(kl, H * dn).astype(BF16)
    wvt = wkv3[:, :, dn:].reshape(kl, H * dv).T.astype(BF16)
    qn = q_norm.reshape(1, ql)
    kvn = kv_norm.reshape(1, kl)

    full = lambda a: pl.BlockSpec(a.shape, lambda i: (0,) * a.ndim)
    q, k, vt = pl.pallas_call(
        _mla_proj_kernel,
        grid=(T // tm,),
        in_specs=[pl.BlockSpec((tm, D), lambda i: (i, 0)),
                  pl.BlockSpec((tm, LANES), lambda i: (i, 0)),
                  pl.BlockSpec((tm, LANES), lambda i: (i, 0)),
                  full(wcq), full(wckv), full(wkr), full(qn), full(kvn),
                  full(wq), full(wqr), full(wkn), full(wvt)],
        out_specs=[pl.BlockSpec((H, tm, 2 * LANES), lambda i: (0, i, 0)),
                   pl.BlockSpec((H, tm, 2 * LANES), lambda i: (0, i, 0)),
                   pl.BlockSpec((H * dv, tm), lambda i: (0, i))],
        out_shape=[jax.ShapeDtypeStruct((H, T, 2 * LANES), BF16),
                   jax.ShapeDtypeStruct((H, T, 2 * LANES), BF16),
                   jax.ShapeDtypeStruct((H * dv, T), BF16)],
        compiler_params=_params("parallel"),
        name="mla_proj",
    )(x2d, cos_t, sin_t, wcq, wckv, wkr, qn, kvn, wq, wqr, wkn, wvt)

    tile = min(ATTN_TILE, seq)
    hp = ATTN_HEADS_PER_STEP
    qsub = min(ATTN_QUERY_TILES_PER_STEP, seq // tile)
    tq = tile * qsub
    nq = seq // tq
    nhp = H // hp
    nc = hp * qsub

    def attention(lagged):
        return pl.pallas_call(
            functools.partial(_attn_kernel, tile=tile, heads=hp, qsub=qsub, lagged=lagged),
            grid=(batch, nhp, nq),
            in_specs=[pl.BlockSpec((hp, tq, 2 * LANES), lambda b, h, i: (h, b * nq + i, 0)),
                      pl.BlockSpec((hp, seq, 2 * LANES), lambda b, h, i: (h, b, 0)),
                      pl.BlockSpec((hp * dv, seq), lambda b, h, i: (h, b))],
            out_specs=[pl.BlockSpec((tq, hp * dv), lambda b, h, i: (b * nq + i, h)),
                       pl.BlockSpec((1, nc, 1, tile), lambda b, h, i: ((b * nhp + h) * nq + i, 0, 0, 0))],
            out_shape=[jax.ShapeDtypeStruct((T, H * dv), BF16),
                       jax.ShapeDtypeStruct((batch * nhp * nq, nc, 1, tile), F32)],
            scratch_shapes=[pltpu.VMEM((nc, 1, tile), F32), pltpu.VMEM((nc, 1, tile), F32),
                            pltpu.VMEM((nc, dv + BF16_SUBLANES, tile), F32)],
            compiler_params=_params("parallel", "parallel", "arbitrary"),
            name="mla_attention" if lagged else "mla_attention_exact",
        )(q, k, vt)

    out, not_finite = attention(lagged=True)
    return lax.cond(jnp.any(not_finite > 0.0), lambda: attention(lagged=False)[0], lambda: out)


def _post_kernel(h_ref, x_ref, w_ref, g_ref, b_ref, wr_ref, br_ref, tri_ref,
                 xo_ref, xp_ref, idx_ref, gate_ref, pos_ref, cnt_ref, run_ref):
    E = N_EXPERTS
    tm = x_ref.shape[0]

    @pl.when(pl.program_id(0) == 0)
    def _():
        run_ref[...] = jnp.zeros_like(run_ref)

    mix = jnp.dot(h_ref[...], w_ref[...], preferred_element_type=F32)
    x1 = _layer_norm(DEEPNORM_ALPHA * x_ref[...] + mix, g_ref[...], b_ref[...])
    xo_ref[...] = x1
    _store_packed(xp_ref, x1)

    xh = x1.astype(BF16)
    xl = (x1 - xh.astype(F32)).astype(BF16)
    wh, wl = wr_ref[0], wr_ref[1]
    ntdot = lambda a, b: lax.dot_general(a, b, _NT, preferred_element_type=F32)
    logits = ntdot(wh, xh) + ntdot(wh, xl) + ntdot(wl, xh) + br_ref[...]

    e_iota = lax.broadcasted_iota(I32, (E, tm), 0)
    rest = logits
    vals, sels = [], []
    for k in range(TOP_K):
        v = jnp.max(rest, axis=0, keepdims=True)
        ik = jnp.min(jnp.where(rest == v, e_iota, E), axis=0, keepdims=True)
        sel = e_iota == ik
        rest = jnp.where(sel, -jnp.inf, rest)
        vals.append(v)
        sels.append(sel)
        idx_ref[k:k + 1, :] = ik
    ex = [jnp.exp(v - vals[0]) for v in vals]
    inv = 1.0 / sum(ex)
    for k in range(TOP_K):
        gate_ref[k:k + 1, :] = ex[k] * inv

    chosen = functools.reduce(jnp.logical_or, sels)
    onehot = chosen.astype(BF16)
    before = jnp.dot(onehot, tri_ref[...], preferred_element_type=F32)
    run = run_ref[:, 0:1]
    rank = before + run
    for k in range(TOP_K):
        pos_ref[k:k + 1, :] = jnp.sum(jnp.where(sels[k], rank, 0.0), axis=0,
                                      keepdims=True).astype(I32)
    run_new = run + jnp.sum(chosen.astype(F32), axis=1, keepdims=True)
    run_ref[...] = jnp.broadcast_to(run_new, run_ref.shape)
    cnt_ref[...] = jnp.broadcast_to(run_new, cnt_ref.shape)


def _post_mixer(h, x2d, w_out, ln_g, ln_b, w_router, b_router):
    T, D = x2d.shape
    nch = D // (2 * SC_COLS)
    E = N_EXPERTS
    tm = TOKEN_TILE
    wo = w_out.astype(BF16)
    wrt = w_router.T
    wrh = wrt.astype(BF16)
    wr = jnp.stack([wrh, (wrt - wrh.astype(F32)).astype(BF16)])
    full = lambda a: pl.BlockSpec(a.shape, lambda i: (0,) * a.ndim)
    g, b, br = ln_g.reshape(1, D), ln_b.reshape(1, D), b_router.reshape(E, 1)
    pos_ids = jnp.arange(tm, dtype=I32)
    tri = (pos_ids[:, None] < pos_ids[None, :]).astype(BF16)
    rows = TOP_K
    return pl.pallas_call(
        _post_kernel,
        grid=(T // tm,),
        in_specs=[pl.BlockSpec((tm, h.shape[1]), lambda i: (i, 0)),
                  pl.BlockSpec((tm, D), lambda i: (i, 0)),
                  full(wo), full(g), full(b), full(wr), full(br), full(tri)],
        out_specs=[pl.BlockSpec((tm, D), lambda i: (i, 0)),
                   pl.BlockSpec((nch, tm, SC_COLS), lambda i: (0, i, 0)),
                   pl.BlockSpec((rows, tm), lambda i: (0, i)),
                   pl.BlockSpec((rows, tm), lambda i: (0, i)),
                   pl.BlockSpec((rows, tm), lambda i: (0, i)),
                   pl.BlockSpec((E, LANES), lambda i: (0, 0))],
        out_shape=[jax.ShapeDtypeStruct((T, D), F32),
                   jax.ShapeDtypeStruct((nch, T, SC_COLS), U32),
                   jax.ShapeDtypeStruct((rows, T), I32),
                   jax.ShapeDtypeStruct((rows, T), F32),
                   jax.ShapeDtypeStruct((rows, T), I32),
                   jax.ShapeDtypeStruct((E, LANES), F32)],
        scratch_shapes=[pltpu.VMEM((E, LANES), F32)],
        compiler_params=_params("arbitrary"),
        name="post_mixer_router",
    )(h, x2d, wo, g, b, wr, br, tri)


def _sc_mesh():
    return plsc.VectorSubcoreMesh(core_axis_name="core", subcore_axis_name="subcore")


def _sc_scatter_rows(xc, dest, n_rows):
    nch, T, C = xc.shape
    K = dest.shape[0]
    W = SC_WINDOW
    xs = xc.reshape(nch * T, C)
    nb = (nch * T) // W
    offs = (jnp.arange(nch, dtype=I32) * n_rows)[None, :, None]
    idx = (dest[:, None, :] + offs).reshape(K, nch * T)

    @functools.partial(pl.kernel, out_type=jax.ShapeDtypeStruct((nch * n_rows, C), xc.dtype),
                       mesh=_sc_mesh(), scratch_types=[], name="moe_dispatch_scatter")
    def scatter(x_hbm, i_hbm, o_hbm):
        def body(x_vmem, i_vmem):
            for k in range(K):
                pltpu.sync_copy(x_vmem, o_hbm.at[i_vmem.at[k]])

        pltpu.emit_pipeline(
            body,
            grid=(nb,),
            in_specs=[pl.BlockSpec((W, C), lambda g: (g, 0)),
                      pl.BlockSpec((K, W), lambda g: (0, g))],
            out_specs=[],
            core_axis_name=("core", "subcore"),
            dimension_semantics=(pltpu.PARALLEL,),
        )(x_hbm, i_hbm)

    return scatter(xs, idx).reshape(nch, n_rows, C)


def _sc_gather_rows(yc, dest):
    nch, n_rows, C = yc.shape
    K, T = dest.shape
    W = SC_WINDOW
    ys = yc.reshape(nch * n_rows, C)
    offs = (jnp.arange(nch, dtype=I32) * n_rows)[:, None, None]
    idx = (dest[None, :, :] + offs).reshape(1, nch * K * T)
    n_sub = nch * K * T

    @functools.partial(pl.kernel, out_type=jax.ShapeDtypeStruct((n_sub, C), yc.dtype),
                       mesh=_sc_mesh(), scratch_types=[], name="moe_combine_gather")
    def gather(t_hbm, i_hbm, o_hbm):
        def body(i_vmem, o_vmem):
            pltpu.sync_copy(t_hbm.at[i_vmem.at[0]], o_vmem)

        pltpu.emit_pipeline(
            body,
            grid=(n_sub // W,),
            in_specs=[pl.BlockSpec((1, W), lambda g: (0, g))],
            out_specs=[pl.BlockSpec((W, C), lambda g: (g, 0))],
            core_axis_name=("core", "subcore"),
            dimension_semantics=(pltpu.PARALLEL,),
        )(i_hbm, o_hbm)

    return gather(ys, idx).reshape(nch, K, T, C)


def _expert_kernel(te_ref, nu_ref, x_ref, wgu_ref, bgl_ref, wd_ref, bd_ref, perm_ref,
                   y_ref, wgl_s, wd_s):
    i = pl.program_id(0)
    active = i < nu_ref[0]
    e = te_ref[i]
    changed = jnp.logical_or(i == 0, e != te_ref[jnp.maximum(i - 1, 0)])
    dot = functools.partial(jnp.dot, preferred_element_type=F32)
    n_blocks = wgu_ref.shape[3] // MXU_DIM

    @pl.when(jnp.logical_and(active, changed))
    def _():
        for blk in range(n_blocks):
            cols = slice(blk * MXU_DIM, (blk + 1) * MXU_DIM)
            wb = wgu_ref[0, 0, :, cols].astype(BF16)
            wgl_s[:, cols] = dot(wb, perm_ref[...]).astype(BF16)
        wd_s[...] = wd_ref[0, 0].astype(BF16)

    @pl.when(active)
    def _():
        xb = _load_packed(x_ref).astype(BF16)
        h = dot(xb, wgl_s[...]) + bgl_ref[0]
        acts = []
        for blk in range(n_blocks):
            g = jnp.minimum(h[:, blk * MXU_DIM:blk * MXU_DIM + LANES], SWIGLU_LIMIT)
            lin = jnp.clip(h[:, blk * MXU_DIM + LANES:(blk + 1) * MXU_DIM], -SWIGLU_LIMIT, SWIGLU_LIMIT)
            acts.append(((lin + 1.0) * g * jax.nn.sigmoid(SWIGLU_ALPHA * g)).astype(BF16))
        act = jnp.concatenate(acts, axis=1)
        _store_packed(y_ref, dot(act, wd_s[...]) + bd_ref[0])


def _combine_kernel(y0_ref, y1_ref, y2_ref, y3_ref, gate_ref, x_ref, g_ref, b_ref, o_ref):
    gate = gate_ref[...]
    ff = (gate[:, 0:1] * _load_packed(y0_ref) + gate[:, 1:2] * _load_packed(y1_ref)
          + gate[:, 2:3] * _load_packed(y2_ref) + gate[:, 3:4] * _load_packed(y3_ref))
    o_ref[...] = _layer_norm(DEEPNORM_ALPHA * x_ref[...] + ff, g_ref[...], b_ref[...])


def _moe_dispatch(x1p, idx, pos, cnt):
    nch, T, C = x1p.shape
    E, K = N_EXPERTS, TOP_K
    te = EXPERT_TILE
    n_tiles = (T * K) // te + E
    n_rows = n_tiles * te

    counts = cnt[:, 0].astype(I32)
    padded = (counts + te - 1) // te * te
    pad_end = jnp.cumsum(padded)
    pad_start = pad_end - padded
    experts = jnp.arange(E, dtype=I32)
    start_of = jnp.sum(jnp.where(idx[None] == experts[:, None, None],
                                 pad_start[:, None, None], 0), axis=0)
    dest = start_of + pos
    tile_start = jnp.arange(n_tiles, dtype=I32) * te
    tile_expert = jnp.minimum(
        jnp.sum((pad_end[None, :] <= tile_start[:, None]).astype(I32), axis=1), E - 1)
    n_used = (pad_end[-1] // te).astype(I32).reshape(1)

    x_rows = _sc_scatter_rows(x1p, dest, n_rows)
    return x_rows, dest, tile_expert, n_used


def _moe_experts(x_rows, tile_expert, n_used, layer, w_gu, b_gu, w_down, b_down):
    nch, n_rows, C = x_rows.shape
    E = N_EXPERTS
    D, dff = w_down.shape[3], w_down.shape[2]
    te = EXPERT_TILE
    n_tiles = n_rows // te
    half = MXU_DIM // 2
    src = jnp.arange(MXU_DIM)
    perm = (src[:, None] == jnp.where(src < half, 2 * src, 2 * (src - half) + 1)[None, :])
    perm = perm.astype(BF16)
    bgl = b_gu.reshape(E, dff // half, half, 2).transpose(0, 1, 3, 2).reshape(E, 1, 2 * dff)
    bd = b_down.reshape(E, 1, D)

    def row_map(i, te_ref, nu_ref):
        return (0, jnp.minimum(i, nu_ref[0] - 1), 0)

    def exp_map(i, te_ref, nu_ref):
        return (te_ref[i], 0, 0)

    def stacked_map(i, te_ref, nu_ref):
        return (layer, te_ref[i], 0, 0)

    return pl.pallas_call(
        _expert_kernel,
        grid_spec=pltpu.PrefetchScalarGridSpec(
            num_scalar_prefetch=2,
            grid=(n_tiles,),
            in_specs=[pl.BlockSpec((nch, te, C), row_map),
                      pl.BlockSpec((1, 1, D, 2 * dff), stacked_map),
                      pl.BlockSpec((1, 1, 2 * dff), exp_map),
                      pl.BlockSpec((1, 1, dff, D), stacked_map),
                      pl.BlockSpec((1, 1, D), exp_map),
                      pl.BlockSpec((MXU_DIM, MXU_DIM), lambda i, a, b: (0, 0))],
            out_specs=pl.BlockSpec((nch, te, C), row_map),
            scratch_shapes=[pltpu.VMEM((D, 2 * dff), BF16), pltpu.VMEM((dff, D), BF16)],
        ),
        out_shape=jax.ShapeDtypeStruct((nch, n_rows, C), U32),
        compiler_params=_params("arbitrary"),
        name="moe_experts",
    )(tile_expert, n_used, x_rows, w_gu, bgl, w_down, bd, perm)


def _moe_combine(y_tok, gate, x1, ln_g, ln_b):
    nch, K, T, C = y_tok.shape
    D = x1.shape[1]
    tm = TOKEN_TILE
    gate_col = gate.T
    slot = lambda k: pl.BlockSpec((nch, None, tm, C), lambda i, k=k: (0, k, i, 0))
    vec = pl.BlockSpec((1, D), lambda i: (0, 0))
    return pl.pallas_call(
        _combine_kernel,
        grid=(T // tm,),
        in_specs=[slot(0), slot(1), slot(2), slot(3),
                  pl.BlockSpec((tm, K), lambda i: (i, 0)),
                  pl.BlockSpec((tm, D), lambda i: (i, 0)), vec, vec],
        out_specs=pl.BlockSpec((tm, D), lambda i: (i, 0)),
        out_shape=jax.ShapeDtypeStruct((T, D), F32),
        compiler_params=_params("parallel"),
        name="moe_combine_norm",
    )(y_tok, y_tok, y_tok, y_tok, gate_col, x1, ln_g.reshape(1, D), ln_b.reshape(1, D))


def kernel(x, positions, ln_gain, ln_bias, mlstm_w_in, mlstm_b_gates, mlstm_norm_gain,
           mlstm_w_out, mla_w_in, mla_q_norm, mla_kv_norm, mla_w_qb, mla_w_kvb, mla_w_out,
           moe_w_router, moe_b_router, moe_w_gate_up, moe_b_gate_up, moe_w_down, moe_b_down):
    B, S, D = x.shape
    T = B * S
    x2d = x.reshape(T, D)

    inv_freq = ROPE_THETA ** (-jnp.arange(0, MLA_ROPE, 2, dtype=F32) / MLA_ROPE)
    ang = positions.astype(F32).reshape(T, 1) * inv_freq
    reps = LANES // ang.shape[1]
    cos_t = jnp.tile(jnp.cos(ang), (1, reps))
    sin_t = jnp.tile(jnp.sin(ang), (1, reps))

    ns = TOKEN_STREAMS if B % TOKEN_STREAMS == 0 else 1
    bs, ts = B // ns, T // ns
    streams = range(ns)
    xs = [x2d[i * ts:(i + 1) * ts] for i in streams]
    cos_s = [cos_t[i * ts:(i + 1) * ts] for i in streams]
    sin_s = [sin_t[i * ts:(i + 1) * ts] for i in streams]

    for layer in range(DEPTH):
        j = layer // 2
        if layer % 2 == 0:
            hs = [_mlstm_mixer(xs[i], bs, mlstm_w_in[j], mlstm_b_gates[j], mlstm_norm_gain[j])
                  for i in streams]
            w_out = mlstm_w_out[j]
        else:
            hs = [_mla_mixer(xs[i], bs, cos_s[i], sin_s[i], mla_w_in[j], mla_q_norm[j],
                             mla_kv_norm[j], mla_w_qb[j], mla_w_kvb[j]) for i in streams]
            w_out = mla_w_out[j]
        routed, sent = [], []
        for i in streams:
            routed.append(_post_mixer(hs[i], xs[i], w_out, ln_gain[layer, 0], ln_bias[layer, 0],
                                      moe_w_router[layer], moe_b_router[layer]))
            x1, x1p, idx, gate, pos, cnt = routed[i]
            sent.append(_moe_dispatch(x1p, idx, pos, cnt))
        y_tok = []
        for i in streams:
            x_rows, dest, tile_expert, n_used = sent[i]
            y_rows = _moe_experts(x_rows, tile_expert, n_used, layer, moe_w_gate_up,
                                  moe_b_gate_up[layer], moe_w_down, moe_b_down[layer])
            y_tok.append(_sc_gather_rows(y_rows, dest))
        xs = [_moe_combine(y_tok[i], routed[i][3], routed[i][0], ln_gain[layer, 1], ln_bias[layer, 1])
              for i in streams]
    return jnp.concatenate(xs, axis=0).reshape(B, S, D)
```

```python
import functools

import jax
import jax.numpy as jnp
from jax import lax
from jax.experimental import pallas as pl
from jax.experimental.pallas import tpu as pltpu
from jax.experimental.pallas import tpu_sc as plsc

F32 = jnp.float32
BF16 = jnp.bfloat16
I32 = jnp.int32
U32 = jnp.uint32

DEPTH = 4
MLSTM_HEADS = 8
MLSTM_DQK = 64
MLSTM_DV = 128
MLA_HEADS = 8
MLA_Q_LORA = 384
MLA_KV_LORA = 256
MLA_NOPE = 128
MLA_ROPE = 64
MLA_V = 128
ROPE_THETA = 10000.0
N_EXPERTS = 32
TOP_K = 4
SWIGLU_LIMIT = 7.0
SWIGLU_ALPHA = 1.702
DEEPNORM_ALPHA = (2.0 * DEPTH) ** 0.25
LN_EPS = 1e-5
LOG2_E = 1.4426950408889634
RMS_EPS = 1e-6

LANES = 128
BF16_SUBLANES = 16
MXU_DIM = 256
VMEM_LIMIT_BYTES = 56 * 1024 * 1024

TOKEN_TILE = 1024
POST_ROW_BLOCK = 1024
MLSTM_SEQ_BLOCK = 1024
MLSTM_CHUNK = 256
ATTN_TILE = 512
ATTN_HEADS_PER_STEP = 2
ATTN_QUERY_TILES_PER_STEP = 2
ATTN_SCORE_LOOKAHEAD = 1
EXPERT_TILE = 512
TOKEN_STREAMS = 1
SC_WINDOW = 128
SC_COLS = 256

_NT = (((1,), (1,)), ((), ()))


def _params(*sem):
    return pltpu.CompilerParams(dimension_semantics=sem, vmem_limit_bytes=VMEM_LIMIT_BYTES)


def _layer_norm(z, g, b):
    mu = jnp.mean(z, axis=-1, keepdims=True)
    zc = z - mu
    var = jnp.mean(zc * zc, axis=-1, keepdims=True)
    return zc * lax.rsqrt(var + LN_EPS) * g + b


def _rms_norm(z, g):
    return z * lax.rsqrt(jnp.mean(z * z, axis=-1, keepdims=True) + RMS_EPS) * g


def _store_packed(ref, a, rows=slice(None)):
    half = a.shape[1] // 2
    rounded = lambda v: lax.bitcast_convert_type(v.astype(BF16).astype(F32), U32)
    words = (rounded(a[:, :half]) >> 16) | (rounded(a[:, half:]) & jnp.uint32(0xFFFF0000))
    for c in range(ref.shape[0]):
        ref[c, rows, :] = words[:, c * SC_COLS:(c + 1) * SC_COLS]


def _load_packed(ref):
    chunks = [ref[c] for c in range(ref.shape[0])]
    lo = [lax.bitcast_convert_type(w << 16, F32) for w in chunks]
    hi = [lax.bitcast_convert_type(w & jnp.uint32(0xFFFF0000), F32) for w in chunks]
    return jnp.concatenate(lo + hi, axis=1)


def _split3(a):
    hi = a.astype(BF16)
    r = a - hi.astype(F32)
    mid = r.astype(BF16)
    lo = (r - mid.astype(F32)).astype(BF16)
    return hi, mid, lo


def _mlstm_inproj_kernel(x_ref, wq_ref, wkt_ref, wv_ref, wo_ref, wg_ref, bg_ref,
                         q_ref, kt_ref, v_ref, o_ref, gc_ref, gr_ref):
    xb = x_ref[...].astype(BF16)
    dot = functools.partial(jnp.dot, preferred_element_type=F32)
    q_ref[...] = (dot(xb, wq_ref[...]) * (MLSTM_DQK ** -0.5)).astype(BF16)
    kt_ref[...] = lax.dot_general(wkt_ref[...], xb, _NT, preferred_element_type=F32).astype(BF16)
    v_ref[...] = dot(xb, wv_ref[...]).astype(BF16)
    o_ref[...] = dot(xb, wo_ref[...])
    z = dot(xb, wg_ref[...]) + bg_ref[...]
    lane = lax.broadcasted_iota(I32, z.shape, 1)
    log_sig = jnp.minimum(z, 0.0) - jnp.log1p(jnp.exp(-jnp.abs(z)))
    g = jnp.where(lane < MLSTM_HEADS, z, log_sig)
    gc_ref[...] = g
    gr_ref[...] = g.T[:2 * MLSTM_HEADS, :]


def _mlstm_cell_kernel(q_ref, kt_ref, v_ref, o_ref, gc_ref, gr_ref, ng_ref, out_ref,
                       c_ref, m_ref, *, chunk, n_chunks):
    H, dk, dv = MLSTM_HEADS, MLSTM_DQK, MLSTM_DV

    @pl.when(pl.program_id(1) == 0)
    def _():
        c_ref[...] = jnp.zeros_like(c_ref)
        m_ref[...] = jnp.zeros_like(m_ref)

    row = lax.broadcasted_iota(I32, (chunk, chunk), 0)
    col = lax.broadcasted_iota(I32, (chunk, chunk), 1)
    causal = col <= row
    tri_lower = causal.astype(BF16)
    tri_upper = (row <= col).astype(BF16)
    ones_col = (lax.broadcasted_iota(I32, (chunk, LANES), 1) == 0).astype(BF16)
    dot = functools.partial(jnp.dot, preferred_element_type=F32)

    def chunk_body(c, carry):
        r0 = pl.multiple_of(c * chunk, chunk)
        rows = pl.ds(r0, chunk)
        gc = gc_ref[rows, :]
        gr = gr_ref[:, rows]
        bc = sum(dot(tri_lower, p) for p in _split3(gc))
        br = sum(dot(p, tri_upper) for p in _split3(gr))
        hs = range(H)
        b_c = [bc[:, H + h:H + h + 1] for h in hs]
        ig_r = [gr[h:h + 1, :] for h in hs]
        b_r = [br[H + h:H + h + 1, :] for h in hs]
        b_last = [b[chunk - 1:chunk, :] for b in b_c]
        m_lanes = [m_ref[h:h + 1, :] for h in hs]
        m_old = [m[:, 0:1] for m in m_lanes]
        qh = [q_ref[rows, h * dk:(h + 1) * dk] for h in hs]
        kth = [kt_ref[h * dk:(h + 1) * dk, rows] for h in hs]
        vh = [v_ref[rows, h * dv:(h + 1) * dv] for h in hs]
        ct = [c_ref[h] for h in hs]

        qk = [dot(qh[h], kth[h]) for h in hs]
        qc = [dot(qh[h], ct[h].astype(BF16)) for h in hs]

        m_new = [jnp.maximum(b_last[h] + m_old[h],
                             jnp.max(b_last[h] - b_r[h] + ig_r[h], axis=1, keepdims=True)) for h in hs]
        ktw = [(kth[h].astype(F32) * jnp.exp(b_last[h] - b_r[h] + ig_r[h] - m_new[h])).astype(BF16)
               for h in hs]
        for h in hs:
            vaug = jnp.concatenate([vh[h], ones_col], axis=1)
            c_ref[h] = jnp.exp(b_last[h] + m_old[h] - m_new[h]) * ct[h] + dot(ktw[h], vaug)
            m_ref[h:h + 1, :] = jnp.broadcast_to(m_new[h], (1, LANES))

        a_mat = [jnp.where(causal, ig_r[h] - b_r[h], -jnp.inf) for h in hs]
        m_rep = [jnp.broadcast_to(m_lanes[h], (chunk, LANES)) for h in hs]
        g = [jnp.maximum(m_rep[h], jnp.max(a_mat[h], axis=1, keepdims=True)) for h in hs]
        s = [qk[h] * jnp.exp(a_mat[h] - jnp.concatenate([g[h]] * (chunk // LANES), axis=1))
             for h in hs]
        w_inter = [jnp.exp(m_rep[h] - g[h]) for h in hs]
        nd = [dot(s[h].astype(BF16), jnp.concatenate([vh[h], ones_col], axis=1))
              + jnp.concatenate([w_inter[h]] * (2 * dv // LANES), axis=1) * qc[h]
              for h in hs]
        for h in hs:
            num, den = nd[h][:, :dv], nd[h][:, dv:dv + 1]
            r = 1.0 / jnp.maximum(jnp.abs(den), jnp.exp(-(b_c[h] + g[h][:, 0:1])))
            scale = r * lax.rsqrt(r * r * jnp.mean(num * num, axis=1, keepdims=True) + RMS_EPS)
            og = o_ref[rows, h * dv:(h + 1) * dv]
            out_ref[rows, h * dv:(h + 1) * dv] = (
                jax.nn.sigmoid(og) * (num * scale * ng_ref[:, h * dv:(h + 1) * dv])
            ).astype(out_ref.dtype)
        return carry

    lax.fori_loop(0, n_chunks, chunk_body, 0, unroll=2)


def _mlstm_mixer(x2d, batch, w_in, b_gates, norm_gain):
    T, D = x2d.shape
    H, dk, dv = MLSTM_HEADS, MLSTM_DQK, MLSTM_DV
    seq = T // batch
    tm = TOKEN_TILE
    cq, ck, cv, co = H * dk, 2 * H * dk, 2 * H * dk + H * dv, 2 * H * dk + 2 * H * dv
    wq = w_in[:, :cq].astype(BF16)
    wkt = w_in[:, cq:ck].T.astype(BF16)
    wv = w_in[:, ck:cv].astype(BF16)
    wo = w_in[:, cv:co].astype(BF16)
    wg = jnp.pad(w_in[:, co:], ((0, 0), (0, LANES - 2 * H))).astype(BF16)
    bg = jnp.pad(b_gates, (0, LANES - 2 * H)).reshape(1, LANES)

    full = lambda a: pl.BlockSpec(a.shape, lambda i: (0,) * a.ndim)
    q, kt, v, o, gc, gr = pl.pallas_call(
        _mlstm_inproj_kernel,
        grid=(T // tm,),
        in_specs=[pl.BlockSpec((tm, D), lambda i: (i, 0)), full(wq), full(wkt), full(wv), full(wo),
                  full(wg), full(bg)],
        out_specs=[pl.BlockSpec((tm, H * dk), lambda i: (i, 0)),
                   pl.BlockSpec((H * dk, tm), lambda i: (0, i)),
                   pl.BlockSpec((tm, H * dv), lambda i: (i, 0)),
                   pl.BlockSpec((tm, H * dv), lambda i: (i, 0)),
                   pl.BlockSpec((tm, LANES), lambda i: (i, 0)),
                   pl.BlockSpec((2 * H, tm), lambda i: (0, i))],
        out_shape=[jax.ShapeDtypeStruct((T, H * dk), BF16),
                   jax.ShapeDtypeStruct((H * dk, T), BF16),
                   jax.ShapeDtypeStruct((T, H * dv), BF16),
                   jax.ShapeDtypeStruct((T, H * dv), F32),
                   jax.ShapeDtypeStruct((T, LANES), F32),
                   jax.ShapeDtypeStruct((2 * H, T), F32)],
        compiler_params=_params("parallel"),
        name="mlstm_inproj",
    )(x2d, wq, wkt, wv, wo, wg, bg)

    ts = min(MLSTM_SEQ_BLOCK, seq)
    chunk = min(MLSTM_CHUNK, ts)
    nsb = seq // ts
    ng = norm_gain.reshape(1, H * dv)
    return pl.pallas_call(
        functools.partial(_mlstm_cell_kernel, chunk=chunk, n_chunks=ts // chunk),
        grid=(batch, nsb),
        in_specs=[pl.BlockSpec((ts, H * dk), lambda b, s: (b * nsb + s, 0)),
                  pl.BlockSpec((H * dk, ts), lambda b, s: (0, b * nsb + s)),
                  pl.BlockSpec((ts, H * dv), lambda b, s: (b * nsb + s, 0)),
                  pl.BlockSpec((ts, H * dv), lambda b, s: (b * nsb + s, 0)),
                  pl.BlockSpec((ts, LANES), lambda b, s: (b * nsb + s, 0)),
                  pl.BlockSpec((2 * H, ts), lambda b, s: (0, b * nsb + s)),
                  pl.BlockSpec((1, H * dv), lambda b, s: (0, 0))],
        out_specs=pl.BlockSpec((ts, H * dv), lambda b, s: (b * nsb + s, 0)),
        out_shape=jax.ShapeDtypeStruct((T, H * dv), BF16),
        scratch_shapes=[pltpu.VMEM((H, dk, 2 * dv), F32), pltpu.VMEM((H, LANES), F32)],
        compiler_params=_params("parallel", "arbitrary"),
        name="mlstm_cell",
    )(q, kt, v, o, gc, gr, ng)


def _mla_proj_kernel(x_ref, cos_ref, sin_ref, wcq_ref, wckv_ref, wkr_ref, qn_ref, kvn_ref,
                     wq_ref, wqr_ref, wkn_ref, wvt_ref, q_ref, k_ref, vt_ref):
    H, dn = MLA_HEADS, MLA_NOPE
    dot = functools.partial(jnp.dot, preferred_element_type=F32)
    xb = x_ref[...].astype(BF16)
    cos, sin = cos_ref[...], sin_ref[...]
    c_q = _rms_norm(dot(xb, wcq_ref[...]), qn_ref[...]).astype(BF16)
    c_kv = _rms_norm(dot(xb, wckv_ref[...]), kvn_ref[...]).astype(BF16)
    kr2 = dot(xb, wkr_ref[...])
    kr = (kr2[:, :LANES] * cos + kr2[:, LANES:] * sin).astype(BF16)
    scale = (MLA_NOPE + MLA_ROPE) ** -0.5 * LOG2_E
    qa = dot(c_q, wq_ref[...])
    qr = dot(c_q, wqr_ref[...])
    kn = dot(c_kv, wkn_ref[...])
    for h in range(H):
        base = h * 2 * LANES
        q_ref[h, :, :dn] = (qa[:, base:base + dn] * scale).astype(BF16)
        rope = qa[:, base + dn:base + 2 * LANES] * cos + qr[:, h * LANES:(h + 1) * LANES] * sin
        q_ref[h, :, dn:] = (rope * scale).astype(BF16)
        k_ref[h, :, :dn] = kn[:, h * dn:(h + 1) * dn].astype(BF16)
        k_ref[h, :, dn:] = kr
    vt_ref[...] = lax.dot_general(wvt_ref[...], c_kv, _NT, preferred_element_type=F32).astype(BF16)


def _attn_kernel(q_ref, k_ref, vt_ref, o_ref, ok_ref, m_ref, mu_ref, acc_ref,
                 *, tile, heads, qsub, lagged):
    dv = MLA_V
    qi = pl.program_id(2)
    m_ref[...] = jnp.full_like(m_ref, -jnp.inf)
    mu_ref[...] = jnp.full_like(mu_ref, -jnp.inf)
    acc_ref[...] = jnp.zeros_like(acc_ref)
    ones_rows = jnp.ones((acc_ref.shape[1] - dv, tile), BF16)

    def emit(tiles):
        units = [(j, ds, ex, g, u) for j, ds, ex in tiles for g in range(heads)
                 for u in range(qsub) if ds is None or u >= ds]
        keys_of = lambda j: pl.ds(pl.multiple_of(j * tile, tile), tile)
        sts = {}

        def score(i):
            j, ds, ex, g, u = units[i]
            st = lax.dot_general(k_ref[g, keys_of(j), :], q_ref[g, u * tile:(u + 1) * tile, :], _NT,
                                 preferred_element_type=F32)
            if u == ds:
                kpos = lax.broadcasted_iota(I32, st.shape, 0)
                qpos = lax.broadcasted_iota(I32, st.shape, 1)
                st = jnp.where(kpos <= qpos, st, -jnp.inf)
            sts[i] = st

        def softmax_value(i):
            j, ds, ex, g, u = units[i]
            c = g * qsub + u
            st = sts.pop(i)
            tile_max = jnp.max(st, axis=0, keepdims=True)
            seen = m_ref[c]
            if ex == "two_pass":
                stab = jnp.maximum(seen, tile_max)
            elif ex == "key0":
                stab = st[0:1, :]
            else:
                stab = seen
            p = jnp.exp2(st - stab).astype(BF16)
            vt = jnp.concatenate([vt_ref[g * dv:(g + 1) * dv, keys_of(j)], ones_rows], axis=0)
            acc_ref[c] = jnp.exp2(mu_ref[c] - stab) * acc_ref[c] + jnp.dot(
                vt, p, preferred_element_type=F32)
            mu_ref[c] = stab
            m_ref[c] = jnp.maximum(seen, tile_max)

        ahead = ATTN_SCORE_LOOKAHEAD
        for i in range(min(ahead, len(units))):
            score(i)
        for i in range(len(units)):
            if i + ahead < len(units):
                score(i + ahead)
            softmax_value(i)

    ex = "earlier" if lagged else "two_pass"
    ex0 = "key0" if lagged else "two_pass"

    @pl.when(qi > 0)
    def _():
        emit([(0, None, ex0), (1, None, ex)])

    n_pairs = jnp.maximum(qi - 1, 0)

    def body(i, carry):
        emit([(2 + 4 * i + d, None, ex) for d in range(4)])
        return carry

    lax.fori_loop(0, n_pairs // 2, body, 0)

    @pl.when(n_pairs % 2 == 1)
    def _():
        emit([(2 * qi - 2, None, ex), (2 * qi - 1, None, ex)])

    @pl.when(qi > 0)
    def _():
        emit([(qsub * qi + u, u, ex) for u in range(qsub)])

    @pl.when(qi == 0)
    def _():
        emit([(0, 0, ex0), (1, 1, ex)])

    for g in range(heads):
        for u in range(qsub):
            c = g * qsub + u
            acc = acc_ref[c]
            o_ref[u * tile:(u + 1) * tile, g * dv:(g + 1) * dv] = (
                acc[:dv] * (1.0 / acc[dv:dv + 1])).T.astype(o_ref.dtype)
            ok_ref[0, c] = jnp.max(jnp.where(acc - acc == 0.0, 0.0, 1.0), axis=0, keepdims=True)


def _mla_mixer(x2d, batch, cos_t, sin_t, w_in, q_norm, kv_norm, w_qb, w_kvb):
    T, D = x2d.shape
    H, dn, dr, dv = MLA_HEADS, MLA_NOPE, MLA_ROPE, MLA_V
    seq = T // batch
    tm = TOKEN_TILE
    ql, kl = MLA_Q_LORA, MLA_KV_LORA
    half = dr // 2

    def rot(w):
        return jnp.concatenate([-w[..., half:], w[..., :half]], axis=-1)

    wcq = w_in[:, :ql].astype(BF16)
    wckv = w_in[:, ql:ql + kl].astype(BF16)
    wr = w_in[:, ql + kl:]
    zr = jnp.zeros((D, LANES - dr), F32)
    wkr = jnp.concatenate([wr, zr, rot(wr), zr], axis=1).astype(BF16)
    wq3 = w_qb.reshape(ql, H, dn + dr)
    zq = jnp.zeros((ql, H, LANES - dr), F32)
    wq = jnp.concatenate([wq3, zq], axis=2).reshape(ql, H * 2 * LANES).astype(BF16)
    wqr = jnp.concatenate([rot(wq3[:, :, dn:]), zq], axis=2).reshape(ql, H * LANES).astype(BF16)
    wkv3 = w_kvb.reshape(kl, H, dn + dv)
    wkn = wkv3[:, :, :dn].reshape(kl, H * dn).astype(BF16)
    wvt = wkv3[:, :, dn:].reshape(kl, H * dv).T.astype(BF16)
    qn = q_norm.reshape(1, ql)
    kvn = kv_norm.reshape(1, kl)

    full = lambda a: pl.BlockSpec(a.shape, lambda i: (0,) * a.ndim)
    q, k, vt = pl.pallas_call(
        _mla_proj_kernel,
        grid=(T // tm,),
        in_specs=[pl.BlockSpec((tm, D), lambda i: (i, 0)),
                  pl.BlockSpec((tm, LANES), lambda i: (i, 0)),
                  pl.BlockSpec((tm, LANES), lambda i: (i, 0)),
                  full(wcq), full(wckv), full(wkr), full(qn), full(kvn),
                  full(wq), full(wqr), full(wkn), full(wvt)],
        out_specs=[pl.BlockSpec((H, tm, 2 * LANES), lambda i: (0, i, 0)),
                   pl.BlockSpec((H, tm, 2 * LANES), lambda i: (0, i, 0)),
                   pl.BlockSpec((H * dv, tm), lambda i: (0, i))],
        out_shape=[jax.ShapeDtypeStruct((H, T, 2 * LANES), BF16),
                   jax.ShapeDtypeStruct((H, T, 2 * LANES), BF16),
                   jax.ShapeDtypeStruct((H * dv, T), BF16)],
        compiler_params=_params("parallel"),
        name="mla_proj",
    )(x2d, cos_t, sin_t, wcq, wckv, wkr, qn, kvn, wq, wqr, wkn, wvt)

    tile = min(ATTN_TILE, seq)
    hp = ATTN_HEADS_PER_STEP
    qsub = min(ATTN_QUERY_TILES_PER_STEP, seq // tile)
    tq = tile * qsub
    nq = seq // tq
    nhp = H // hp
    nc = hp * qsub

    def attention(lagged):
        return pl.pallas_call(
            functools.partial(_attn_kernel, tile=tile, heads=hp, qsub=qsub, lagged=lagged),
            grid=(batch, nhp, nq),
            in_specs=[pl.BlockSpec((hp, tq, 2 * LANES), lambda b, h, i: (h, b * nq + i, 0)),
                      pl.BlockSpec((hp, seq, 2 * LANES), lambda b, h, i: (h, b, 0)),
                      pl.BlockSpec((hp * dv, seq), lambda b, h, i: (h, b))],
            out_specs=[pl.BlockSpec((tq, hp * dv), lambda b, h, i: (b * nq + i, h)),
                       pl.BlockSpec((1, nc, 1, tile), lambda b, h, i: ((b * nhp + h) * nq + i, 0, 0, 0))],
            out_shape=[jax.ShapeDtypeStruct((T, H * dv), BF16),
                       jax.ShapeDtypeStruct((batch * nhp * nq, nc, 1, tile), F32)],
            scratch_shapes=[pltpu.VMEM((nc, 1, tile), F32), pltpu.VMEM((nc, 1, tile), F32),
                            pltpu.VMEM((nc, dv + BF16_SUBLANES, tile), F32)],
            compiler_params=_params("parallel", "parallel", "arbitrary"),
            name="mla_attention" if lagged else "mla_attention_exact",
        )(q, k, vt)

    out, not_finite = attention(lagged=True)
    return lax.cond(jnp.any(not_finite > 0.0), lambda: attention(lagged=False)[0], lambda: out)


def _post_kernel(h_ref, x_ref, w_ref, g_ref, b_ref, wr_ref, br_ref, tri_ref,
                 xo_ref, xp_ref, idx_ref, gate_ref, pos_ref, cnt_ref, run_ref):
    E = N_EXPERTS
    tm = x_ref.shape[0]

    @pl.when(pl.program_id(0) == 0)
    def _():
        run_ref[...] = jnp.zeros_like(run_ref)

    rb = tri_ref.shape[0]
    blocks = [pl.ds(r, rb) for r in range(0, tm, rb)]
    zs = [DEEPNORM_ALPHA * x_ref[rows, :]
          + jnp.dot(h_ref[rows, :], w_ref[...], preferred_element_type=F32) for rows in blocks]
    wh, wl = wr_ref[0], wr_ref[1]
    ntdot = lambda a, b: lax.dot_general(a, b, _NT, preferred_element_type=F32)
    e_iota = lax.broadcasted_iota(I32, (E, rb), 0)
    for rows, z in zip(blocks, zs):
        x1 = _layer_norm(z, g_ref[...], b_ref[...])
        xo_ref[rows, :] = x1
        _store_packed(xp_ref, x1, rows)

        xh = x1.astype(BF16)
        xl = (x1 - xh.astype(F32)).astype(BF16)
        rest = ntdot(wh, xh) + ntdot(wh, xl) + ntdot(wl, xh) + br_ref[...]
        vals, sels = [], []
        for k in range(TOP_K):
            v = jnp.max(rest, axis=0, keepdims=True)
            ik = jnp.min(jnp.where(rest == v, e_iota, E), axis=0, keepdims=True)
            sel = e_iota == ik
            rest = jnp.where(sel, -jnp.inf, rest)
            vals.append(v)
            sels.append(sel)
            idx_ref[k:k + 1, rows] = ik
        ex = [jnp.exp(v - vals[0]) for v in vals]
        inv = 1.0 / sum(ex)
        for k in range(TOP_K):
            gate_ref[k:k + 1, rows] = ex[k] * inv

        chosen = functools.reduce(jnp.logical_or, sels)
        before = jnp.dot(chosen.astype(BF16), tri_ref[...], preferred_element_type=F32)
        run = run_ref[:, 0:1]
        rank = before + run
        for k in range(TOP_K):
            pos_ref[k:k + 1, rows] = jnp.sum(jnp.where(sels[k], rank, 0.0), axis=0,
                                             keepdims=True).astype(I32)
        run_ref[...] = jnp.broadcast_to(run + jnp.sum(chosen.astype(F32), axis=1, keepdims=True),
                                        run_ref.shape)
    cnt_ref[...] = run_ref[...]


def _post_mixer(h, x2d, w_out, ln_g, ln_b, w_router, b_router):
    T, D = x2d.shape
    nch = D // (2 * SC_COLS)
    E = N_EXPERTS
    tm = TOKEN_TILE
    wo = w_out.astype(BF16)
    wrt = w_router.T
    wrh = wrt.astype(BF16)
    wr = jnp.stack([wrh, (wrt - wrh.astype(F32)).astype(BF16)])
    full = lambda a: pl.BlockSpec(a.shape, lambda i: (0,) * a.ndim)
    g, b, br = ln_g.reshape(1, D), ln_b.reshape(1, D), b_router.reshape(E, 1)
    pos_ids = jnp.arange(POST_ROW_BLOCK, dtype=I32)
    tri = (pos_ids[:, None] < pos_ids[None, :]).astype(BF16)
    rows = TOP_K
    return pl.pallas_call(
        _post_kernel,
        grid=(T // tm,),
        in_specs=[pl.BlockSpec((tm, h.shape[1]), lambda i: (i, 0)),
                  pl.BlockSpec((tm, D), lambda i: (i, 0)),
                  full(wo), full(g), full(b), full(wr), full(br), full(tri)],
        out_specs=[pl.BlockSpec((tm, D), lambda i: (i, 0)),
                   pl.BlockSpec((nch, tm, SC_COLS), lambda i: (0, i, 0)),
                   pl.BlockSpec((rows, tm), lambda i: (0, i)),
                   pl.BlockSpec((rows, tm), lambda i: (0, i)),
                   pl.BlockSpec((rows, tm), lambda i: (0, i)),
                   pl.BlockSpec((E, LANES), lambda i: (0, 0))],
        out_shape=[jax.ShapeDtypeStruct((T, D), F32),
                   jax.ShapeDtypeStruct((nch, T, SC_COLS), U32),
                   jax.ShapeDtypeStruct((rows, T), I32),
                   jax.ShapeDtypeStruct((rows, T), F32),
                   jax.ShapeDtypeStruct((rows, T), I32),
                   jax.ShapeDtypeStruct((E, LANES), F32)],
        scratch_shapes=[pltpu.VMEM((E, LANES), F32)],
        compiler_params=_params("arbitrary"),
        name="post_mixer_router",
    )(h, x2d, wo, g, b, wr, br, tri)


def _sc_mesh():
    return plsc.VectorSubcoreMesh(core_axis_name="core", subcore_axis_name="subcore")


def _sc_scatter_rows(xc, dest, n_rows):
    nch, T, C = xc.shape
    K = dest.shape[0]
    W = SC_WINDOW
    xs = xc.reshape(nch * T, C)
    nb = (nch * T) // W
    offs = (jnp.arange(nch, dtype=I32) * n_rows)[None, :, None]
    idx = (dest[:, None, :] + offs).reshape(K, nch * T)

    @functools.partial(pl.kernel, out_type=jax.ShapeDtypeStruct((nch * n_rows, C), xc.dtype),
                       mesh=_sc_mesh(), scratch_types=[], name="moe_dispatch_scatter")
    def scatter(x_hbm, i_hbm, o_hbm):
        def body(x_vmem, i_vmem):
            for k in range(K):
                pltpu.sync_copy(x_vmem, o_hbm.at[i_vmem.at[k]])

        pltpu.emit_pipeline(
            body,
            grid=(nb,),
            in_specs=[pl.BlockSpec((W, C), lambda g: (g, 0)),
                      pl.BlockSpec((K, W), lambda g: (0, g))],
            out_specs=[],
            core_axis_name=("core", "subcore"),
            dimension_semantics=(pltpu.PARALLEL,),
        )(x_hbm, i_hbm)

    return scatter(xs, idx).reshape(nch, n_rows, C)


def _sc_gather_rows(yc, dest):
    nch, n_rows, C = yc.shape
    K, T = dest.shape
    W = SC_WINDOW
    ys = yc.reshape(nch * n_rows, C)
    offs = (jnp.arange(nch, dtype=I32) * n_rows)[:, None, None]
    idx = (dest[None, :, :] + offs).reshape(1, nch * K * T)
    n_sub = nch * K * T

    @functools.partial(pl.kernel, out_type=jax.ShapeDtypeStruct((n_sub, C), yc.dtype),
                       mesh=_sc_mesh(), scratch_types=[], name="moe_combine_gather")
    def gather(t_hbm, i_hbm, o_hbm):
        def body(i_vmem, o_vmem):
            pltpu.sync_copy(t_hbm.at[i_vmem.at[0]], o_vmem)

        pltpu.emit_pipeline(
            body,
            grid=(n_sub // W,),
            in_specs=[pl.BlockSpec((1, W), lambda g: (0, g))],
            out_specs=[pl.BlockSpec((W, C), lambda g: (g, 0))],
            core_axis_name=("core", "subcore"),
            dimension_semantics=(pltpu.PARALLEL,),
        )(i_hbm, o_hbm)

    return gather(ys, idx).reshape(nch, K, T, C)


def _expert_kernel(te_ref, nu_ref, x_ref, wgu_ref, bg_ref, bl_ref, wd_ref, bd_ref, perm_ref,
                   y_ref, wg_s, wl_s, wd_s):
    i = pl.program_id(0)
    active = i < nu_ref[0]
    e = te_ref[i]
    changed = jnp.logical_or(i == 0, e != te_ref[jnp.maximum(i - 1, 0)])
    dot = functools.partial(jnp.dot, preferred_element_type=F32)

    @pl.when(jnp.logical_and(active, changed))
    def _():
        n_blocks = wgu_ref.shape[3] // MXU_DIM
        for blk in range(n_blocks):
            wb = wgu_ref[0, 0, :, blk * MXU_DIM:(blk + 1) * MXU_DIM].astype(BF16)
            wp = dot(wb, perm_ref[...]).astype(BF16)
            wg_s[:, blk * LANES:(blk + 1) * LANES] = wp[:, :LANES]
            wl_s[:, blk * LANES:(blk + 1) * LANES] = wp[:, LANES:]
        wd_s[...] = wd_ref[0, 0].astype(BF16)

    @pl.when(active)
    def _():
        xb = _load_packed(x_ref).astype(BF16)
        g = jnp.minimum(dot(xb, wg_s[...]) + bg_ref[0], SWIGLU_LIMIT)
        lin = jnp.clip(dot(xb, wl_s[...]) + bl_ref[0], -SWIGLU_LIMIT, SWIGLU_LIMIT)
        act = (lin + 1.0) * g * jax.nn.sigmoid(SWIGLU_ALPHA * g)
        _store_packed(y_ref, dot(act.astype(BF16), wd_s[...]) + bd_ref[0])


def _combine_kernel(y0_ref, y1_ref, y2_ref, y3_ref, gate_ref, x_ref, g_ref, b_ref, o_ref):
    gate = gate_ref[...]
    ff = (gate[:, 0:1] * _load_packed(y0_ref) + gate[:, 1:2] * _load_packed(y1_ref)
          + gate[:, 2:3] * _load_packed(y2_ref) + gate[:, 3:4] * _load_packed(y3_ref))
    o_ref[...] = _layer_norm(DEEPNORM_ALPHA * x_ref[...] + ff, g_ref[...], b_ref[...])


def _moe_dispatch(x1p, idx, pos, cnt):
    nch, T, C = x1p.shape
    E, K = N_EXPERTS, TOP_K
    te = EXPERT_TILE
    n_tiles = (T * K) // te + E
    n_rows = n_tiles * te

    counts = cnt[:, 0].astype(I32)
    padded = (counts + te - 1) // te * te
    pad_end = jnp.cumsum(padded)
    pad_start = pad_end - padded
    experts = jnp.arange(E, dtype=I32)
    start_of = jnp.sum(jnp.where(idx[None] == experts[:, None, None],
                                 pad_start[:, None, None], 0), axis=0)
    dest = start_of + pos
    tile_start = jnp.arange(n_tiles, dtype=I32) * te
    tile_expert = jnp.minimum(
        jnp.sum((pad_end[None, :] <= tile_start[:, None]).astype(I32), axis=1), E - 1)
    n_used = (pad_end[-1] // te).astype(I32).reshape(1)

    x_rows = _sc_scatter_rows(x1p, dest, n_rows)
    return x_rows, dest, tile_expert, n_used


def _moe_experts(x_rows, tile_expert, n_used, layer, w_gu, b_gu, w_down, b_down):
    nch, n_rows, C = x_rows.shape
    E = N_EXPERTS
    D, dff = w_down.shape[3], w_down.shape[2]
    te = EXPERT_TILE
    n_tiles = n_rows // te
    half = MXU_DIM // 2
    src = jnp.arange(MXU_DIM)
    perm = (src[:, None] == jnp.where(src < half, 2 * src, 2 * (src - half) + 1)[None, :])
    perm = perm.astype(BF16)
    bg = b_gu[:, 0::2].reshape(E, 1, dff)
    bl = b_gu[:, 1::2].reshape(E, 1, dff)
    bd = b_down.reshape(E, 1, D)

    def row_map(i, te_ref, nu_ref):
        return (0, jnp.minimum(i, nu_ref[0] - 1), 0)

    def exp_map(i, te_ref, nu_ref):
        return (te_ref[i], 0, 0)

    def stacked_map(i, te_ref, nu_ref):
        return (layer, te_ref[i], 0, 0)

    return pl.pallas_call(
        _expert_kernel,
        grid_spec=pltpu.PrefetchScalarGridSpec(
            num_scalar_prefetch=2,
            grid=(n_tiles,),
            in_specs=[pl.BlockSpec((nch, te, C), row_map),
                      pl.BlockSpec((1, 1, D, 2 * dff), stacked_map),
                      pl.BlockSpec((1, 1, dff), exp_map),
                      pl.BlockSpec((1, 1, dff), exp_map),
                      pl.BlockSpec((1, 1, dff, D), stacked_map),
                      pl.BlockSpec((1, 1, D), exp_map),
                      pl.BlockSpec((MXU_DIM, MXU_DIM), lambda i, a, b: (0, 0))],
            out_specs=pl.BlockSpec((nch, te, C), row_map),
            scratch_shapes=[pltpu.VMEM((D, dff), BF16), pltpu.VMEM((D, dff), BF16),
                            pltpu.VMEM((dff, D), BF16)],
        ),
        out_shape=jax.ShapeDtypeStruct((nch, n_rows, C), U32),
        compiler_params=_params("arbitrary"),
        name="moe_experts",
    )(tile_expert, n_used, x_rows, w_gu, bg, bl, w_down, bd, perm)


def _moe_combine(y_tok, gate, x1, ln_g, ln_b):
    nch, K, T, C = y_tok.shape
    D = x1.shape[1]
    tm = TOKEN_TILE
    gate_col = gate.T
    slot = lambda k: pl.BlockSpec((nch, None, tm, C), lambda i, k=k: (0, k, i, 0))
    vec = pl.BlockSpec((1, D), lambda i: (0, 0))
    return pl.pallas_call(
        _combine_kernel,
        grid=(T // tm,),
        in_specs=[slot(0), slot(1), slot(2), slot(3),
                  pl.BlockSpec((tm, K), lambda i: (i, 0)),
                  pl.BlockSpec((tm, D), lambda i: (i, 0)), vec, vec],
        out_specs=pl.BlockSpec((tm, D), lambda i: (i, 0)),
        out_shape=jax.ShapeDtypeStruct((T, D), F32),
        compiler_params=_params("parallel"),
        name="moe_combine_norm",
    )(y_tok, y_tok, y_tok, y_tok, gate_col, x1, ln_g.reshape(1, D), ln_b.reshape(1, D))


def kernel(x, positions, ln_gain, ln_bias, mlstm_w_in, mlstm_b_gates, mlstm_norm_gain,
           mlstm_w_out, mla_w_in, mla_q_norm, mla_kv_norm, mla_w_qb, mla_w_kvb, mla_w_out,
           moe_w_router, moe_b_router, moe_w_gate_up, moe_b_gate_up, moe_w_down, moe_b_down):
    B, S, D = x.shape
    T = B * S
    x2d = x.reshape(T, D)

    inv_freq = ROPE_THETA ** (-jnp.arange(0, MLA_ROPE, 2, dtype=F32) / MLA_ROPE)
    ang = positions.astype(F32).reshape(T, 1) * inv_freq
    reps = LANES // ang.shape[1]
    cos_t = jnp.tile(jnp.cos(ang), (1, reps))
    sin_t = jnp.tile(jnp.sin(ang), (1, reps))

    ns = TOKEN_STREAMS if B % TOKEN_STREAMS == 0 else 1
    bs, ts = B // ns, T // ns
    streams = range(ns)
    xs = [x2d[i * ts:(i + 1) * ts] for i in streams]
    cos_s = [cos_t[i * ts:(i + 1) * ts] for i in streams]
    sin_s = [sin_t[i * ts:(i + 1) * ts] for i in streams]

    for layer in range(DEPTH):
        j = layer // 2
        if layer % 2 == 0:
            hs = [_mlstm_mixer(xs[i], bs, mlstm_w_in[j], mlstm_b_gates[j], mlstm_norm_gain[j])
                  for i in streams]
            w_out = mlstm_w_out[j]
        else:
            hs = [_mla_mixer(xs[i], bs, cos_s[i], sin_s[i], mla_w_in[j], mla_q_norm[j],
                             mla_kv_norm[j], mla_w_qb[j], mla_w_kvb[j]) for i in streams]
            w_out = mla_w_out[j]
        routed, sent = [], []
        for i in streams:
            routed.append(_post_mixer(hs[i], xs[i], w_out, ln_gain[layer, 0], ln_bias[layer, 0],
                                      moe_w_router[layer], moe_b_router[layer]))
            x1, x1p, idx, gate, pos, cnt = routed[i]
            sent.append(_moe_dispatch(x1p, idx, pos, cnt))
        y_tok = []
        for i in streams:
            x_rows, dest, tile_expert, n_used = sent[i]
            y_rows = _moe_experts(x_rows, tile_expert, n_used, layer, moe_w_gate_up,
                                  moe_b_gate_up[layer], moe_w_down, moe_b_down[layer])
            y_tok.append(_sc_gather_rows(y_rows, dest))
        xs = [_moe_combine(y_tok[i], routed[i][3], routed[i][0], ln_gain[layer, 1], ln_bias[layer, 1])
              for i in streams]
    return jnp.concatenate(xs, axis=0).reshape(B, S, D)
```

```python
import functools

import jax
import jax.numpy as jnp
from jax import lax
from jax.experimental import pallas as pl
from jax.experimental.pallas import tpu as pltpu
from jax.experimental.pallas import tpu_sc as plsc

F32 = jnp.float32
BF16 = jnp.bfloat16
I32 = jnp.int32
U32 = jnp.uint32

DEPTH = 4
MLSTM_HEADS = 8
MLSTM_DQK = 64
MLSTM_DV = 128
MLA_HEADS = 8
MLA_Q_LORA = 384
MLA_KV_LORA = 256
MLA_NOPE = 128
MLA_ROPE = 64
MLA_V = 128
ROPE_THETA = 10000.0
N_EXPERTS = 32
TOP_K = 4
SWIGLU_LIMIT = 7.0
SWIGLU_ALPHA = 1.702
DEEPNORM_ALPHA = (2.0 * DEPTH) ** 0.25
LN_EPS = 1e-5
LOG2_E = 1.4426950408889634
RMS_EPS = 1e-6

LANES = 128
BF16_SUBLANES = 16
MXU_DIM = 256
VMEM_LIMIT_BYTES = 56 * 1024 * 1024

TOKEN_TILE = 1024
MLSTM_SEQ_BLOCK = 1024
MLSTM_CHUNK = 256
ATTN_TILE = 512
ATTN_HEADS_PER_STEP = 2
ATTN_QUERY_TILES_PER_STEP = 2
ATTN_SCORE_LOOKAHEAD = 1
EXPERT_TILE = 512
SC_WINDOW = 128
SC_COLS = 256

_NT = (((1,), (1,)), ((), ()))
HALF_WORD_BITS = 16
HIGH_HALF_MASK = 0xFFFF0000


def _params(*sem):
    return pltpu.CompilerParams(dimension_semantics=sem, vmem_limit_bytes=VMEM_LIMIT_BYTES)


def _layer_norm(z, g, b):
    mu = jnp.mean(z, axis=-1, keepdims=True)
    zc = z - mu
    var = jnp.mean(zc * zc, axis=-1, keepdims=True)
    return zc * lax.rsqrt(var + LN_EPS) * g + b


def _rms_norm(z, g):
    return z * lax.rsqrt(jnp.mean(z * z, axis=-1, keepdims=True) + RMS_EPS) * g


def _store_packed(ref, a):
    half = a.shape[1] // 2
    rounded = lambda v: lax.bitcast_convert_type(v.astype(BF16).astype(F32), U32)
    words = ((rounded(a[:, :half]) >> HALF_WORD_BITS)
             | (rounded(a[:, half:]) & jnp.uint32(HIGH_HALF_MASK)))
    for c in range(ref.shape[0]):
        ref[c] = words[:, c * SC_COLS:(c + 1) * SC_COLS]


def _load_packed(ref):
    chunks = [ref[c] for c in range(ref.shape[0])]
    lo = [lax.bitcast_convert_type(w << HALF_WORD_BITS, F32) for w in chunks]
    hi = [lax.bitcast_convert_type(w & jnp.uint32(HIGH_HALF_MASK), F32) for w in chunks]
    return jnp.concatenate(lo + hi, axis=1)


def _split3(a):
    hi = a.astype(BF16)
    r = a - hi.astype(F32)
    mid = r.astype(BF16)
    lo = (r - mid.astype(F32)).astype(BF16)
    return hi, mid, lo


def _mlstm_inproj_kernel(x_ref, wq_ref, wkt_ref, wv_ref, wo_ref, wg_ref, bg_ref,
                         q_ref, kt_ref, v_ref, o_ref, gc_ref, gr_ref):
    xb = x_ref[...].astype(BF16)
    dot = functools.partial(jnp.dot, preferred_element_type=F32)
    q_ref[...] = (dot(xb, wq_ref[...]) * (MLSTM_DQK ** -0.5)).astype(BF16)
    kt_ref[...] = lax.dot_general(wkt_ref[...], xb, _NT, preferred_element_type=F32).astype(BF16)
    v_ref[...] = dot(xb, wv_ref[...]).astype(BF16)
    o_ref[...] = dot(xb, wo_ref[...])
    z = dot(xb, wg_ref[...]) + bg_ref[...]
    lane = lax.broadcasted_iota(I32, z.shape, 1)
    log_sig = jnp.minimum(z, 0.0) - jnp.log1p(jnp.exp(-jnp.abs(z)))
    g = jnp.where(lane < MLSTM_HEADS, z, log_sig)
    gc_ref[...] = g
    gr_ref[...] = g.T[:2 * MLSTM_HEADS, :]


def _mlstm_cell_kernel(q_ref, kt_ref, v_ref, o_ref, gc_ref, gr_ref, ng_ref, out_ref,
                       c_ref, m_ref, *, chunk, n_chunks):
    H, dk, dv = MLSTM_HEADS, MLSTM_DQK, MLSTM_DV

    @pl.when(pl.program_id(1) == 0)
    def _():
        c_ref[...] = jnp.zeros_like(c_ref)
        m_ref[...] = jnp.zeros_like(m_ref)

    row = lax.broadcasted_iota(I32, (chunk, chunk), 0)
    col = lax.broadcasted_iota(I32, (chunk, chunk), 1)
    causal = col <= row
    tri_lower = causal.astype(BF16)
    tri_upper = (row <= col).astype(BF16)
    ones_col = (lax.broadcasted_iota(I32, (chunk, LANES), 1) == 0).astype(BF16)
    dot = functools.partial(jnp.dot, preferred_element_type=F32)

    def chunk_body(c, carry):
        r0 = pl.multiple_of(c * chunk, chunk)
        rows = pl.ds(r0, chunk)
        gc = gc_ref[rows, :]
        gr = gr_ref[:, rows]
        bc = sum(dot(tri_lower, p) for p in _split3(gc))
        br = sum(dot(p, tri_upper) for p in _split3(gr))
        hs = range(H)
        b_c = [bc[:, H + h:H + h + 1] for h in hs]
        ig_r = [gr[h:h + 1, :] for h in hs]
        b_r = [br[H + h:H + h + 1, :] for h in hs]
        b_last = [b[chunk - 1:chunk, :] for b in b_c]
        m_lanes = [m_ref[h:h + 1, :] for h in hs]
        m_old = [m[:, 0:1] for m in m_lanes]
        qh = [q_ref[rows, h * dk:(h + 1) * dk] for h in hs]
        kth = [kt_ref[h * dk:(h + 1) * dk, rows] for h in hs]
        vh = [v_ref[rows, h * dv:(h + 1) * dv] for h in hs]
        ct = [c_ref[h] for h in hs]

        qk = [dot(qh[h], kth[h]) for h in hs]
        qc = [dot(qh[h], ct[h].astype(BF16)) for h in hs]

        m_new = [jnp.maximum(b_last[h] + m_old[h],
                             jnp.max(b_last[h] - b_r[h] + ig_r[h], axis=1, keepdims=True)) for h in hs]
        ktw = [(kth[h].astype(F32) * jnp.exp(b_last[h] - b_r[h] + ig_r[h] - m_new[h])).astype(BF16)
               for h in hs]
        for h in hs:
            vaug = jnp.concatenate([vh[h], ones_col], axis=1)
            c_ref[h] = jnp.exp(b_last[h] + m_old[h] - m_new[h]) * ct[h] + dot(ktw[h], vaug)
            m_ref[h:h + 1, :] = jnp.broadcast_to(m_new[h], (1, LANES))

        a_mat = [jnp.where(causal, ig_r[h] - b_r[h], -jnp.inf) for h in hs]
        m_rep = [jnp.broadcast_to(m_lanes[h], (chunk, LANES)) for h in hs]
        g = [jnp.maximum(m_rep[h], jnp.max(a_mat[h], axis=1, keepdims=True)) for h in hs]
        s = [qk[h] * jnp.exp(a_mat[h] - jnp.concatenate([g[h]] * (chunk // LANES), axis=1))
             for h in hs]
        w_inter = [jnp.exp(m_rep[h] - g[h]) for h in hs]
        nd = [dot(s[h].astype(BF16), jnp.concatenate([vh[h], ones_col], axis=1))
              + jnp.concatenate([w_inter[h]] * (2 * dv // LANES), axis=1) * qc[h]
              for h in hs]
        for h in hs:
            num, den = nd[h][:, :dv], nd[h][:, dv:dv + 1]
            r = 1.0 / jnp.maximum(jnp.abs(den), jnp.exp(-(b_c[h] + g[h][:, 0:1])))
            scale = r * lax.rsqrt(r * r * jnp.mean(num * num, axis=1, keepdims=True) + RMS_EPS)
            og = o_ref[rows, h * dv:(h + 1) * dv]
            out_ref[rows, h * dv:(h + 1) * dv] = (
                jax.nn.sigmoid(og) * (num * scale * ng_ref[:, h * dv:(h + 1) * dv])
            ).astype(out_ref.dtype)
        return carry

    lax.fori_loop(0, n_chunks, chunk_body, 0)


def _mlstm_mixer(x2d, batch, w_in, b_gates, norm_gain):
    T, D = x2d.shape
    H, dk, dv = MLSTM_HEADS, MLSTM_DQK, MLSTM_DV
    seq = T // batch
    tm = TOKEN_TILE
    cq, ck, cv, co = H * dk, 2 * H * dk, 2 * H * dk + H * dv, 2 * H * dk + 2 * H * dv
    wq = w_in[:, :cq].astype(BF16)
    wkt = w_in[:, cq:ck].T.astype(BF16)
    wv = w_in[:, ck:cv].astype(BF16)
    wo = w_in[:, cv:co].astype(BF16)
    wg = jnp.pad(w_in[:, co:], ((0, 0), (0, LANES - 2 * H))).astype(BF16)
    bg = jnp.pad(b_gates, (0, LANES - 2 * H)).reshape(1, LANES)

    full = lambda a: pl.BlockSpec(a.shape, lambda i: (0,) * a.ndim)
    q, kt, v, o, gc, gr = pl.pallas_call(
        _mlstm_inproj_kernel,
        grid=(T // tm,),
        in_specs=[pl.BlockSpec((tm, D), lambda i: (i, 0)), full(wq), full(wkt), full(wv), full(wo),
                  full(wg), full(bg)],
        out_specs=[pl.BlockSpec((tm, H * dk), lambda i: (i, 0)),
                   pl.BlockSpec((H * dk, tm), lambda i: (0, i)),
                   pl.BlockSpec((tm, H * dv), lambda i: (i, 0)),
                   pl.BlockSpec((tm, H * dv), lambda i: (i, 0)),
                   pl.BlockSpec((tm, LANES), lambda i: (i, 0)),
                   pl.BlockSpec((2 * H, tm), lambda i: (0, i))],
        out_shape=[jax.ShapeDtypeStruct((T, H * dk), BF16),
                   jax.ShapeDtypeStruct((H * dk, T), BF16),
                   jax.ShapeDtypeStruct((T, H * dv), BF16),
                   jax.ShapeDtypeStruct((T, H * dv), F32),
                   jax.ShapeDtypeStruct((T, LANES), F32),
                   jax.ShapeDtypeStruct((2 * H, T), F32)],
        compiler_params=_params("parallel"),
        name="mlstm_inproj",
    )(x2d, wq, wkt, wv, wo, wg, bg)

    ts = min(MLSTM_SEQ_BLOCK, seq)
    chunk = min(MLSTM_CHUNK, ts)
    nsb = seq // ts
    ng = norm_gain.reshape(1, H * dv)
    return pl.pallas_call(
        functools.partial(_mlstm_cell_kernel, chunk=chunk, n_chunks=ts // chunk),
        grid=(batch, nsb),
        in_specs=[pl.BlockSpec((ts, H * dk), lambda b, s: (b * nsb + s, 0)),
                  pl.BlockSpec((H * dk, ts), lambda b, s: (0, b * nsb + s)),
                  pl.BlockSpec((ts, H * dv), lambda b, s: (b * nsb + s, 0)),
                  pl.BlockSpec((ts, H * dv), lambda b, s: (b * nsb + s, 0)),
                  pl.BlockSpec((ts, LANES), lambda b, s: (b * nsb + s, 0)),
                  pl.BlockSpec((2 * H, ts), lambda b, s: (0, b * nsb + s)),
                  pl.BlockSpec((1, H * dv), lambda b, s: (0, 0))],
        out_specs=pl.BlockSpec((ts, H * dv), lambda b, s: (b * nsb + s, 0)),
        out_shape=jax.ShapeDtypeStruct((T, H * dv), BF16),
        scratch_shapes=[pltpu.VMEM((H, dk, 2 * dv), F32), pltpu.VMEM((H, LANES), F32)],
        compiler_params=_params("parallel", "arbitrary"),
        name="mlstm_cell",
    )(q, kt, v, o, gc, gr, ng)


def _mla_proj_kernel(x_ref, cos_ref, sin_ref, wcq_ref, wckv_ref, wkr_ref, qn_ref, kvn_ref,
                     wq_ref, wqr_ref, wkn_ref, wvt_ref, q_ref, k_ref, vt_ref):
    H, dn = MLA_HEADS, MLA_NOPE
    dot = functools.partial(jnp.dot, preferred_element_type=F32)
    xb = x_ref[...].astype(BF16)
    cos, sin = cos_ref[...], sin_ref[...]
    c_q = _rms_norm(dot(xb, wcq_ref[...]), qn_ref[...]).astype(BF16)
    c_kv = _rms_norm(dot(xb, wckv_ref[...]), kvn_ref[...]).astype(BF16)
    kr2 = dot(xb, wkr_ref[...])
    kr = (kr2[:, :LANES] * cos + kr2[:, LANES:] * sin).astype(BF16)
    scale = (MLA_NOPE + MLA_ROPE) ** -0.5 * LOG2_E
    qa = dot(c_q, wq_ref[...])
    qr = dot(c_q, wqr_ref[...])
    kn = dot(c_kv, wkn_ref[...])
    for h in range(H):
        base = h * 2 * LANES
        q_ref[h, :, :dn] = (qa[:, base:base + dn] * scale).astype(BF16)
        rope = qa[:, base + dn:base + 2 * LANES] * cos + qr[:, h * LANES:(h + 1) * LANES] * sin
        q_ref[h, :, dn:] = (rope * scale).astype(BF16)
        k_ref[h, :, :dn] = kn[:, h * dn:(h + 1) * dn].astype(BF16)
        k_ref[h, :, dn:] = kr
    vt_ref[...] = lax.dot_general(wvt_ref[...], c_kv, _NT, preferred_element_type=F32).astype(BF16)


def _attn_kernel(q_ref, k_ref, vt_ref, o_ref, bad_ref, m_ref, mu_ref, acc_ref,
                 *, tile, heads, qsub, lagged):
    assert qsub == 2, "the key-tile bookkeeping below is written for two query tiles per step"
    dv = MLA_V
    qi = pl.program_id(2)
    m_ref[...] = jnp.full_like(m_ref, -jnp.inf)
    mu_ref[...] = jnp.full_like(mu_ref, -jnp.inf)
    acc_ref[...] = jnp.zeros_like(acc_ref)
    ones_rows = jnp.ones((acc_ref.shape[1] - dv, tile), BF16)

    def emit(tiles):
        units = [(j, ds, ex, g, u) for j, ds, ex in tiles for g in range(heads)
                 for u in range(qsub) if ds is None or u >= ds]
        keys_of = lambda j: pl.ds(pl.multiple_of(j * tile, tile), tile)
        sts = {}

        def score(i):
            j, ds, ex, g, u = units[i]
            st = lax.dot_general(k_ref[g, keys_of(j), :], q_ref[g, u * tile:(u + 1) * tile, :], _NT,
                                 preferred_element_type=F32)
            if u == ds:
                kpos = lax.broadcasted_iota(I32, st.shape, 0)
                qpos = lax.broadcasted_iota(I32, st.shape, 1)
                st = jnp.where(kpos <= qpos, st, -jnp.inf)
            sts[i] = st

        def softmax_value(i):
            j, ds, ex, g, u = units[i]
            c = g * qsub + u
            st = sts.pop(i)
            tile_max = jnp.max(st, axis=0, keepdims=True)
            seen = m_ref[c]
            if ex == "two_pass":
                stab = jnp.maximum(seen, tile_max)
            elif ex == "key0":
                stab = st[0:1, :]
            else:
                stab = seen
            p = jnp.exp2(st - stab).astype(BF16)
            vt = jnp.concatenate([vt_ref[g * dv:(g + 1) * dv, keys_of(j)], ones_rows], axis=0)
            acc_ref[c] = jnp.exp2(mu_ref[c] - stab) * acc_ref[c] + jnp.dot(
                vt, p, preferred_element_type=F32)
            mu_ref[c] = stab
            m_ref[c] = jnp.maximum(seen, tile_max)

        ahead = ATTN_SCORE_LOOKAHEAD
        for i in range(min(ahead, len(units))):
            score(i)
        for i in range(len(units)):
            if i + ahead < len(units):
                score(i + ahead)
            softmax_value(i)

    ex = "earlier" if lagged else "two_pass"
    ex0 = "key0" if lagged else "two_pass"

    @pl.when(qi > 0)
    def _():
        emit([(0, None, ex0), (1, None, ex)])

    n_pairs = jnp.maximum(qi - 1, 0)

    def body(i, carry):
        emit([(2 + 2 * qsub * i + d, None, ex) for d in range(2 * qsub)])
        return carry

    lax.fori_loop(0, n_pairs // 2, body, 0)

    @pl.when(n_pairs % 2 == 1)
    def _():
        emit([(2 * qi - 2, None, ex), (2 * qi - 1, None, ex)])

    @pl.when(qi > 0)
    def _():
        emit([(qsub * qi + u, u, ex) for u in range(qsub)])

    @pl.when(qi == 0)
    def _():
        emit([(0, 0, ex0), (1, 1, ex)])

    for g in range(heads):
        for u in range(qsub):
            c = g * qsub + u
            acc = acc_ref[c]
            o_ref[u * tile:(u + 1) * tile, g * dv:(g + 1) * dv] = (
                acc[:dv] * (1.0 / acc[dv:dv + 1])).T.astype(o_ref.dtype)
            bad_ref[0, c] = jnp.max(jnp.where(acc - acc == 0.0, 0.0, 1.0), axis=0, keepdims=True)


def _mla_mixer(x2d, batch, cos_t, sin_t, w_in, q_norm, kv_norm, w_qb, w_kvb):
    T, D = x2d.shape
    H, dn, dr, dv = MLA_HEADS, MLA_NOPE, MLA_ROPE, MLA_V
    seq = T // batch
    tm = TOKEN_TILE
    ql, kl = MLA_Q_LORA, MLA_KV_LORA
    half = dr // 2

    def rot(w):
        return jnp.concatenate([-w[..., half:], w[..., :half]], axis=-1)

    wcq = w_in[:, :ql].astype(BF16)
    wckv = w_in[:, ql:ql + kl].astype(BF16)
    wr = w_in[:, ql + kl:]
    zr = jnp.zeros((D, LANES - dr), F32)
    wkr = jnp.concatenate([wr, zr, rot(wr), zr], axis=1).astype(BF16)
    wq3 = w_qb.reshape(ql, H, dn + dr)
    zq = jnp.zeros((ql, H, LANES - dr), F32)
    wq = jnp.concatenate([wq3, zq], axis=2).reshape(ql, H * 2 * LANES).astype(BF16)
    wqr = jnp.concatenate([rot(wq3[:, :, dn:]), zq], axis=2).reshape(ql, H * LANES).astype(BF16)
    wkv3 = w_kvb.reshape(kl, H, dn + dv)
    wkn = wkv3[:, :, :dn].reshape(kl, H * dn).astype(BF16)
    wvt = wkv3[:, :, dn:].reshape(kl, H * dv).T.astype(BF16)
    qn = q_norm.reshape(1, ql)
    kvn = kv_norm.reshape(1, kl)

    full = lambda a: pl.BlockSpec(a.shape, lambda i: (0,) * a.ndim)
    q, k, vt = pl.pallas_call(
        _mla_proj_kernel,
        grid=(T // tm,),
        in_specs=[pl.BlockSpec((tm, D), lambda i: (i, 0)),
                  pl.BlockSpec((tm, LANES), lambda i: (i, 0)),
                  pl.BlockSpec((tm, LANES), lambda i: (i, 0)),
                  full(wcq), full(wckv), full(wkr), full(qn), full(kvn),
                  full(wq), full(wqr), full(wkn), full(wvt)],
        out_specs=[pl.BlockSpec((H, tm, 2 * LANES), lambda i: (0, i, 0)),
                   pl.BlockSpec((H, tm, 2 * LANES), lambda i: (0, i, 0)),
                   pl.BlockSpec((H * dv, tm), lambda i: (0, i))],
        out_shape=[jax.ShapeDtypeStruct((H, T, 2 * LANES), BF16),
                   jax.ShapeDtypeStruct((H, T, 2 * LANES), BF16),
                   jax.ShapeDtypeStruct((H * dv, T), BF16)],
        compiler_params=_params("parallel"),
        name="mla_proj",
    )(x2d, cos_t, sin_t, wcq, wckv, wkr, qn, kvn, wq, wqr, wkn, wvt)

    tile = min(ATTN_TILE, seq)
    hp = ATTN_HEADS_PER_STEP
    qsub = min(ATTN_QUERY_TILES_PER_STEP, seq // tile)
    tq = tile * qsub
    nq = seq // tq
    nhp = H // hp
    nc = hp * qsub

    def attention(lagged):
        return pl.pallas_call(
            functools.partial(_attn_kernel, tile=tile, heads=hp, qsub=qsub, lagged=lagged),
            grid=(batch, nhp, nq),
            in_specs=[pl.BlockSpec((hp, tq, 2 * LANES), lambda b, h, i: (h, b * nq + i, 0)),
                      pl.BlockSpec((hp, seq, 2 * LANES), lambda b, h, i: (h, b, 0)),
                      pl.BlockSpec((hp * dv, seq), lambda b, h, i: (h, b))],
            out_specs=[pl.BlockSpec((tq, hp * dv), lambda b, h, i: (b * nq + i, h)),
                       pl.BlockSpec((1, nc, 1, tile), lambda b, h, i: ((b * nhp + h) * nq + i, 0, 0, 0))],
            out_shape=[jax.ShapeDtypeStruct((T, H * dv), BF16),
                       jax.ShapeDtypeStruct((batch * nhp * nq, nc, 1, tile), F32)],
            scratch_shapes=[pltpu.VMEM((nc, 1, tile), F32), pltpu.VMEM((nc, 1, tile), F32),
                            pltpu.VMEM((nc, dv + BF16_SUBLANES, tile), F32)],
            compiler_params=_params("parallel", "parallel", "arbitrary"),
            name="mla_attention" if lagged else "mla_attention_exact",
        )(q, k, vt)

    out, not_finite = attention(lagged=True)
    return lax.cond(jnp.any(not_finite > 0.0), lambda: attention(lagged=False)[0], lambda: out)


def _post_kernel(h_ref, x_ref, w_ref, g_ref, b_ref, wr_ref, br_ref, tri_ref,
                 xo_ref, xp_ref, idx_ref, gate_ref, pos_ref, cnt_ref, run_ref):
    E = N_EXPERTS
    tm = x_ref.shape[0]

    @pl.when(pl.program_id(0) == 0)
    def _():
        run_ref[...] = jnp.zeros_like(run_ref)

    mix = jnp.dot(h_ref[...], w_ref[...], preferred_element_type=F32)
    x1 = _layer_norm(DEEPNORM_ALPHA * x_ref[...] + mix, g_ref[...], b_ref[...])
    xo_ref[...] = x1
    _store_packed(xp_ref, x1)

    xh = x1.astype(BF16)
    xl = (x1 - xh.astype(F32)).astype(BF16)
    wh, wl = wr_ref[0], wr_ref[1]
    ntdot = lambda a, b: lax.dot_general(a, b, _NT, preferred_element_type=F32)
    rest = ntdot(wh, xh) + ntdot(wh, xl) + ntdot(wl, xh) + br_ref[...]

    e_iota = lax.broadcasted_iota(I32, (E, tm), 0)
    vals, sels = [], []
    for k in range(TOP_K):
        v = jnp.max(rest, axis=0, keepdims=True)
        ik = jnp.min(jnp.where(rest == v, e_iota, E), axis=0, keepdims=True)
        sel = e_iota == ik
        rest = jnp.where(sel, -jnp.inf, rest)
        vals.append(v)
        sels.append(sel)
        idx_ref[k:k + 1, :] = ik
    ex = [jnp.exp(v - vals[0]) for v in vals]
    inv = 1.0 / sum(ex)
    for k in range(TOP_K):
        gate_ref[k:k + 1, :] = ex[k] * inv

    chosen = functools.reduce(jnp.logical_or, sels)
    before = jnp.dot(chosen.astype(BF16), tri_ref[...], preferred_element_type=F32)
    run = run_ref[:, 0:1]
    rank = before + run
    for k in range(TOP_K):
        pos_ref[k:k + 1, :] = jnp.sum(jnp.where(sels[k], rank, 0.0), axis=0,
                                      keepdims=True).astype(I32)
    run_new = run + jnp.sum(chosen.astype(F32), axis=1, keepdims=True)
    run_ref[...] = jnp.broadcast_to(run_new, run_ref.shape)
    cnt_ref[...] = jnp.broadcast_to(run_new, cnt_ref.shape)


def _post_mixer(h, x2d, w_out, ln_g, ln_b, w_router, b_router):
    T, D = x2d.shape
    nch = D // (2 * SC_COLS)
    E = N_EXPERTS
    tm = TOKEN_TILE
    wo = w_out.astype(BF16)
    wrt = w_router.T
    wrh = wrt.astype(BF16)
    wr = jnp.stack([wrh, (wrt - wrh.astype(F32)).astype(BF16)])
    full = lambda a: pl.BlockSpec(a.shape, lambda i: (0,) * a.ndim)
    g, b, br = ln_g.reshape(1, D), ln_b.reshape(1, D), b_router.reshape(E, 1)
    pos_ids = jnp.arange(tm, dtype=I32)
    tri = (pos_ids[:, None] < pos_ids[None, :]).astype(BF16)
    rows = TOP_K
    return pl.pallas_call(
        _post_kernel,
        grid=(T // tm,),
        in_specs=[pl.BlockSpec((tm, h.shape[1]), lambda i: (i, 0)),
                  pl.BlockSpec((tm, D), lambda i: (i, 0)),
                  full(wo), full(g), full(b), full(wr), full(br), full(tri)],
        out_specs=[pl.BlockSpec((tm, D), lambda i: (i, 0)),
                   pl.BlockSpec((nch, tm, SC_COLS), lambda i: (0, i, 0)),
                   pl.BlockSpec((rows, tm), lambda i: (0, i)),
                   pl.BlockSpec((rows, tm), lambda i: (0, i)),
                   pl.BlockSpec((rows, tm), lambda i: (0, i)),
                   pl.BlockSpec((E, LANES), lambda i: (0, 0))],
        out_shape=[jax.ShapeDtypeStruct((T, D), F32),
                   jax.ShapeDtypeStruct((nch, T, SC_COLS), U32),
                   jax.ShapeDtypeStruct((rows, T), I32),
                   jax.ShapeDtypeStruct((rows, T), F32),
                   jax.ShapeDtypeStruct((rows, T), I32),
                   jax.ShapeDtypeStruct((E, LANES), F32)],
        scratch_shapes=[pltpu.VMEM((E, LANES), F32)],
        compiler_params=_params("arbitrary"),
        name="post_mixer_router",
    )(h, x2d, wo, g, b, wr, br, tri)


def _sc_mesh():
    return plsc.VectorSubcoreMesh(core_axis_name="core", subcore_axis_name="subcore")


def _sc_scatter_rows(xc, dest, n_rows):
    nch, T, C = xc.shape
    K = dest.shape[0]
    W = SC_WINDOW
    xs = xc.reshape(nch * T, C)
    nb = (nch * T) // W
    offs = (jnp.arange(nch, dtype=I32) * n_rows)[None, :, None]
    idx = (dest[:, None, :] + offs).reshape(K, nch * T)

    @functools.partial(pl.kernel, out_type=jax.ShapeDtypeStruct((nch * n_rows, C), xc.dtype),
                       mesh=_sc_mesh(), scratch_types=[], name="moe_dispatch_scatter")
    def scatter(x_hbm, i_hbm, o_hbm):
        def body(x_vmem, i_vmem):
            for k in range(K):
                pltpu.sync_copy(x_vmem, o_hbm.at[i_vmem.at[k]])

        pltpu.emit_pipeline(
            body,
            grid=(nb,),
            in_specs=[pl.BlockSpec((W, C), lambda g: (g, 0)),
                      pl.BlockSpec((K, W), lambda g: (0, g))],
            out_specs=[],
            core_axis_name=("core", "subcore"),
            dimension_semantics=(pltpu.PARALLEL,),
        )(x_hbm, i_hbm)

    return scatter(xs, idx).reshape(nch, n_rows, C)


def _sc_gather_rows(yc, dest):
    nch, n_rows, C = yc.shape
    K, T = dest.shape
    W = SC_WINDOW
    ys = yc.reshape(nch * n_rows, C)
    offs = (jnp.arange(nch, dtype=I32) * n_rows)[:, None, None]
    idx = (dest[None, :, :] + offs).reshape(1, nch * K * T)
    n_sub = nch * K * T

    @functools.partial(pl.kernel, out_type=jax.ShapeDtypeStruct((n_sub, C), yc.dtype),
                       mesh=_sc_mesh(), scratch_types=[], name="moe_combine_gather")
    def gather(t_hbm, i_hbm, o_hbm):
        def body(i_vmem, o_vmem):
            pltpu.sync_copy(t_hbm.at[i_vmem.at[0]], o_vmem)

        pltpu.emit_pipeline(
            body,
            grid=(n_sub // W,),
            in_specs=[pl.BlockSpec((1, W), lambda g: (0, g))],
            out_specs=[pl.BlockSpec((W, C), lambda g: (g, 0))],
            core_axis_name=("core", "subcore"),
            dimension_semantics=(pltpu.PARALLEL,),
        )(i_hbm, o_hbm)

    return gather(ys, idx).reshape(nch, K, T, C)


def _expert_kernel(te_ref, nu_ref, x_ref, wgu_ref, bg_ref, bl_ref, wd_ref, bd_ref, perm_ref,
                   y_ref, wg_s, wl_s, wd_s):
    i = pl.program_id(0)
    active = i < nu_ref[0]
    e = te_ref[i]
    changed = jnp.logical_or(i == 0, e != te_ref[jnp.maximum(i - 1, 0)])
    dot = functools.partial(jnp.dot, preferred_element_type=F32)

    @pl.when(jnp.logical_and(active, changed))
    def _():
        n_blocks = wgu_ref.shape[3] // MXU_DIM
        for blk in range(n_blocks):
            wb = wgu_ref[0, 0, :, blk * MXU_DIM:(blk + 1) * MXU_DIM].astype(BF16)
            wp = dot(wb, perm_ref[...]).astype(BF16)
            wg_s[:, blk * LANES:(blk + 1) * LANES] = wp[:, :LANES]
            wl_s[:, blk * LANES:(blk + 1) * LANES] = wp[:, LANES:]
        wd_s[...] = wd_ref[0, 0].astype(BF16)

    @pl.when(active)
    def _():
        xb = _load_packed(x_ref).astype(BF16)
        g = jnp.minimum(dot(xb, wg_s[...]) + bg_ref[0], SWIGLU_LIMIT)
        lin = jnp.clip(dot(xb, wl_s[...]) + bl_ref[0], -SWIGLU_LIMIT, SWIGLU_LIMIT)
        act = (lin + 1.0) * g * jax.nn.sigmoid(SWIGLU_ALPHA * g)
        _store_packed(y_ref, dot(act.astype(BF16), wd_s[...]) + bd_ref[0])


def _combine_kernel(y0_ref, y1_ref, y2_ref, y3_ref, gate_ref, x_ref, g_ref, b_ref, o_ref):
    gate = gate_ref[...]
    ff = (gate[:, 0:1] * _load_packed(y0_ref) + gate[:, 1:2] * _load_packed(y1_ref)
          + gate[:, 2:3] * _load_packed(y2_ref) + gate[:, 3:4] * _load_packed(y3_ref))
    o_ref[...] = _layer_norm(DEEPNORM_ALPHA * x_ref[...] + ff, g_ref[...], b_ref[...])


def _moe_dispatch(x1p, idx, pos, cnt):
    nch, T, C = x1p.shape
    E, K = N_EXPERTS, TOP_K
    te = EXPERT_TILE
    n_tiles = (T * K) // te + E
    n_rows = n_tiles * te

    counts = cnt[:, 0].astype(I32)
    padded = (counts + te - 1) // te * te
    pad_end = jnp.cumsum(padded)
    pad_start = pad_end - padded
    experts = jnp.arange(E, dtype=I32)
    start_of = jnp.sum(jnp.where(idx[None] == experts[:, None, None],
                                 pad_start[:, None, None], 0), axis=0)
    dest = start_of + pos
    tile_start = jnp.arange(n_tiles, dtype=I32) * te
    tile_expert = jnp.minimum(
        jnp.sum((pad_end[None, :] <= tile_start[:, None]).astype(I32), axis=1), E - 1)
    n_used = (pad_end[-1] // te).astype(I32).reshape(1)

    x_rows = _sc_scatter_rows(x1p, dest, n_rows)
    return x_rows, dest, tile_expert, n_used


def _moe_experts(x_rows, tile_expert, n_used, layer, w_gu, b_gu, w_down, b_down):
    nch, n_rows, C = x_rows.shape
    E = N_EXPERTS
    D, dff = w_down.shape[3], w_down.shape[2]
    te = EXPERT_TILE
    n_tiles = n_rows // te
    half = MXU_DIM // 2
    src = jnp.arange(MXU_DIM)
    perm = (src[:, None] == jnp.where(src < half, 2 * src, 2 * (src - half) + 1)[None, :])
    perm = perm.astype(BF16)
    bg = b_gu[:, 0::2].reshape(E, 1, dff)
    bl = b_gu[:, 1::2].reshape(E, 1, dff)
    bd = b_down.reshape(E, 1, D)

    def row_map(i, te_ref, nu_ref):
        return (0, jnp.minimum(i, nu_ref[0] - 1), 0)

    def exp_map(i, te_ref, nu_ref):
        return (te_ref[i], 0, 0)

    def stacked_map(i, te_ref, nu_ref):
        return (layer, te_ref[i], 0, 0)

    return pl.pallas_call(
        _expert_kernel,
        grid_spec=pltpu.PrefetchScalarGridSpec(
            num_scalar_prefetch=2,
            grid=(n_tiles,),
            in_specs=[pl.BlockSpec((nch, te, C), row_map),
                      pl.BlockSpec((1, 1, D, 2 * dff), stacked_map),
                      pl.BlockSpec((1, 1, dff), exp_map),
                      pl.BlockSpec((1, 1, dff), exp_map),
                      pl.BlockSpec((1, 1, dff, D), stacked_map),
                      pl.BlockSpec((1, 1, D), exp_map),
                      pl.BlockSpec((MXU_DIM, MXU_DIM), lambda i, a, b: (0, 0))],
            out_specs=pl.BlockSpec((nch, te, C), row_map),
            scratch_shapes=[pltpu.VMEM((D, dff), BF16), pltpu.VMEM((D, dff), BF16),
                            pltpu.VMEM((dff, D), BF16)],
        ),
        out_shape=jax.ShapeDtypeStruct((nch, n_rows, C), U32),
        compiler_params=_params("arbitrary"),
        name="moe_experts",
    )(tile_expert, n_used, x_rows, w_gu, bg, bl, w_down, bd, perm)


def _moe_combine(y_tok, gate, x1, ln_g, ln_b):
    nch, K, T, C = y_tok.shape
    D = x1.shape[1]
    tm = TOKEN_TILE
    gate_col = gate.T
    slot = lambda k: pl.BlockSpec((nch, None, tm, C), lambda i, k=k: (0, k, i, 0))
    vec = pl.BlockSpec((1, D), lambda i: (0, 0))
    return pl.pallas_call(
        _combine_kernel,
        grid=(T // tm,),
        in_specs=[slot(0), slot(1), slot(2), slot(3),
                  pl.BlockSpec((tm, K), lambda i: (i, 0)),
                  pl.BlockSpec((tm, D), lambda i: (i, 0)), vec, vec],
        out_specs=pl.BlockSpec((tm, D), lambda i: (i, 0)),
        out_shape=jax.ShapeDtypeStruct((T, D), F32),
        compiler_params=_params("parallel"),
        name="moe_combine_norm",
    )(y_tok, y_tok, y_tok, y_tok, gate_col, x1, ln_g.reshape(1, D), ln_b.reshape(1, D))


def kernel(x, positions, ln_gain, ln_bias, mlstm_w_in, mlstm_b_gates, mlstm_norm_gain,
           mlstm_w_out, mla_w_in, mla_q_norm, mla_kv_norm, mla_w_qb, mla_w_kvb, mla_w_out,
           moe_w_router, moe_b_router, moe_w_gate_up, moe_b_gate_up, moe_w_down, moe_b_down):
    B, S, D = x.shape
    T = B * S
    x2d = x.reshape(T, D)

    inv_freq = ROPE_THETA ** (-jnp.arange(0, MLA_ROPE, 2, dtype=F32) / MLA_ROPE)
    ang = positions.astype(F32).reshape(T, 1) * inv_freq
    reps = LANES // ang.shape[1]
    cos_t = jnp.tile(jnp.cos(ang), (1, reps))
    sin_t = jnp.tile(jnp.sin(ang), (1, reps))

    for layer in range(DEPTH):
        j = layer // 2
        if layer % 2 == 0:
            h = _mlstm_mixer(x2d, B, mlstm_w_in[j], mlstm_b_gates[j], mlstm_norm_gain[j])
            w_out = mlstm_w_out[j]
        else:
            h = _mla_mixer(x2d, B, cos_t, sin_t, mla_w_in[j], mla_q_norm[j], mla_kv_norm[j],
                           mla_w_qb[j], mla_w_kvb[j])
            w_out = mla_w_out[j]
        x1, x1p, idx, gate, pos, cnt = _post_mixer(h, x2d, w_out, ln_gain[layer, 0], ln_bias[layer, 0],
                                                   moe_w_router[layer], moe_b_router[layer])
        x_rows, dest, tile_expert, n_used = _moe_dispatch(x1p, idx, pos, cnt)
        y_rows = _moe_experts(x_rows, tile_expert, n_used, layer, moe_w_gate_up, moe_b_gate_up[layer],
                              moe_w_down, moe_b_down[layer])
        y_tok = _sc_gather_rows(y_rows, dest)
        x2d = _moe_combine(y_tok, gate, x1, ln_gain[layer, 1], ln_bias[layer, 1])
    return x2d.reshape(B, S, D)
```

```python
import functools

import jax
import jax.numpy as jnp
from jax import lax
from jax.experimental import pallas as pl
from jax.experimental.pallas import tpu as pltpu
from jax.experimental.pallas import tpu_sc as plsc

F32 = jnp.float32
BF16 = jnp.bfloat16
I32 = jnp.int32
U32 = jnp.uint32

DEPTH = 4
MLSTM_HEADS = 8
MLSTM_DQK = 64
MLSTM_DV = 128
MLA_HEADS = 8
MLA_Q_LORA = 384
MLA_KV_LORA = 256
MLA_NOPE = 128
MLA_ROPE = 64
MLA_V = 128
ROPE_THETA = 10000.0
N_EXPERTS = 32
TOP_K = 4
SWIGLU_LIMIT = 7.0
SWIGLU_ALPHA = 1.702
DEEPNORM_ALPHA = (2.0 * DEPTH) ** 0.25
LN_EPS = 1e-5
LOG2_E = 1.4426950408889634
RMS_EPS = 1e-6

LANES = 128
BF16_SUBLANES = 16
MXU_DIM = 256
VMEM_LIMIT_BYTES = 56 * 1024 * 1024

TOKEN_TILE = 1024
MLSTM_SEQ_BLOCK = 1024
MLSTM_CHUNK = 256
ATTN_TILE = 512
ATTN_HEADS_PER_STEP = 2
ATTN_QUERY_TILES_PER_STEP = 2
ATTN_SCORE_LOOKAHEAD = 1
EXPERT_TILE = 512
SC_WINDOW = 128
SC_COLS = 256

_NT = (((1,), (1,)), ((), ()))
HALF_WORD_BITS = 16
HIGH_HALF_MASK = 0xFFFF0000


def _params(*sem):
    return pltpu.CompilerParams(dimension_semantics=sem, vmem_limit_bytes=VMEM_LIMIT_BYTES)


def _layer_norm(z, g, b):
    mu = jnp.mean(z, axis=-1, keepdims=True)
    zc = z - mu
    var = jnp.mean(zc * zc, axis=-1, keepdims=True)
    return zc * lax.rsqrt(var + LN_EPS) * g + b


def _rms_norm(z, g):
    return z * lax.rsqrt(jnp.mean(z * z, axis=-1, keepdims=True) + RMS_EPS) * g


def _store_packed(ref, a):
    half = a.shape[1] // 2
    rounded = lambda v: lax.bitcast_convert_type(v.astype(BF16).astype(F32), U32)
    words = ((rounded(a[:, :half]) >> HALF_WORD_BITS)
             | (rounded(a[:, half:]) & jnp.uint32(HIGH_HALF_MASK)))
    for c in range(ref.shape[0]):
        ref[c] = words[:, c * SC_COLS:(c + 1) * SC_COLS]


def _load_packed(ref):
    chunks = [ref[c] for c in range(ref.shape[0])]
    lo = [lax.bitcast_convert_type(w << HALF_WORD_BITS, F32) for w in chunks]
    hi = [lax.bitcast_convert_type(w & jnp.uint32(HIGH_HALF_MASK), F32) for w in chunks]
    return jnp.concatenate(lo + hi, axis=1)


def _split3(a):
    hi = a.astype(BF16)
    r = a - hi.astype(F32)
    mid = r.astype(BF16)
    lo = (r - mid.astype(F32)).astype(BF16)
    return hi, mid, lo


def _mlstm_inproj_kernel(x_ref, wq_ref, wk_ref, wvt_ref, wot_ref, wg_ref, bg_ref,
                         q_ref, k_ref, vt_ref, ot_ref, gc_ref, gs_ref, gr_ref):
    H = MLSTM_HEADS
    xb = x_ref[...].astype(BF16)
    dot = functools.partial(jnp.dot, preferred_element_type=F32)
    ntdot = lambda a, b: lax.dot_general(a, b, _NT, preferred_element_type=F32)
    q_ref[...] = (dot(xb, wq_ref[...]) * (MLSTM_DQK ** -0.5)).astype(BF16)
    k_ref[...] = dot(xb, wk_ref[...]).astype(BF16)
    vt_ref[...] = ntdot(wvt_ref[...], xb).astype(BF16)
    ot_ref[...] = ntdot(wot_ref[...], xb)
    z = dot(xb, wg_ref[...]) + bg_ref[...]
    lane = lax.broadcasted_iota(I32, z.shape, 1)
    log_sig = jnp.minimum(z, 0.0) - jnp.log1p(jnp.exp(-jnp.abs(z)))
    g = jnp.where(lane < H, z, log_sig) * LOG2_E
    gc_ref[...] = g
    gs_ref[...] = pltpu.roll(g, LANES - H, 1)
    gr_ref[...] = g.T[:2 * H, :]


def _mlstm_cell_kernel(q_ref, k_ref, vt_ref, ot_ref, gc_ref, gs_ref, gr_ref, ng_ref, out_ref,
                       c_ref, m_ref, *, chunk, n_chunks):
    H, dk, dv = MLSTM_HEADS, MLSTM_DQK, MLSTM_DV
    reps = chunk // LANES
    aug = c_ref.shape[1] - dv

    @pl.when(pl.program_id(1) == 0)
    def _():
        c_ref[...] = jnp.zeros_like(c_ref)
        m_ref[...] = jnp.zeros_like(m_ref)

    s_idx = lax.broadcasted_iota(I32, (chunk, chunk), 0)
    j_idx = lax.broadcasted_iota(I32, (chunk, chunk), 1)
    visible = s_idx <= j_idx
    tri_upper = visible.astype(BF16)
    tri_lower = (j_idx <= s_idx).astype(BF16)
    ones_rows = (lax.broadcasted_iota(I32, (aug, chunk), 0) == 0).astype(BF16)
    dot = functools.partial(jnp.dot, preferred_element_type=F32)
    ntdot = lambda a, b: lax.dot_general(a, b, _NT, preferred_element_type=F32)

    def chunk_body(c, carry):
        rows = pl.ds(pl.multiple_of(c * chunk, chunk), chunk)
        gr = gr_ref[:, rows]
        b_cols = sum(dot(tri_lower, p) for p in _split3(gs_ref[rows, :]))
        b_rows = sum(dot(p, tri_upper) for p in _split3(gr))
        a_cols = gc_ref[rows, :] - b_cols
        hs = range(H)
        ig_r = [gr[h:h + 1, :] for h in hs]
        b_r = [b_rows[H + h:H + h + 1, :] for h in hs]
        b_last = [b[:, chunk - 1:chunk] for b in b_r]
        m_row = [jnp.concatenate([m_ref[h:h + 1, :]] * reps, axis=1) for h in hs]
        m_old = [m[:, 0:1] for m in m_row]
        qh = [q_ref[rows, h * dk:(h + 1) * dk] for h in hs]
        kh = [k_ref[rows, h * dk:(h + 1) * dk] for h in hs]
        vaug = [jnp.concatenate([vt_ref[h * dv:(h + 1) * dv, rows], ones_rows], axis=0)
                for h in hs]
        ct = [c_ref[h] for h in hs]

        qk = [ntdot(kh[h], qh[h]) for h in hs]
        qc = [ntdot(ct[h].astype(BF16), qh[h]) for h in hs]

        m_new = [jnp.maximum(b_last[h] + m_old[h],
                             jnp.max(b_last[h] - b_r[h] + ig_r[h], axis=1, keepdims=True)) for h in hs]
        for h in hs:
            w_row = jnp.exp2(b_last[h] - b_r[h] + ig_r[h] - m_new[h])
            vw = vaug[h] * w_row.astype(BF16)
            c_ref[h] = jnp.exp2(b_last[h] + m_old[h] - m_new[h]) * ct[h] + dot(vw, kh[h])
            m_ref[h:h + 1, :] = jnp.broadcast_to(m_new[h], (1, LANES))

        a_mat = [jnp.where(visible, a_cols[:, h:h + 1], -jnp.inf) for h in hs]
        g = [jnp.maximum(m_row[h], jnp.max(a_mat[h], axis=0, keepdims=True)) for h in hs]
        s = [qk[h] * jnp.exp2(a_mat[h] - g[h]) for h in hs]
        nd = [dot(vaug[h], s[h].astype(BF16)) + jnp.exp2(m_row[h] - g[h]) * qc[h]
              for h in hs]
        for h in hs:
            num, den = nd[h][:dv], nd[h][dv:dv + 1]
            r = 1.0 / jnp.maximum(jnp.abs(den), jnp.exp2(-(b_r[h] + g[h])))
            scale = r * lax.rsqrt(r * r * jnp.mean(num * num, axis=0, keepdims=True) + RMS_EPS)
            gain = jnp.concatenate([ng_ref[h * dv:(h + 1) * dv, :]] * reps, axis=1)
            out_t = jax.nn.sigmoid(ot_ref[h * dv:(h + 1) * dv, rows]) * (num * scale * gain)
            out_ref[rows, h * dv:(h + 1) * dv] = out_t.T.astype(out_ref.dtype)
        return carry

    lax.fori_loop(0, n_chunks, chunk_body, 0)


def _mlstm_mixer(x2d, batch, w_in, b_gates, norm_gain):
    T, D = x2d.shape
    H, dk, dv = MLSTM_HEADS, MLSTM_DQK, MLSTM_DV
    seq = T // batch
    tm = TOKEN_TILE
    cq, ck, cv, co = H * dk, 2 * H * dk, 2 * H * dk + H * dv, 2 * H * dk + 2 * H * dv
    wq = w_in[:, :cq].astype(BF16)
    wk = w_in[:, cq:ck].astype(BF16)
    wvt = w_in[:, ck:cv].T.astype(BF16)
    wot = w_in[:, cv:co].T.astype(BF16)
    wg = jnp.pad(w_in[:, co:], ((0, 0), (0, LANES - 2 * H))).astype(BF16)
    bg = jnp.pad(b_gates, (0, LANES - 2 * H)).reshape(1, LANES)

    full = lambda a: pl.BlockSpec(a.shape, lambda i: (0,) * a.ndim)
    tok = lambda w: pl.BlockSpec((tm, w), lambda i: (i, 0))
    tok_t = lambda h: pl.BlockSpec((h, tm), lambda i: (0, i))
    q, k, vt, ot, gc, gs, gr = pl.pallas_call(
        _mlstm_inproj_kernel,
        grid=(T // tm,),
        in_specs=[tok(D), full(wq), full(wk), full(wvt), full(wot), full(wg), full(bg)],
        out_specs=[tok(H * dk), tok(H * dk), tok_t(H * dv), tok_t(H * dv),
                   tok(LANES), tok(LANES), tok_t(2 * H)],
        out_shape=[jax.ShapeDtypeStruct((T, H * dk), BF16),
                   jax.ShapeDtypeStruct((T, H * dk), BF16),
                   jax.ShapeDtypeStruct((H * dv, T), BF16),
                   jax.ShapeDtypeStruct((H * dv, T), F32),
                   jax.ShapeDtypeStruct((T, LANES), F32),
                   jax.ShapeDtypeStruct((T, LANES), F32),
                   jax.ShapeDtypeStruct((2 * H, T), F32)],
        compiler_params=_params("parallel"),
        name="mlstm_inproj",
    )(x2d, wq, wk, wvt, wot, wg, bg)

    ts = min(MLSTM_SEQ_BLOCK, seq)
    chunk = min(MLSTM_CHUNK, ts)
    nsb = seq // ts
    ng = jnp.broadcast_to(norm_gain.reshape(H * dv, 1), (H * dv, LANES))
    seq_rows = lambda w: pl.BlockSpec((ts, w), lambda b, s: (b * nsb + s, 0))
    seq_lanes = lambda h: pl.BlockSpec((h, ts), lambda b, s: (0, b * nsb + s))
    return pl.pallas_call(
        functools.partial(_mlstm_cell_kernel, chunk=chunk, n_chunks=ts // chunk),
        grid=(batch, nsb),
        in_specs=[seq_rows(H * dk), seq_rows(H * dk), seq_lanes(H * dv), seq_lanes(H * dv),
                  seq_rows(LANES), seq_rows(LANES), seq_lanes(2 * H),
                  pl.BlockSpec((H * dv, LANES), lambda b, s: (0, 0))],
        out_specs=seq_rows(H * dv),
        out_shape=jax.ShapeDtypeStruct((T, H * dv), BF16),
        scratch_shapes=[pltpu.VMEM((H, dv + BF16_SUBLANES, dk), F32), pltpu.VMEM((H, LANES), F32)],
        compiler_params=_params("parallel", "arbitrary"),
        name="mlstm_cell",
    )(q, k, vt, ot, gc, gs, gr, ng)


def _mla_proj_kernel(x_ref, cos_ref, sin_ref, wcq_ref, wckv_ref, wkr_ref, qn_ref, kvn_ref,
                     wq_ref, wqr_ref, wkn_ref, wvt_ref, q_ref, k_ref, vt_ref):
    H, dn = MLA_HEADS, MLA_NOPE
    dot = functools.partial(jnp.dot, preferred_element_type=F32)
    xb = x_ref[...].astype(BF16)
    cos, sin = cos_ref[...], sin_ref[...]
    c_q = _rms_norm(dot(xb, wcq_ref[...]), qn_ref[...]).astype(BF16)
    c_kv = _rms_norm(dot(xb, wckv_ref[...]), kvn_ref[...]).astype(BF16)
    kr2 = dot(xb, wkr_ref[...])
    kr = (kr2[:, :LANES] * cos + kr2[:, LANES:] * sin).astype(BF16)
    scale = (MLA_NOPE + MLA_ROPE) ** -0.5 * LOG2_E
    qa = dot(c_q, wq_ref[...])
    qr = dot(c_q, wqr_ref[...])
    kn = dot(c_kv, wkn_ref[...])
    for h in range(H):
        base = h * 2 * LANES
        q_ref[h, :, :dn] = (qa[:, base:base + dn] * scale).astype(BF16)
        rope = qa[:, base + dn:base + 2 * LANES] * cos + qr[:, h * LANES:(h + 1) * LANES] * sin
        q_ref[h, :, dn:] = (rope * scale).astype(BF16)
        k_ref[h, :, :dn] = kn[:, h * dn:(h + 1) * dn].astype(BF16)
        k_ref[h, :, dn:] = kr
    vt_ref[...] = lax.dot_general(wvt_ref[...], c_kv, _NT, preferred_element_type=F32).astype(BF16)


def _attn_kernel(q_ref, k_ref, vt_ref, o_ref, bad_ref, m_ref, mu_ref, acc_ref,
                 *, tile, heads, qsub, lagged):
    assert qsub == 2, "the key-tile bookkeeping below is written for two query tiles per step"
    dv = MLA_V
    qi = pl.program_id(2)
    m_ref[...] = jnp.full_like(m_ref, -jnp.inf)
    mu_ref[...] = jnp.full_like(mu_ref, -jnp.inf)
    acc_ref[...] = jnp.zeros_like(acc_ref)
    ones_rows = jnp.ones((acc_ref.shape[1] - dv, tile), BF16)

    def emit(tiles):
        units = [(j, ds, ex, g, u) for j, ds, ex in tiles for g in range(heads)
                 for u in range(qsub) if ds is None or u >= ds]
        keys_of = lambda j: pl.ds(pl.multiple_of(j * tile, tile), tile)
        sts = {}

        def score(i):
            j, ds, ex, g, u = units[i]
            st = lax.dot_general(k_ref[g, keys_of(j), :], q_ref[g, u * tile:(u + 1) * tile, :], _NT,
                                 preferred_element_type=F32)
            if u == ds:
                kpos = lax.broadcasted_iota(I32, st.shape, 0)
                qpos = lax.broadcasted_iota(I32, st.shape, 1)
                st = jnp.where(kpos <= qpos, st, -jnp.inf)
            sts[i] = st

        def softmax_value(i):
            j, ds, ex, g, u = units[i]
            c = g * qsub + u
            st = sts.pop(i)
            tile_max = jnp.max(st, axis=0, keepdims=True)
            seen = m_ref[c]
            if ex == "two_pass":
                stab = jnp.maximum(seen, tile_max)
            elif ex == "key0":
                stab = st[0:1, :]
            else:
                stab = seen
            p = jnp.exp2(st - stab).astype(BF16)
            vt = jnp.concatenate([vt_ref[g * dv:(g + 1) * dv, keys_of(j)], ones_rows], axis=0)
            acc_ref[c] = jnp.exp2(mu_ref[c] - stab) * acc_ref[c] + jnp.dot(
                vt, p, preferred_element_type=F32)
            mu_ref[c] = stab
            m_ref[c] = jnp.maximum(seen, tile_max)

        ahead = ATTN_SCORE_LOOKAHEAD
        for i in range(min(ahead, len(units))):
            score(i)
        for i in range(len(units)):
            if i + ahead < len(units):
                score(i + ahead)
            softmax_value(i)

    ex = "earlier" if lagged else "two_pass"
    ex0 = "key0" if lagged else "two_pass"

    @pl.when(qi > 0)
    def _():
        emit([(0, None, ex0), (1, None, ex)])

    n_pairs = jnp.maximum(qi - 1, 0)

    def body(i, carry):
        emit([(2 + 2 * qsub * i + d, None, ex) for d in range(2 * qsub)])
        return carry

    lax.fori_loop(0, n_pairs // 2, body, 0)

    @pl.when(n_pairs % 2 == 1)
    def _():
        emit([(2 * qi - 2, None, ex), (2 * qi - 1, None, ex)])

    @pl.when(qi > 0)
    def _():
        emit([(qsub * qi + u, u, ex) for u in range(qsub)])

    @pl.when(qi == 0)
    def _():
        emit([(0, 0, ex0), (1, 1, ex)])

    for g in range(heads):
        for u in range(qsub):
            c = g * qsub + u
            acc = acc_ref[c]
            o_ref[u * tile:(u + 1) * tile, g * dv:(g + 1) * dv] = (
                acc[:dv] * (1.0 / acc[dv:dv + 1])).T.astype(o_ref.dtype)
            bad_ref[0, c] = jnp.max(jnp.where(acc - acc == 0.0, 0.0, 1.0), axis=0, keepdims=True)


def _mla_mixer(x2d, batch, cos_t, sin_t, w_in, q_norm, kv_norm, w_qb, w_kvb):
    T, D = x2d.shape
    H, dn, dr, dv = MLA_HEADS, MLA_NOPE, MLA_ROPE, MLA_V
    seq = T // batch
    tm = TOKEN_TILE
    ql, kl = MLA_Q_LORA, MLA_KV_LORA
    half = dr // 2

    def rot(w):
        return jnp.concatenate([-w[..., half:], w[..., :half]], axis=-1)

    wcq = w_in[:, :ql].astype(BF16)
    wckv = w_in[:, ql:ql + kl].astype(BF16)
    wr = w_in[:, ql + kl:]
    zr = jnp.zeros((D, LANES - dr), F32)
    wkr = jnp.concatenate([wr, zr, rot(wr), zr], axis=1).astype(BF16)
    wq3 = w_qb.reshape(ql, H, dn + dr)
    zq = jnp.zeros((ql, H, LANES - dr), F32)
    wq = jnp.concatenate([wq3, zq], axis=2).reshape(ql, H * 2 * LANES).astype(BF16)
    wqr = jnp.concatenate([rot(wq3[:, :, dn:]), zq], axis=2).reshape(ql, H * LANES).astype(BF16)
    wkv3 = w_kvb.reshape(kl, H, dn + dv)
    wkn = wkv3[:, :, :dn].reshape(kl, H * dn).astype(BF16)
    wvt = wkv3[:, :, dn:].reshape(kl, H * dv).T.astype(BF16)
    qn = q_norm.reshape(1, ql)
    kvn = kv_norm.reshape(1, kl)

    full = lambda a: pl.BlockSpec(a.shape, lambda i: (0,) * a.ndim)
    q, k, vt = pl.pallas_call(
        _mla_proj_kernel,
        grid=(T // tm,),
        in_specs=[pl.BlockSpec((tm, D), lambda i: (i, 0)),
                  pl.BlockSpec((tm, LANES), lambda i: (i, 0)),
                  pl.BlockSpec((tm, LANES), lambda i: (i, 0)),
                  full(wcq), full(wckv), full(wkr), full(qn), full(kvn),
                  full(wq), full(wqr), full(wkn), full(wvt)],
        out_specs=[pl.BlockSpec((H, tm, 2 * LANES), lambda i: (0, i, 0)),
                   pl.BlockSpec((H, tm, 2 * LANES), lambda i: (0, i, 0)),
                   pl.BlockSpec((H * dv, tm), lambda i: (0, i))],
        out_shape=[jax.ShapeDtypeStruct((H, T, 2 * LANES), BF16),
                   jax.ShapeDtypeStruct((H, T, 2 * LANES), BF16),
                   jax.ShapeDtypeStruct((H * dv, T), BF16)],
        compiler_params=_params("parallel"),
        name="mla_proj",
    )(x2d, cos_t, sin_t, wcq, wckv, wkr, qn, kvn, wq, wqr, wkn, wvt)

    tile = min(ATTN_TILE, seq)
    hp = ATTN_HEADS_PER_STEP
    qsub = min(ATTN_QUERY_TILES_PER_STEP, seq // tile)
    tq = tile * qsub
    nq = seq // tq
    nhp = H // hp
    nc = hp * qsub

    def attention(lagged):
        return pl.pallas_call(
            functools.partial(_attn_kernel, tile=tile, heads=hp, qsub=qsub, lagged=lagged),
            grid=(batch, nhp, nq),
            in_specs=[pl.BlockSpec((hp, tq, 2 * LANES), lambda b, h, i: (h, b * nq + i, 0)),
                      pl.BlockSpec((hp, seq, 2 * LANES), lambda b, h, i: (h, b, 0)),
                      pl.BlockSpec((hp * dv, seq), lambda b, h, i: (h, b))],
            out_specs=[pl.BlockSpec((tq, hp * dv), lambda b, h, i: (b * nq + i, h)),
                       pl.BlockSpec((1, nc, 1, tile), lambda b, h, i: ((b * nhp + h) * nq + i, 0, 0, 0))],
            out_shape=[jax.ShapeDtypeStruct((T, H * dv), BF16),
                       jax.ShapeDtypeStruct((batch * nhp * nq, nc, 1, tile), F32)],
            scratch_shapes=[pltpu.VMEM((nc, 1, tile), F32), pltpu.VMEM((nc, 1, tile), F32),
                            pltpu.VMEM((nc, dv + BF16_SUBLANES, tile), F32)],
            compiler_params=_params("parallel", "parallel", "arbitrary"),
            name="mla_attention" if lagged else "mla_attention_exact",
        )(q, k, vt)

    out, not_finite = attention(lagged=True)
    return lax.cond(jnp.any(not_finite > 0.0), lambda: attention(lagged=False)[0], lambda: out)


def _post_kernel(h_ref, x_ref, w_ref, g_ref, b_ref, wr_ref, br_ref, tri_ref,
                 xo_ref, xp_ref, idx_ref, gate_ref, pos_ref, cnt_ref, run_ref):
    E = N_EXPERTS
    tm = x_ref.shape[0]

    @pl.when(pl.program_id(0) == 0)
    def _():
        run_ref[...] = jnp.zeros_like(run_ref)

    mix = jnp.dot(h_ref[...], w_ref[...], preferred_element_type=F32)
    x1 = _layer_norm(DEEPNORM_ALPHA * x_ref[...] + mix, g_ref[...], b_ref[...])
    xo_ref[...] = x1
    _store_packed(xp_ref, x1)

    xh = x1.astype(BF16)
    xl = (x1 - xh.astype(F32)).astype(BF16)
    wh, wl = wr_ref[0], wr_ref[1]
    ntdot = lambda a, b: lax.dot_general(a, b, _NT, preferred_element_type=F32)
    rest = ntdot(wh, xh) + ntdot(wh, xl) + ntdot(wl, xh) + br_ref[...]

    e_iota = lax.broadcasted_iota(I32, (E, tm), 0)
    vals, sels = [], []
    for k in range(TOP_K):
        v = jnp.max(rest, axis=0, keepdims=True)
        ik = jnp.min(jnp.where(rest == v, e_iota, E), axis=0, keepdims=True)
        sel = e_iota == ik
        rest = jnp.where(sel, -jnp.inf, rest)
        vals.append(v)
        sels.append(sel)
        idx_ref[k:k + 1, :] = ik
    ex = [jnp.exp(v - vals[0]) for v in vals]
    inv = 1.0 / sum(ex)
    for k in range(TOP_K):
        gate_ref[k:k + 1, :] = ex[k] * inv

    chosen = functools.reduce(jnp.logical_or, sels)
    before = jnp.dot(chosen.astype(BF16), tri_ref[...], preferred_element_type=F32)
    run = run_ref[:, 0:1]
    rank = before + run
    for k in range(TOP_K):
        pos_ref[k:k + 1, :] = jnp.sum(jnp.where(sels[k], rank, 0.0), axis=0,
                                      keepdims=True).astype(I32)
    run_new = run + jnp.sum(chosen.astype(F32), axis=1, keepdims=True)
    run_ref[...] = jnp.broadcast_to(run_new, run_ref.shape)
    cnt_ref[...] = jnp.broadcast_to(run_new, cnt_ref.shape)


def _post_mixer(h, x2d, w_out, ln_g, ln_b, w_router, b_router):
    T, D = x2d.shape
    nch = D // (2 * SC_COLS)
    E = N_EXPERTS
    tm = TOKEN_TILE
    wo = w_out.astype(BF16)
    wrt = w_router.T
    wrh = wrt.astype(BF16)
    wr = jnp.stack([wrh, (wrt - wrh.astype(F32)).astype(BF16)])
    full = lambda a: pl.BlockSpec(a.shape, lambda i: (0,) * a.ndim)
    g, b, br = ln_g.reshape(1, D), ln_b.reshape(1, D), b_router.reshape(E, 1)
    pos_ids = jnp.arange(tm, dtype=I32)
    tri = (pos_ids[:, None] < pos_ids[None, :]).astype(BF16)
    rows = TOP_K
    return pl.pallas_call(
        _post_kernel,
        grid=(T // tm,),
        in_specs=[pl.BlockSpec((tm, h.shape[1]), lambda i: (i, 0)),
                  pl.BlockSpec((tm, D), lambda i: (i, 0)),
                  full(wo), full(g), full(b), full(wr), full(br), full(tri)],
        out_specs=[pl.BlockSpec((tm, D), lambda i: (i, 0)),
                   pl.BlockSpec((nch, tm, SC_COLS), lambda i: (0, i, 0)),
                   pl.BlockSpec((rows, tm), lambda i: (0, i)),
                   pl.BlockSpec((rows, tm), lambda i: (0, i)),
                   pl.BlockSpec((rows, tm), lambda i: (0, i)),
                   pl.BlockSpec((E, LANES), lambda i: (0, 0))],
        out_shape=[jax.ShapeDtypeStruct((T, D), F32),
                   jax.ShapeDtypeStruct((nch, T, SC_COLS), U32),
                   jax.ShapeDtypeStruct((rows, T), I32),
                   jax.ShapeDtypeStruct((rows, T), F32),
                   jax.ShapeDtypeStruct((rows, T), I32),
                   jax.ShapeDtypeStruct((E, LANES), F32)],
        scratch_shapes=[pltpu.VMEM((E, LANES), F32)],
        compiler_params=_params("arbitrary"),
        name="post_mixer_router",
    )(h, x2d, wo, g, b, wr, br, tri)


def _sc_mesh():
    return plsc.VectorSubcoreMesh(core_axis_name="core", subcore_axis_name="subcore")


def _sc_scatter_rows(xc, dest, n_rows):
    nch, T, C = xc.shape
    K = dest.shape[0]
    W = SC_WINDOW
    xs = xc.reshape(nch * T, C)
    nb = (nch * T) // W
    offs = (jnp.arange(nch, dtype=I32) * n_rows)[None, :, None]
    idx = (dest[:, None, :] + offs).reshape(K, nch * T)

    @functools.partial(pl.kernel, out_type=jax.ShapeDtypeStruct((nch * n_rows, C), xc.dtype),
                       mesh=_sc_mesh(), scratch_types=[], name="moe_dispatch_scatter")
    def scatter(x_hbm, i_hbm, o_hbm):
        def body(x_vmem, i_vmem):
            for k in range(K):
                pltpu.sync_copy(x_vmem, o_hbm.at[i_vmem.at[k]])

        pltpu.emit_pipeline(
            body,
            grid=(nb,),
            in_specs=[pl.BlockSpec((W, C), lambda g: (g, 0)),
                      pl.BlockSpec((K, W), lambda g: (0, g))],
            out_specs=[],
            core_axis_name=("core", "subcore"),
            dimension_semantics=(pltpu.PARALLEL,),
        )(x_hbm, i_hbm)

    return scatter(xs, idx).reshape(nch, n_rows, C)


def _sc_gather_rows(yc, dest):
    nch, n_rows, C = yc.shape
    K, T = dest.shape
    W = SC_WINDOW
    ys = yc.reshape(nch * n_rows, C)
    offs = (jnp.arange(nch, dtype=I32) * n_rows)[:, None, None]
    idx = (dest[None, :, :] + offs).reshape(1, nch * K * T)
    n_sub = nch * K * T

    @functools.partial(pl.kernel, out_type=jax.ShapeDtypeStruct((n_sub, C), yc.dtype),
                       mesh=_sc_mesh(), scratch_types=[], name="moe_combine_gather")
    def gather(t_hbm, i_hbm, o_hbm):
        def body(i_vmem, o_vmem):
            pltpu.sync_copy(t_hbm.at[i_vmem.at[0]], o_vmem)

        pltpu.emit_pipeline(
            body,
            grid=(n_sub // W,),
            in_specs=[pl.BlockSpec((1, W), lambda g: (0, g))],
            out_specs=[pl.BlockSpec((W, C), lambda g: (g, 0))],
            core_axis_name=("core", "subcore"),
            dimension_semantics=(pltpu.PARALLEL,),
        )(i_hbm, o_hbm)

    return gather(ys, idx).reshape(nch, K, T, C)


def _expert_kernel(te_ref, nu_ref, x_ref, wgu_ref, bg_ref, bl_ref, wd_ref, bd_ref, perm_ref,
                   y_ref, wg_s, wl_s, wd_s):
    i = pl.program_id(0)
    active = i < nu_ref[0]
    e = te_ref[i]
    changed = jnp.logical_or(i == 0, e != te_ref[jnp.maximum(i - 1, 0)])
    dot = functools.partial(jnp.dot, preferred_element_type=F32)

    @pl.when(jnp.logical_and(active, changed))
    def _():
        n_blocks = wgu_ref.shape[3] // MXU_DIM
        for blk in range(n_blocks):
            wb = wgu_ref[0, 0, :, blk * MXU_DIM:(blk + 1) * MXU_DIM].astype(BF16)
            wp = dot(wb, perm_ref[...]).astype(BF16)
            wg_s[:, blk * LANES:(blk + 1) * LANES] = wp[:, :LANES]
            wl_s[:, blk * LANES:(blk + 1) * LANES] = wp[:, LANES:]
        wd_s[...] = wd_ref[0, 0].astype(BF16)

    @pl.when(active)
    def _():
        xb = _load_packed(x_ref).astype(BF16)
        g = jnp.minimum(dot(xb, wg_s[...]) + bg_ref[0], SWIGLU_LIMIT)
        lin = jnp.clip(dot(xb, wl_s[...]) + bl_ref[0], -SWIGLU_LIMIT, SWIGLU_LIMIT)
        act = (lin + 1.0) * g * jax.nn.sigmoid(SWIGLU_ALPHA * g)
        _store_packed(y_ref, dot(act.astype(BF16), wd_s[...]) + bd_ref[0])


def _combine_kernel(y0_ref, y1_ref, y2_ref, y3_ref, gate_ref, x_ref, g_ref, b_ref, o_ref):
    gate = gate_ref[...]
    ff = (gate[:, 0:1] * _load_packed(y0_ref) + gate[:, 1:2] * _load_packed(y1_ref)
          + gate[:, 2:3] * _load_packed(y2_ref) + gate[:, 3:4] * _load_packed(y3_ref))
    o_ref[...] = _layer_norm(DEEPNORM_ALPHA * x_ref[...] + ff, g_ref[...], b_ref[...])


def _moe_dispatch(x1p, idx, pos, cnt):
    nch, T, C = x1p.shape
    E, K = N_EXPERTS, TOP_K
    te = EXPERT_TILE
    n_tiles = (T * K) // te + E
    n_rows = n_tiles * te

    counts = cnt[:, 0].astype(I32)
    padded = (counts + te - 1) // te * te
    pad_end = jnp.cumsum(padded)
    pad_start = pad_end - padded
    experts = jnp.arange(E, dtype=I32)
    start_of = jnp.sum(jnp.where(idx[None] == experts[:, None, None],
                                 pad_start[:, None, None], 0), axis=0)
    dest = start_of + pos
    tile_start = jnp.arange(n_tiles, dtype=I32) * te
    tile_expert = jnp.minimum(
        jnp.sum((pad_end[None, :] <= tile_start[:, None]).astype(I32), axis=1), E - 1)
    n_used = (pad_end[-1] // te).astype(I32).reshape(1)

    x_rows = _sc_scatter_rows(x1p, dest, n_rows)
    return x_rows, dest, tile_expert, n_used


def _moe_experts(x_rows, tile_expert, n_used, layer, w_gu, b_gu, w_down, b_down):
    nch, n_rows, C = x_rows.shape
    E = N_EXPERTS
    D, dff = w_down.shape[3], w_down.shape[2]
    te = EXPERT_TILE
    n_tiles = n_rows // te
    half = MXU_DIM // 2
    src = jnp.arange(MXU_DIM)
    perm = (src[:, None] == jnp.where(src < half, 2 * src, 2 * (src - half) + 1)[None, :])
    perm = perm.astype(BF16)
    bg = b_gu[:, 0::2].reshape(E, 1, dff)
    bl = b_gu[:, 1::2].reshape(E, 1, dff)
    bd = b_down.reshape(E, 1, D)

    def row_map(i, te_ref, nu_ref):
        return (0, jnp.minimum(i, nu_ref[0] - 1), 0)

    def exp_map(i, te_ref, nu_ref):
        return (te_ref[i], 0, 0)

    def stacked_map(i, te_ref, nu_ref):
        return (layer, te_ref[i], 0, 0)

    return pl.pallas_call(
        _expert_kernel,
        grid_spec=pltpu.PrefetchScalarGridSpec(
            num_scalar_prefetch=2,
            grid=(n_tiles,),
            in_specs=[pl.BlockSpec((nch, te, C), row_map),
                      pl.BlockSpec((1, 1, D, 2 * dff), stacked_map),
                      pl.BlockSpec((1, 1, dff), exp_map),
                      pl.BlockSpec((1, 1, dff), exp_map),
                      pl.BlockSpec((1, 1, dff, D), stacked_map),
                      pl.BlockSpec((1, 1, D), exp_map),
                      pl.BlockSpec((MXU_DIM, MXU_DIM), lambda i, a, b: (0, 0))],
            out_specs=pl.BlockSpec((nch, te, C), row_map),
            scratch_shapes=[pltpu.VMEM((D, dff), BF16), pltpu.VMEM((D, dff), BF16),
                            pltpu.VMEM((dff, D), BF16)],
        ),
        out_shape=jax.ShapeDtypeStruct((nch, n_rows, C), U32),
        compiler_params=_params("arbitrary"),
        name="moe_experts",
    )(tile_expert, n_used, x_rows, w_gu, bg, bl, w_down, bd, perm)


def _moe_combine(y_tok, gate, x1, ln_g, ln_b):
    nch, K, T, C = y_tok.shape
    D = x1.shape[1]
    tm = TOKEN_TILE
    gate_col = gate.T
    slot = lambda k: pl.BlockSpec((nch, None, tm, C), lambda i, k=k: (0, k, i, 0))
    vec = pl.BlockSpec((1, D), lambda i: (0, 0))
    return pl.pallas_call(
        _combine_kernel,
        grid=(T // tm,),
        in_specs=[slot(0), slot(1), slot(2), slot(3),
                  pl.BlockSpec((tm, K), lambda i: (i, 0)),
                  pl.BlockSpec((tm, D), lambda i: (i, 0)), vec, vec],
        out_specs=pl.BlockSpec((tm, D), lambda i: (i, 0)),
        out_shape=jax.ShapeDtypeStruct((T, D), F32),
        compiler_params=_params("parallel"),
        name="moe_combine_norm",
    )(y_tok, y_tok, y_tok, y_tok, gate_col, x1, ln_g.reshape(1, D), ln_b.reshape(1, D))


def kernel(x, positions, ln_gain, ln_bias, mlstm_w_in, mlstm_b_gates, mlstm_norm_gain,
           mlstm_w_out, mla_w_in, mla_q_norm, mla_kv_norm, mla_w_qb, mla_w_kvb, mla_w_out,
           moe_w_router, moe_b_router, moe_w_gate_up, moe_b_gate_up, moe_w_down, moe_b_down):
    B, S, D = x.shape
    T = B * S
    x2d = x.reshape(T, D)

    inv_freq = ROPE_THETA ** (-jnp.arange(0, MLA_ROPE, 2, dtype=F32) / MLA_ROPE)
    ang = positions.astype(F32).reshape(T, 1) * inv_freq
    reps = LANES // ang.shape[1]
    cos_t = jnp.tile(jnp.cos(ang), (1, reps))
    sin_t = jnp.tile(jnp.sin(ang), (1, reps))

    for layer in range(DEPTH):
        j = layer // 2
        if layer % 2 == 0:
            h = _mlstm_mixer(x2d, B, mlstm_w_in[j], mlstm_b_gates[j], mlstm_norm_gain[j])
            w_out = mlstm_w_out[j]
        else:
            h = _mla_mixer(x2d, B, cos_t, sin_t, mla_w_in[j], mla_q_norm[j], mla_kv_norm[j],
                           mla_w_qb[j], mla_w_kvb[j])
            w_out = mla_w_out[j]
        x1, x1p, idx, gate, pos, cnt = _post_mixer(h, x2d, w_out, ln_gain[layer, 0], ln_bias[layer, 0],
                                                   moe_w_router[layer], moe_b_router[layer])
        x_rows, dest, tile_expert, n_used = _moe_dispatch(x1p, idx, pos, cnt)
        y_rows = _moe_experts(x_rows, tile_expert, n_used, layer, moe_w_gate_up, moe_b_gate_up[layer],
                              moe_w_down, moe_b_down[layer])
        y_tok = _sc_gather_rows(y_rows, dest)
        x2d = _moe_combine(y_tok, gate, x1, ln_gain[layer, 1], ln_bias[layer, 1])
    return x2d.reshape(B, S, D)
```

```python
import functools

import jax
import jax.numpy as jnp
from jax import lax
from jax.experimental import pallas as pl
from jax.experimental.pallas import tpu as pltpu
from jax.experimental.pallas import tpu_sc as plsc

F32 = jnp.float32
BF16 = jnp.bfloat16
I32 = jnp.int32
U32 = jnp.uint32

DEPTH = 4
MLSTM_HEADS = 8
MLSTM_DQK = 64
MLSTM_DV = 128
MLA_HEADS = 8
MLA_Q_LORA = 384
MLA_KV_LORA = 256
MLA_NOPE = 128
MLA_ROPE = 64
MLA_V = 128
ROPE_THETA = 10000.0
N_EXPERTS = 32
TOP_K = 4
SWIGLU_LIMIT = 7.0
SWIGLU_ALPHA = 1.702
DEEPNORM_ALPHA = (2.0 * DEPTH) ** 0.25
LN_EPS = 1e-5
LOG2_E = 1.4426950408889634
RMS_EPS = 1e-6

LANES = 128
BF16_SUBLANES = 16
MXU_DIM = 256
VMEM_LIMIT_BYTES = 56 * 1024 * 1024

TOKEN_TILE = 1024
MLSTM_SEQ_BLOCK = 1024
MLSTM_CHUNK = 256
ATTN_TILE = 512
ATTN_HEADS_PER_STEP = 2
ATTN_QUERY_TILES_PER_STEP = 4
ATTN_SCORE_LOOKAHEAD = 1
EXPERT_TILE = 512
SC_WINDOW = 128
SC_COLS = 256

_NT = (((1,), (1,)), ((), ()))
HALF_WORD_BITS = 16
HIGH_HALF_MASK = 0xFFFF0000


def _params(*sem):
    return pltpu.CompilerParams(dimension_semantics=sem, vmem_limit_bytes=VMEM_LIMIT_BYTES)


def _layer_norm(z, g, b):
    mu = jnp.mean(z, axis=-1, keepdims=True)
    zc = z - mu
    var = jnp.mean(zc * zc, axis=-1, keepdims=True)
    return zc * lax.rsqrt(var + LN_EPS) * g + b


def _rms_norm(z, g):
    return z * lax.rsqrt(jnp.mean(z * z, axis=-1, keepdims=True) + RMS_EPS) * g


def _store_packed(ref, a):
    half = a.shape[1] // 2
    rounded = lambda v: lax.bitcast_convert_type(v.astype(BF16).astype(F32), U32)
    words = ((rounded(a[:, :half]) >> HALF_WORD_BITS)
             | (rounded(a[:, half:]) & jnp.uint32(HIGH_HALF_MASK)))
    for c in range(ref.shape[0]):
        ref[c] = words[:, c * SC_COLS:(c + 1) * SC_COLS]


def _load_packed(ref):
    chunks = [ref[c] for c in range(ref.shape[0])]
    lo = [lax.bitcast_convert_type(w << HALF_WORD_BITS, F32) for w in chunks]
    hi = [lax.bitcast_convert_type(w & jnp.uint32(HIGH_HALF_MASK), F32) for w in chunks]
    return jnp.concatenate(lo + hi, axis=1)


def _split3(a):
    hi = a.astype(BF16)
    r = a - hi.astype(F32)
    mid = r.astype(BF16)
    lo = (r - mid.astype(F32)).astype(BF16)
    return hi, mid, lo


def _mlstm_inproj_kernel(x_ref, wq_ref, wk_ref, wvt_ref, wot_ref, wg_ref, bg_ref,
                         q_ref, k_ref, vt_ref, ot_ref, gc_ref, gs_ref, gr_ref):
    H = MLSTM_HEADS
    xb = x_ref[...].astype(BF16)
    dot = functools.partial(jnp.dot, preferred_element_type=F32)
    ntdot = lambda a, b: lax.dot_general(a, b, _NT, preferred_element_type=F32)
    q_ref[...] = (dot(xb, wq_ref[...]) * (MLSTM_DQK ** -0.5)).astype(BF16)
    k_ref[...] = dot(xb, wk_ref[...]).astype(BF16)
    vt_ref[...] = ntdot(wvt_ref[...], xb).astype(BF16)
    ot_ref[...] = ntdot(wot_ref[...], xb)
    z = dot(xb, wg_ref[...]) + bg_ref[...]
    lane = lax.broadcasted_iota(I32, z.shape, 1)
    log_sig = jnp.minimum(z, 0.0) - jnp.log1p(jnp.exp(-jnp.abs(z)))
    g = jnp.where(lane < H, z, log_sig) * LOG2_E
    gc_ref[...] = g
    gs_ref[...] = pltpu.roll(g, LANES - H, 1)
    gr_ref[...] = g.T[:2 * H, :]


def _mlstm_cell_kernel(q_ref, k_ref, vt_ref, ot_ref, gc_ref, gs_ref, gr_ref, ng_ref, out_ref,
                       c_ref, m_ref, *, chunk, n_chunks):
    H, dk, dv = MLSTM_HEADS, MLSTM_DQK, MLSTM_DV
    reps = chunk // LANES
    aug = c_ref.shape[1] - dv

    @pl.when(pl.program_id(1) == 0)
    def _():
        c_ref[...] = jnp.zeros_like(c_ref)
        m_ref[...] = jnp.zeros_like(m_ref)

    s_idx = lax.broadcasted_iota(I32, (chunk, chunk), 0)
    j_idx = lax.broadcasted_iota(I32, (chunk, chunk), 1)
    visible = s_idx <= j_idx
    tri_upper = visible.astype(BF16)
    tri_lower = (j_idx <= s_idx).astype(BF16)
    ones_rows = (lax.broadcasted_iota(I32, (aug, chunk), 0) == 0).astype(BF16)
    dot = functools.partial(jnp.dot, preferred_element_type=F32)
    ntdot = lambda a, b: lax.dot_general(a, b, _NT, preferred_element_type=F32)

    def chunk_body(c, carry):
        rows = pl.ds(pl.multiple_of(c * chunk, chunk), chunk)
        gr = gr_ref[:, rows]
        b_cols = sum(dot(tri_lower, p) for p in _split3(gs_ref[rows, :]))
        b_rows = sum(dot(p, tri_upper) for p in _split3(gr))
        a_cols = gc_ref[rows, :] - b_cols
        hs = range(H)
        ig_r = [gr[h:h + 1, :] for h in hs]
        b_r = [b_rows[H + h:H + h + 1, :] for h in hs]
        b_last = [b[:, chunk - 1:chunk] for b in b_r]
        m_row = [jnp.concatenate([m_ref[h:h + 1, :]] * reps, axis=1) for h in hs]
        m_old = [m[:, 0:1] for m in m_row]
        qh = [q_ref[rows, h * dk:(h + 1) * dk] for h in hs]
        kh = [k_ref[rows, h * dk:(h + 1) * dk] for h in hs]
        vaug = [jnp.concatenate([vt_ref[h * dv:(h + 1) * dv, rows], ones_rows], axis=0)
                for h in hs]
        ct = [c_ref[h] for h in hs]

        qk = [ntdot(kh[h], qh[h]) for h in hs]
        qc = [ntdot(ct[h].astype(BF16), qh[h]) for h in hs]

        m_new = [jnp.maximum(b_last[h] + m_old[h],
                             jnp.max(b_last[h] - b_r[h] + ig_r[h], axis=1, keepdims=True)) for h in hs]
        for h in hs:
            w_row = jnp.exp2(b_last[h] - b_r[h] + ig_r[h] - m_new[h])
            vw = vaug[h] * w_row.astype(BF16)
            c_ref[h] = jnp.exp2(b_last[h] + m_old[h] - m_new[h]) * ct[h] + dot(vw, kh[h])
            m_ref[h:h + 1, :] = jnp.broadcast_to(m_new[h], (1, LANES))

        a_mat = [jnp.where(visible, a_cols[:, h:h + 1], -jnp.inf) for h in hs]
        g = [jnp.maximum(m_row[h], jnp.max(a_mat[h], axis=0, keepdims=True)) for h in hs]
        s = [qk[h] * jnp.exp2(a_mat[h] - g[h]) for h in hs]
        nd = [dot(vaug[h], s[h].astype(BF16)) + jnp.exp2(m_row[h] - g[h]) * qc[h]
              for h in hs]
        for h in hs:
            num, den = nd[h][:dv], nd[h][dv:dv + 1]
            r = 1.0 / jnp.maximum(jnp.abs(den), jnp.exp2(-(b_r[h] + g[h])))
            scale = r * lax.rsqrt(r * r * jnp.mean(num * num, axis=0, keepdims=True) + RMS_EPS)
            gain = jnp.concatenate([ng_ref[h * dv:(h + 1) * dv, :]] * reps, axis=1)
            out_t = jax.nn.sigmoid(ot_ref[h * dv:(h + 1) * dv, rows]) * (num * scale * gain)
            out_ref[rows, h * dv:(h + 1) * dv] = out_t.T.astype(out_ref.dtype)
        return carry

    lax.fori_loop(0, n_chunks, chunk_body, 0)


def _mlstm_mixer(x2d, batch, w_in, b_gates, norm_gain):
    T, D = x2d.shape
    H, dk, dv = MLSTM_HEADS, MLSTM_DQK, MLSTM_DV
    seq = T // batch
    tm = TOKEN_TILE
    cq, ck, cv, co = H * dk, 2 * H * dk, 2 * H * dk + H * dv, 2 * H * dk + 2 * H * dv
    wq = w_in[:, :cq].astype(BF16)
    wk = w_in[:, cq:ck].astype(BF16)
    wvt = w_in[:, ck:cv].T.astype(BF16)
    wot = w_in[:, cv:co].T.astype(BF16)
    wg = jnp.pad(w_in[:, co:], ((0, 0), (0, LANES - 2 * H))).astype(BF16)
    bg = jnp.pad(b_gates, (0, LANES - 2 * H)).reshape(1, LANES)

    full = lambda a: pl.BlockSpec(a.shape, lambda i: (0,) * a.ndim)
    tok = lambda w: pl.BlockSpec((tm, w), lambda i: (i, 0))
    tok_t = lambda h: pl.BlockSpec((h, tm), lambda i: (0, i))
    q, k, vt, ot, gc, gs, gr = pl.pallas_call(
        _mlstm_inproj_kernel,
        grid=(T // tm,),
        in_specs=[tok(D), full(wq), full(wk), full(wvt), full(wot), full(wg), full(bg)],
        out_specs=[tok(H * dk), tok(H * dk), tok_t(H * dv), tok_t(H * dv),
                   tok(LANES), tok(LANES), tok_t(2 * H)],
        out_shape=[jax.ShapeDtypeStruct((T, H * dk), BF16),
                   jax.ShapeDtypeStruct((T, H * dk), BF16),
                   jax.ShapeDtypeStruct((H * dv, T), BF16),
                   jax.ShapeDtypeStruct((H * dv, T), F32),
                   jax.ShapeDtypeStruct((T, LANES), F32),
                   jax.ShapeDtypeStruct((T, LANES), F32),
                   jax.ShapeDtypeStruct((2 * H, T), F32)],
        compiler_params=_params("parallel"),
        name="mlstm_inproj",
    )(x2d, wq, wk, wvt, wot, wg, bg)

    ts = min(MLSTM_SEQ_BLOCK, seq)
    chunk = min(MLSTM_CHUNK, ts)
    nsb = seq // ts
    ng = jnp.broadcast_to(norm_gain.reshape(H * dv, 1), (H * dv, LANES))
    seq_rows = lambda w: pl.BlockSpec((ts, w), lambda b, s: (b * nsb + s, 0))
    seq_lanes = lambda h: pl.BlockSpec((h, ts), lambda b, s: (0, b * nsb + s))
    return pl.pallas_call(
        functools.partial(_mlstm_cell_kernel, chunk=chunk, n_chunks=ts // chunk),
        grid=(batch, nsb),
        in_specs=[seq_rows(H * dk), seq_rows(H * dk), seq_lanes(H * dv), seq_lanes(H * dv),
                  seq_rows(LANES), seq_rows(LANES), seq_lanes(2 * H),
                  pl.BlockSpec((H * dv, LANES), lambda b, s: (0, 0))],
        out_specs=seq_rows(H * dv),
        out_shape=jax.ShapeDtypeStruct((T, H * dv), BF16),
        scratch_shapes=[pltpu.VMEM((H, dv + BF16_SUBLANES, dk), F32), pltpu.VMEM((H, LANES), F32)],
        compiler_params=_params("parallel", "arbitrary"),
        name="mlstm_cell",
    )(q, k, vt, ot, gc, gs, gr, ng)


def _mla_proj_kernel(x_ref, cos_ref, sin_ref, wcq_ref, wckv_ref, wkr_ref, qn_ref, kvn_ref,
                     wq_ref, wqr_ref, wkn_ref, wvt_ref, q_ref, k_ref, vt_ref):
    H, dn = MLA_HEADS, MLA_NOPE
    dot = functools.partial(jnp.dot, preferred_element_type=F32)
    xb = x_ref[...].astype(BF16)
    cos, sin = cos_ref[...], sin_ref[...]
    c_q = _rms_norm(dot(xb, wcq_ref[...]), qn_ref[...]).astype(BF16)
    c_kv = _rms_norm(dot(xb, wckv_ref[...]), kvn_ref[...]).astype(BF16)
    kr2 = dot(xb, wkr_ref[...])
    kr = (kr2[:, :LANES] * cos + kr2[:, LANES:] * sin).astype(BF16)
    scale = (MLA_NOPE + MLA_ROPE) ** -0.5 * LOG2_E
    qa = dot(c_q, wq_ref[...])
    qr = dot(c_q, wqr_ref[...])
    kn = dot(c_kv, wkn_ref[...])
    for h in range(H):
        base = h * 2 * LANES
        q_ref[h, :, :dn] = (qa[:, base:base + dn] * scale).astype(BF16)
        rope = qa[:, base + dn:base + 2 * LANES] * cos + qr[:, h * LANES:(h + 1) * LANES] * sin
        q_ref[h, :, dn:] = (rope * scale).astype(BF16)
        k_ref[h, :, :dn] = kn[:, h * dn:(h + 1) * dn].astype(BF16)
        k_ref[h, :, dn:] = kr
    vt_ref[...] = lax.dot_general(wvt_ref[...], c_kv, _NT, preferred_element_type=F32).astype(BF16)


def _attn_kernel(q_ref, k_ref, vt_ref, o_ref, bad_ref, m_ref, mu_ref, acc_ref,
                 *, tile, heads, qsub, lagged):
    dv = MLA_V
    qi = pl.program_id(2)
    m_ref[...] = jnp.full_like(m_ref, -jnp.inf)
    mu_ref[...] = jnp.full_like(mu_ref, -jnp.inf)
    acc_ref[...] = jnp.zeros_like(acc_ref)
    ones_rows = jnp.ones((acc_ref.shape[1] - dv, tile), BF16)

    def emit(tiles):
        units = [(j, ds, ex, g, u) for j, ds, ex in tiles for g in range(heads)
                 for u in range(qsub) if ds is None or u >= ds]
        keys_of = lambda j: pl.ds(pl.multiple_of(j * tile, tile), tile)
        sts = {}

        def score(i):
            j, ds, ex, g, u = units[i]
            st = lax.dot_general(k_ref[g, keys_of(j), :], q_ref[g, u * tile:(u + 1) * tile, :], _NT,
                                 preferred_element_type=F32)
            if u == ds:
                kpos = lax.broadcasted_iota(I32, st.shape, 0)
                qpos = lax.broadcasted_iota(I32, st.shape, 1)
                st = jnp.where(kpos <= qpos, st, -jnp.inf)
            sts[i] = st

        def softmax_value(i):
            j, ds, ex, g, u = units[i]
            c = g * qsub + u
            st = sts.pop(i)
            tile_max = jnp.max(st, axis=0, keepdims=True)
            seen = m_ref[c]
            if ex == "two_pass":
                stab = jnp.maximum(seen, tile_max)
            elif ex == "key0":
                stab = st[0:1, :]
            else:
                stab = seen
            p = jnp.exp2(st - stab).astype(BF16)
            vt = jnp.concatenate([vt_ref[g * dv:(g + 1) * dv, keys_of(j)], ones_rows], axis=0)
            acc_ref[c] = jnp.exp2(mu_ref[c] - stab) * acc_ref[c] + jnp.dot(
                vt, p, preferred_element_type=F32)
            mu_ref[c] = stab
            m_ref[c] = jnp.maximum(seen, tile_max)

        ahead = ATTN_SCORE_LOOKAHEAD
        for i in range(min(ahead, len(units))):
            score(i)
        for i in range(len(units)):
            if i + ahead < len(units):
                score(i + ahead)
            softmax_value(i)

    ex = "earlier" if lagged else "two_pass"
    ex0 = "key0" if lagged else "two_pass"

    @pl.when(qi > 0)
    def _():
        emit([(0, None, ex0)] + [(d, None, ex) for d in range(1, qsub)])

    def body(i, carry):
        emit([(qsub * i + d, None, ex) for d in range(qsub)])
        return carry

    lax.fori_loop(1, qi, body, 0)

    @pl.when(qi > 0)
    def _():
        emit([(qsub * qi + u, u, ex) for u in range(qsub)])

    @pl.when(qi == 0)
    def _():
        emit([(0, 0, ex0)] + [(u, u, ex) for u in range(1, qsub)])

    for g in range(heads):
        for u in range(qsub):
            c = g * qsub + u
            acc = acc_ref[c]
            o_ref[u * tile:(u + 1) * tile, g * dv:(g + 1) * dv] = (
                acc[:dv] * (1.0 / acc[dv:dv + 1])).T.astype(o_ref.dtype)
            bad_ref[0, c] = jnp.max(jnp.where(acc - acc == 0.0, 0.0, 1.0), axis=0, keepdims=True)


def _mla_mixer(x2d, batch, cos_t, sin_t, w_in, q_norm, kv_norm, w_qb, w_kvb):
    T, D = x2d.shape
    H, dn, dr, dv = MLA_HEADS, MLA_NOPE, MLA_ROPE, MLA_V
    seq = T // batch
    tm = TOKEN_TILE
    ql, kl = MLA_Q_LORA, MLA_KV_LORA
    half = dr // 2

    def rot(w):
        return jnp.concatenate([-w[..., half:], w[..., :half]], axis=-1)

    wcq = w_in[:, :ql].astype(BF16)
    wckv = w_in[:, ql:ql + kl].astype(BF16)
    wr = w_in[:, ql + kl:]
    zr = jnp.zeros((D, LANES - dr), F32)
    wkr = jnp.concatenate([wr, zr, rot(wr), zr], axis=1).astype(BF16)
    wq3 = w_qb.reshape(ql, H, dn + dr)
    zq = jnp.zeros((ql, H, LANES - dr), F32)
    wq = jnp.concatenate([wq3, zq], axis=2).reshape(ql, H * 2 * LANES).astype(BF16)
    wqr = jnp.concatenate([rot(wq3[:, :, dn:]), zq], axis=2).reshape(ql, H * LANES).astype(BF16)
    wkv3 = w_kvb.reshape(kl, H, dn + dv)
    wkn = wkv3[:, :, :dn].reshape(kl, H * dn).astype(BF16)
    wvt = wkv3[:, :, dn:].reshape(kl, H * dv).T.astype(BF16)
    qn = q_norm.reshape(1, ql)
    kvn = kv_norm.reshape(1, kl)

    full = lambda a: pl.BlockSpec(a.shape, lambda i: (0,) * a.ndim)
    q, k, vt = pl.pallas_call(
        _mla_proj_kernel,
        grid=(T // tm,),
        in_specs=[pl.BlockSpec((tm, D), lambda i: (i, 0)),
                  pl.BlockSpec((tm, LANES), lambda i: (i, 0)),
                  pl.BlockSpec((tm, LANES), lambda i: (i, 0)),
                  full(wcq), full(wckv), full(wkr), full(qn), full(kvn),
                  full(wq), full(wqr), full(wkn), full(wvt)],
        out_specs=[pl.BlockSpec((H, tm, 2 * LANES), lambda i: (0, i, 0)),
                   pl.BlockSpec((H, tm, 2 * LANES), lambda i: (0, i, 0)),
                   pl.BlockSpec((H * dv, tm), lambda i: (0, i))],
        out_shape=[jax.ShapeDtypeStruct((H, T, 2 * LANES), BF16),
                   jax.ShapeDtypeStruct((H, T, 2 * LANES), BF16),
                   jax.ShapeDtypeStruct((H * dv, T), BF16)],
        compiler_params=_params("parallel"),
        name="mla_proj",
    )(x2d, cos_t, sin_t, wcq, wckv, wkr, qn, kvn, wq, wqr, wkn, wvt)

    tile = min(ATTN_TILE, seq)
    hp = ATTN_HEADS_PER_STEP
    qsub = min(ATTN_QUERY_TILES_PER_STEP, seq // tile)
    tq = tile * qsub
    nq = seq // tq
    nhp = H // hp
    nc = hp * qsub

    def attention(lagged):
        return pl.pallas_call(
            functools.partial(_attn_kernel, tile=tile, heads=hp, qsub=qsub, lagged=lagged),
            grid=(batch, nhp, nq),
            in_specs=[pl.BlockSpec((hp, tq, 2 * LANES), lambda b, h, i: (h, b * nq + i, 0)),
                      pl.BlockSpec((hp, seq, 2 * LANES), lambda b, h, i: (h, b, 0)),
                      pl.BlockSpec((hp * dv, seq), lambda b, h, i: (h, b))],
            out_specs=[pl.BlockSpec((tq, hp * dv), lambda b, h, i: (b * nq + i, h)),
                       pl.BlockSpec((1, nc, 1, tile), lambda b, h, i: ((b * nhp + h) * nq + i, 0, 0, 0))],
            out_shape=[jax.ShapeDtypeStruct((T, H * dv), BF16),
                       jax.ShapeDtypeStruct((batch * nhp * nq, nc, 1, tile), F32)],
            scratch_shapes=[pltpu.VMEM((nc, 1, tile), F32), pltpu.VMEM((nc, 1, tile), F32),
                            pltpu.VMEM((nc, dv + BF16_SUBLANES, tile), F32)],
            compiler_params=_params("parallel", "parallel", "arbitrary"),
            name="mla_attention" if lagged else "mla_attention_exact",
        )(q, k, vt)

    out, not_finite = attention(lagged=True)
    return lax.cond(jnp.any(not_finite > 0.0), lambda: attention(lagged=False)[0], lambda: out)


def _post_kernel(h_ref, x_ref, w_ref, g_ref, b_ref, wr_ref, br_ref, tri_ref,
                 xo_ref, xp_ref, idx_ref, gate_ref, pos_ref, cnt_ref, run_ref):
    E = N_EXPERTS
    tm = x_ref.shape[0]

    @pl.when(pl.program_id(0) == 0)
    def _():
        run_ref[...] = jnp.zeros_like(run_ref)

    mix = jnp.dot(h_ref[...], w_ref[...], preferred_element_type=F32)
    x1 = _layer_norm(DEEPNORM_ALPHA * x_ref[...] + mix, g_ref[...], b_ref[...])
    xo_ref[...] = x1
    _store_packed(xp_ref, x1)

    xh = x1.astype(BF16)
    xl = (x1 - xh.astype(F32)).astype(BF16)
    wh, wl = wr_ref[0], wr_ref[1]
    ntdot = lambda a, b: lax.dot_general(a, b, _NT, preferred_element_type=F32)
    rest = ntdot(wh, xh) + ntdot(wh, xl) + ntdot(wl, xh) + br_ref[...]

    e_iota = lax.broadcasted_iota(I32, (E, tm), 0)
    vals, sels = [], []
    for k in range(TOP_K):
        v = jnp.max(rest, axis=0, keepdims=True)
        ik = jnp.min(jnp.where(rest == v, e_iota, E), axis=0, keepdims=True)
        sel = e_iota == ik
        rest = jnp.where(sel, -jnp.inf, rest)
        vals.append(v)
        sels.append(sel)
        idx_ref[k:k + 1, :] = ik
    ex = [jnp.exp(v - vals[0]) for v in vals]
    inv = 1.0 / sum(ex)
    for k in range(TOP_K):
        gate_ref[k:k + 1, :] = ex[k] * inv

    chosen = functools.reduce(jnp.logical_or, sels)
    before = jnp.dot(chosen.astype(BF16), tri_ref[...], preferred_element_type=F32)
    run = run_ref[:, 0:1]
    rank = before + run
    for k in range(TOP_K):
        pos_ref[k:k + 1, :] = jnp.sum(jnp.where(sels[k], rank, 0.0), axis=0,
                                      keepdims=True).astype(I32)
    run_new = run + jnp.sum(chosen.astype(F32), axis=1, keepdims=True)
    run_ref[...] = jnp.broadcast_to(run_new, run_ref.shape)
    cnt_ref[...] = jnp.broadcast_to(run_new, cnt_ref.shape)


def _post_mixer(h, x2d, w_out, ln_g, ln_b, w_router, b_router):
    T, D = x2d.shape
    nch = D // (2 * SC_COLS)
    E = N_EXPERTS
    tm = TOKEN_TILE
    wo = w_out.astype(BF16)
    wrt = w_router.T
    wrh = wrt.astype(BF16)
    wr = jnp.stack([wrh, (wrt - wrh.astype(F32)).astype(BF16)])
    full = lambda a: pl.BlockSpec(a.shape, lambda i: (0,) * a.ndim)
    g, b, br = ln_g.reshape(1, D), ln_b.reshape(1, D), b_router.reshape(E, 1)
    pos_ids = jnp.arange(tm, dtype=I32)
    tri = (pos_ids[:, None] < pos_ids[None, :]).astype(BF16)
    rows = TOP_K
    return pl.pallas_call(
        _post_kernel,
        grid=(T // tm,),
        in_specs=[pl.BlockSpec((tm, h.shape[1]), lambda i: (i, 0)),
                  pl.BlockSpec((tm, D), lambda i: (i, 0)),
                  full(wo), full(g), full(b), full(wr), full(br), full(tri)],
        out_specs=[pl.BlockSpec((tm, D), lambda i: (i, 0)),
                   pl.BlockSpec((nch, tm, SC_COLS), lambda i: (0, i, 0)),
                   pl.BlockSpec((rows, tm), lambda i: (0, i)),
                   pl.BlockSpec((rows, tm), lambda i: (0, i)),
                   pl.BlockSpec((rows, tm), lambda i: (0, i)),
                   pl.BlockSpec((E, LANES), lambda i: (0, 0))],
        out_shape=[jax.ShapeDtypeStruct((T, D), F32),
                   jax.ShapeDtypeStruct((nch, T, SC_COLS), U32),
                   jax.ShapeDtypeStruct((rows, T), I32),
                   jax.ShapeDtypeStruct((rows, T), F32),
                   jax.ShapeDtypeStruct((rows, T), I32),
                   jax.ShapeDtypeStruct((E, LANES), F32)],
        scratch_shapes=[pltpu.VMEM((E, LANES), F32)],
        compiler_params=_params("arbitrary"),
        name="post_mixer_router",
    )(h, x2d, wo, g, b, wr, br, tri)


def _sc_mesh():
    return plsc.VectorSubcoreMesh(core_axis_name="core", subcore_axis_name="subcore")


def _sc_scatter_rows(xc, dest, n_rows):
    nch, T, C = xc.shape
    K = dest.shape[0]
    W = SC_WINDOW
    xs = xc.reshape(nch * T, C)
    nb = (nch * T) // W
    offs = (jnp.arange(nch, dtype=I32) * n_rows)[None, :, None]
    idx = (dest[:, None, :] + offs).reshape(K, nch * T)

    @functools.partial(pl.kernel, out_type=jax.ShapeDtypeStruct((nch * n_rows, C), xc.dtype),
                       mesh=_sc_mesh(), scratch_types=[], name="moe_dispatch_scatter")
    def scatter(x_hbm, i_hbm, o_hbm):
        def body(x_vmem, i_vmem):
            for k in range(K):
                pltpu.sync_copy(x_vmem, o_hbm.at[i_vmem.at[k]])

        pltpu.emit_pipeline(
            body,
            grid=(nb,),
            in_specs=[pl.BlockSpec((W, C), lambda g: (g, 0)),
                      pl.BlockSpec((K, W), lambda g: (0, g))],
            out_specs=[],
            core_axis_name=("core", "subcore"),
            dimension_semantics=(pltpu.PARALLEL,),
        )(x_hbm, i_hbm)

    return scatter(xs, idx).reshape(nch, n_rows, C)


def _sc_gather_rows(yc, dest):
    nch, n_rows, C = yc.shape
    K, T = dest.shape
    W = SC_WINDOW
    ys = yc.reshape(nch * n_rows, C)
    offs = (jnp.arange(nch, dtype=I32) * n_rows)[:, None, None]
    idx = (dest[None, :, :] + offs).reshape(1, nch * K * T)
    n_sub = nch * K * T

    @functools.partial(pl.kernel, out_type=jax.ShapeDtypeStruct((n_sub, C), yc.dtype),
                       mesh=_sc_mesh(), scratch_types=[], name="moe_combine_gather")
    def gather(t_hbm, i_hbm, o_hbm):
        def body(i_vmem, o_vmem):
            pltpu.sync_copy(t_hbm.at[i_vmem.at[0]], o_vmem)

        pltpu.emit_pipeline(
            body,
            grid=(n_sub // W,),
            in_specs=[pl.BlockSpec((1, W), lambda g: (0, g))],
            out_specs=[pl.BlockSpec((W, C), lambda g: (g, 0))],
            core_axis_name=("core", "subcore"),
            dimension_semantics=(pltpu.PARALLEL,),
        )(i_hbm, o_hbm)

    return gather(ys, idx).reshape(nch, K, T, C)


def _expert_kernel(te_ref, nu_ref, x_ref, wgu_ref, bg_ref, bl_ref, wd_ref, bd_ref, perm_ref,
                   y_ref, wg_s, wl_s, wd_s):
    i = pl.program_id(0)
    active = i < nu_ref[0]
    e = te_ref[i]
    changed = jnp.logical_or(i == 0, e != te_ref[jnp.maximum(i - 1, 0)])
    dot = functools.partial(jnp.dot, preferred_element_type=F32)

    @pl.when(jnp.logical_and(active, changed))
    def _():
        n_blocks = wgu_ref.shape[3] // MXU_DIM
        for blk in range(n_blocks):
            wb = wgu_ref[0, 0, :, blk * MXU_DIM:(blk + 1) * MXU_DIM].astype(BF16)
            wp = dot(wb, perm_ref[...]).astype(BF16)
            wg_s[:, blk * LANES:(blk + 1) * LANES] = wp[:, :LANES]
            wl_s[:, blk * LANES:(blk + 1) * LANES] = wp[:, LANES:]
        wd_s[...] = wd_ref[0, 0].astype(BF16)

    @pl.when(active)
    def _():
        xb = _load_packed(x_ref).astype(BF16)
        g = jnp.minimum(dot(xb, wg_s[...]) + bg_ref[0], SWIGLU_LIMIT)
        lin = jnp.clip(dot(xb, wl_s[...]) + bl_ref[0], -SWIGLU_LIMIT, SWIGLU_LIMIT)
        act = (lin + 1.0) * g * jax.nn.sigmoid(SWIGLU_ALPHA * g)
        _store_packed(y_ref, dot(act.astype(BF16), wd_s[...]) + bd_ref[0])


def _combine_kernel(y0_ref, y1_ref, y2_ref, y3_ref, gate_ref, x_ref, g_ref, b_ref, o_ref):
    gate = gate_ref[...]
    ff = (gate[:, 0:1] * _load_packed(y0_ref) + gate[:, 1:2] * _load_packed(y1_ref)
          + gate[:, 2:3] * _load_packed(y2_ref) + gate[:, 3:4] * _load_packed(y3_ref))
    o_ref[...] = _layer_norm(DEEPNORM_ALPHA * x_ref[...] + ff, g_ref[...], b_ref[...])


def _moe_dispatch(x1p, idx, pos, cnt):
    nch, T, C = x1p.shape
    E, K = N_EXPERTS, TOP_K
    te = EXPERT_TILE
    n_tiles = (T * K) // te + E
    n_rows = n_tiles * te

    counts = cnt[:, 0].astype(I32)
    padded = (counts + te - 1) // te * te
    pad_end = jnp.cumsum(padded)
    pad_start = pad_end - padded
    experts = jnp.arange(E, dtype=I32)
    start_of = jnp.sum(jnp.where(idx[None] == experts[:, None, None],
                                 pad_start[:, None, None], 0), axis=0)
    dest = start_of + pos
    tile_start = jnp.arange(n_tiles, dtype=I32) * te
    tile_expert = jnp.minimum(
        jnp.sum((pad_end[None, :] <= tile_start[:, None]).astype(I32), axis=1), E - 1)
    n_used = (pad_end[-1] // te).astype(I32).reshape(1)

    x_rows = _sc_scatter_rows(x1p, dest, n_rows)
    return x_rows, dest, tile_expert, n_used


def _moe_experts(x_rows, tile_expert, n_used, layer, w_gu, b_gu, w_down, b_down):
    nch, n_rows, C = x_rows.shape
    E = N_EXPERTS
    D, dff = w_down.shape[3], w_down.shape[2]
    te = EXPERT_TILE
    n_tiles = n_rows // te
    half = MXU_DIM // 2
    src = jnp.arange(MXU_DIM)
    perm = (src[:, None] == jnp.where(src < half, 2 * src, 2 * (src - half) + 1)[None, :])
    perm = perm.astype(BF16)
    bg = b_gu[:, 0::2].reshape(E, 1, dff)
    bl = b_gu[:, 1::2].reshape(E, 1, dff)
    bd = b_down.reshape(E, 1, D)

    def row_map(i, te_ref, nu_ref):
        return (0, jnp.minimum(i, nu_ref[0] - 1), 0)

    def exp_map(i, te_ref, nu_ref):
        return (te_ref[i], 0, 0)

    def stacked_map(i, te_ref, nu_ref):
        return (layer, te_ref[i], 0, 0)

    return pl.pallas_call(
        _expert_kernel,
        grid_spec=pltpu.PrefetchScalarGridSpec(
            num_scalar_prefetch=2,
            grid=(n_tiles,),
            in_specs=[pl.BlockSpec((nch, te, C), row_map),
                      pl.BlockSpec((1, 1, D, 2 * dff), stacked_map),
                      pl.BlockSpec((1, 1, dff), exp_map),
                      pl.BlockSpec((1, 1, dff), exp_map),
                      pl.BlockSpec((1, 1, dff, D), stacked_map),
                      pl.BlockSpec((1, 1, D), exp_map),
                      pl.BlockSpec((MXU_DIM, MXU_DIM), lambda i, a, b: (0, 0))],
            out_specs=pl.BlockSpec((nch, te, C), row_map),
            scratch_shapes=[pltpu.VMEM((D, dff), BF16), pltpu.VMEM((D, dff), BF16),
                            pltpu.VMEM((dff, D), BF16)],
        ),
        out_shape=jax.ShapeDtypeStruct((nch, n_rows, C), U32),
        compiler_params=_params("arbitrary"),
        name="moe_experts",
    )(tile_expert, n_used, x_rows, w_gu, bg, bl, w_down, bd, perm)


def _moe_combine(y_tok, gate, x1, ln_g, ln_b):
    nch, K, T, C = y_tok.shape
    D = x1.shape[1]
    tm = TOKEN_TILE
    gate_col = gate.T
    slot = lambda k: pl.BlockSpec((nch, None, tm, C), lambda i, k=k: (0, k, i, 0))
    vec = pl.BlockSpec((1, D), lambda i: (0, 0))
    return pl.pallas_call(
        _combine_kernel,
        grid=(T // tm,),
        in_specs=[slot(0), slot(1), slot(2), slot(3),
                  pl.BlockSpec((tm, K), lambda i: (i, 0)),
                  pl.BlockSpec((tm, D), lambda i: (i, 0)), vec, vec],
        out_specs=pl.BlockSpec((tm, D), lambda i: (i, 0)),
        out_shape=jax.ShapeDtypeStruct((T, D), F32),
        compiler_params=_params("parallel"),
        name="moe_combine_norm",
    )(y_tok, y_tok, y_tok, y_tok, gate_col, x1, ln_g.reshape(1, D), ln_b.reshape(1, D))


def kernel(x, positions, ln_gain, ln_bias, mlstm_w_in, mlstm_b_gates, mlstm_norm_gain,
           mlstm_w_out, mla_w_in, mla_q_norm, mla_kv_norm, mla_w_qb, mla_w_kvb, mla_w_out,
           moe_w_router, moe_b_router, moe_w_gate_up, moe_b_gate_up, moe_w_down, moe_b_down):
    B, S, D = x.shape
    T = B * S
    x2d = x.reshape(T, D)

    inv_freq = ROPE_THETA ** (-jnp.arange(0, MLA_ROPE, 2, dtype=F32) / MLA_ROPE)
    ang = positions.astype(F32).reshape(T, 1) * inv_freq
    reps = LANES // ang.shape[1]
    cos_t = jnp.tile(jnp.cos(ang), (1, reps))
    sin_t = jnp.tile(jnp.sin(ang), (1, reps))

    for layer in range(DEPTH):
        j = layer // 2
        if layer % 2 == 0:
            h = _mlstm_mixer(x2d, B, mlstm_w_in[j], mlstm_b_gates[j], mlstm_norm_gain[j])
            w_out = mlstm_w_out[j]
        else:
            h = _mla_mixer(x2d, B, cos_t, sin_t, mla_w_in[j], mla_q_norm[j], mla_kv_norm[j],
                           mla_w_qb[j], mla_w_kvb[j])
            w_out = mla_w_out[j]
        x1, x1p, idx, gate, pos, cnt = _post_mixer(h, x2d, w_out, ln_gain[layer, 0], ln_bias[layer, 0],
                                                   moe_w_router[layer], moe_b_router[layer])
        x_rows, dest, tile_expert, n_used = _moe_dispatch(x1p, idx, pos, cnt)
        y_rows = _moe_experts(x_rows, tile_expert, n_used, layer, moe_w_gate_up, moe_b_gate_up[layer],
                              moe_w_down, moe_b_down[layer])
        y_tok = _sc_gather_rows(y_rows, dest)
        x2d = _moe_combine(y_tok, gate, x1, ln_gain[layer, 1], ln_bias[layer, 1])
    return x2d.reshape(B, S, D)
```

```python
import functools

import jax
import jax.numpy as jnp
from jax import lax
from jax.experimental import pallas as pl
from jax.experimental.pallas import tpu as pltpu
from jax.experimental.pallas import tpu_sc as plsc

F32 = jnp.float32
BF16 = jnp.bfloat16
I32 = jnp.int32
U32 = jnp.uint32

DEPTH = 4
MLSTM_HEADS = 8
MLSTM_DQK = 64
MLSTM_DV = 128
MLA_HEADS = 8
MLA_Q_LORA = 384
MLA_KV_LORA = 256
MLA_NOPE = 128
MLA_ROPE = 64
MLA_V = 128
ROPE_THETA = 10000.0
N_EXPERTS = 32
TOP_K = 4
SWIGLU_LIMIT = 7.0
SWIGLU_ALPHA = 1.702
DEEPNORM_ALPHA = (2.0 * DEPTH) ** 0.25
LN_EPS = 1e-5
LOG2_E = 1.4426950408889634
RMS_EPS = 1e-6

LANES = 128
BF16_SUBLANES = 16
MXU_DIM = 256
VMEM_LIMIT_BYTES = 56 * 1024 * 1024

TOKEN_TILE = 1024
MLSTM_SEQ_BLOCK = 1024
MLSTM_CHUNK = 256
ATTN_TILE = 512
ATTN_HEADS_PER_STEP = 2
ATTN_QUERY_TILES_PER_STEP = 4
ATTN_SCORE_LOOKAHEAD = 1
EXPERT_TILE = 512
SC_WINDOW = 128
SC_COLS = 256

_NT = (((1,), (1,)), ((), ()))
HALF_WORD_BITS = 16
HIGH_HALF_MASK = 0xFFFF0000


def _params(*sem):
    return pltpu.CompilerParams(dimension_semantics=sem, vmem_limit_bytes=VMEM_LIMIT_BYTES)


def _layer_norm(z, g, b):
    mu = jnp.mean(z, axis=-1, keepdims=True)
    zc = z - mu
    var = jnp.mean(zc * zc, axis=-1, keepdims=True)
    return zc * lax.rsqrt(var + LN_EPS) * g + b


def _rms_norm(z, g):
    return z * lax.rsqrt(jnp.mean(z * z, axis=-1, keepdims=True) + RMS_EPS) * g


def _store_packed(ref, a):
    half = a.shape[1] // 2
    rounded = lambda v: lax.bitcast_convert_type(v.astype(BF16).astype(F32), U32)
    words = ((rounded(a[:, :half]) >> HALF_WORD_BITS)
             | (rounded(a[:, half:]) & jnp.uint32(HIGH_HALF_MASK)))
    for c in range(ref.shape[0]):
        ref[c] = words[:, c * SC_COLS:(c + 1) * SC_COLS]


def _load_packed(ref):
    chunks = [ref[c] for c in range(ref.shape[0])]
    lo = [lax.bitcast_convert_type(w << HALF_WORD_BITS, F32) for w in chunks]
    hi = [lax.bitcast_convert_type(w & jnp.uint32(HIGH_HALF_MASK), F32) for w in chunks]
    return jnp.concatenate(lo + hi, axis=1)


def _split3(a):
    hi = a.astype(BF16)
    r = a - hi.astype(F32)
    mid = r.astype(BF16)
    lo = (r - mid.astype(F32)).astype(BF16)
    return hi, mid, lo


def _mlstm_inproj_kernel(x_ref, wq_ref, wk_ref, wvt_ref, wot_ref, wg_ref, bg_ref,
                         q_ref, k_ref, vt_ref, ot_ref, gc_ref, gs_ref, gr_ref):
    H = MLSTM_HEADS
    xb = x_ref[...].astype(BF16)
    dot = functools.partial(jnp.dot, preferred_element_type=F32)
    ntdot = lambda a, b: lax.dot_general(a, b, _NT, preferred_element_type=F32)
    q_ref[...] = (dot(xb, wq_ref[...]) * (MLSTM_DQK ** -0.5)).astype(BF16)
    k_ref[...] = dot(xb, wk_ref[...]).astype(BF16)
    vt_ref[...] = ntdot(wvt_ref[...], xb).astype(BF16)
    ot_ref[...] = ntdot(wot_ref[...], xb)
    z = dot(xb, wg_ref[...]) + bg_ref[...]
    lane = lax.broadcasted_iota(I32, z.shape, 1)
    log_sig = jnp.minimum(z, 0.0) - jnp.log1p(jnp.exp(-jnp.abs(z)))
    g = jnp.where(lane < H, z, log_sig) * LOG2_E
    gc_ref[...] = g
    gs_ref[...] = pltpu.roll(g, LANES - H, 1)
    gr_ref[...] = g.T[:2 * H, :]


def _mlstm_cell_kernel(q_ref, k_ref, vt_ref, ot_ref, gc_ref, gs_ref, gr_ref, ng_ref, out_ref,
                       c_ref, m_ref, *, chunk, n_chunks):
    H, dk, dv = MLSTM_HEADS, MLSTM_DQK, MLSTM_DV
    reps = chunk // LANES
    aug = c_ref.shape[1] - dv

    @pl.when(pl.program_id(1) == 0)
    def _():
        c_ref[...] = jnp.zeros_like(c_ref)
        m_ref[...] = jnp.zeros_like(m_ref)

    s_idx = lax.broadcasted_iota(I32, (chunk, chunk), 0)
    j_idx = lax.broadcasted_iota(I32, (chunk, chunk), 1)
    visible = s_idx <= j_idx
    tri_upper = visible.astype(BF16)
    tri_lower = (j_idx <= s_idx).astype(BF16)
    ones_rows = (lax.broadcasted_iota(I32, (aug, chunk), 0) == 0).astype(BF16)
    dot = functools.partial(jnp.dot, preferred_element_type=F32)
    ntdot = lambda a, b: lax.dot_general(a, b, _NT, preferred_element_type=F32)

    def chunk_body(c, carry):
        rows = pl.ds(pl.multiple_of(c * chunk, chunk), chunk)
        gr = gr_ref[:, rows]
        b_cols = sum(dot(tri_lower, p) for p in _split3(gs_ref[rows, :]))
        b_rows = sum(dot(p, tri_upper) for p in _split3(gr))
        a_cols = gc_ref[rows, :] - b_cols
        hs = range(H)
        ig_r = [gr[h:h + 1, :] for h in hs]
        b_r = [b_rows[H + h:H + h + 1, :] for h in hs]
        b_last = [b[:, chunk - 1:chunk] for b in b_r]
        m_row = [jnp.concatenate([m_ref[h:h + 1, :]] * reps, axis=1) for h in hs]
        m_old = [m[:, 0:1] for m in m_row]
        qh = [q_ref[rows, h * dk:(h + 1) * dk] for h in hs]
        kh = [k_ref[rows, h * dk:(h + 1) * dk] for h in hs]
        vaug = [jnp.concatenate([vt_ref[h * dv:(h + 1) * dv, rows], ones_rows], axis=0)
                for h in hs]
        ct = [c_ref[h] for h in hs]

        qk = [ntdot(kh[h], qh[h]) for h in hs]
        qc = [ntdot(ct[h].astype(BF16), qh[h]) for h in hs]

        m_new = [jnp.maximum(b_last[h] + m_old[h],
                             jnp.max(b_last[h] - b_r[h] + ig_r[h], axis=1, keepdims=True)) for h in hs]
        for h in hs:
            w_row = jnp.exp2(b_last[h] - b_r[h] + ig_r[h] - m_new[h])
            vw = vaug[h] * w_row.astype(BF16)
            c_ref[h] = jnp.exp2(b_last[h] + m_old[h] - m_new[h]) * ct[h] + dot(vw, kh[h])
            m_ref[h:h + 1, :] = jnp.broadcast_to(m_new[h], (1, LANES))

        a_mat = [jnp.where(visible, a_cols[:, h:h + 1], -jnp.inf) for h in hs]
        g = [jnp.maximum(m_row[h], jnp.max(a_mat[h], axis=0, keepdims=True)) for h in hs]
        s = [qk[h] * jnp.exp2(a_mat[h] - g[h]) for h in hs]
        nd = [dot(vaug[h], s[h].astype(BF16)) + jnp.exp2(m_row[h] - g[h]) * qc[h]
              for h in hs]
        for h in hs:
            num, den = nd[h][:dv], nd[h][dv:dv + 1]
            r = 1.0 / jnp.maximum(jnp.abs(den), jnp.exp2(-(b_r[h] + g[h])))
            scale = r * lax.rsqrt(r * r * jnp.mean(num * num, axis=0, keepdims=True) + RMS_EPS)
            gain = jnp.concatenate([ng_ref[h * dv:(h + 1) * dv, :]] * reps, axis=1)
            out_t = jax.nn.sigmoid(ot_ref[h * dv:(h + 1) * dv, rows]) * (num * scale * gain)
            out_ref[rows, h * dv:(h + 1) * dv] = out_t.T.astype(out_ref.dtype)
        return carry

    lax.fori_loop(0, n_chunks, chunk_body, 0)


def _mlstm_mixer(x2d, batch, w_in, b_gates, norm_gain):
    T, D = x2d.shape
    H, dk, dv = MLSTM_HEADS, MLSTM_DQK, MLSTM_DV
    seq = T // batch
    tm = TOKEN_TILE
    cq, ck, cv, co = H * dk, 2 * H * dk, 2 * H * dk + H * dv, 2 * H * dk + 2 * H * dv
    wq = w_in[:, :cq].astype(BF16)
    wk = w_in[:, cq:ck].astype(BF16)
    wvt = w_in[:, ck:cv].T.astype(BF16)
    wot = w_in[:, cv:co].T.astype(BF16)
    wg = jnp.pad(w_in[:, co:], ((0, 0), (0, LANES - 2 * H))).astype(BF16)
    bg = jnp.pad(b_gates, (0, LANES - 2 * H)).reshape(1, LANES)

    full = lambda a: pl.BlockSpec(a.shape, lambda i: (0,) * a.ndim)
    tok = lambda w: pl.BlockSpec((tm, w), lambda i: (i, 0))
    tok_t = lambda h: pl.BlockSpec((h, tm), lambda i: (0, i))
    q, k, vt, ot, gc, gs, gr = pl.pallas_call(
        _mlstm_inproj_kernel,
        grid=(T // tm,),
        in_specs=[tok(D), full(wq), full(wk), full(wvt), full(wot), full(wg), full(bg)],
        out_specs=[tok(H * dk), tok(H * dk), tok_t(H * dv), tok_t(H * dv),
                   tok(LANES), tok(LANES), tok_t(2 * H)],
        out_shape=[jax.ShapeDtypeStruct((T, H * dk), BF16),
                   jax.ShapeDtypeStruct((T, H * dk), BF16),
                   jax.ShapeDtypeStruct((H * dv, T), BF16),
                   jax.ShapeDtypeStruct((H * dv, T), F32),
                   jax.ShapeDtypeStruct((T, LANES), F32),
                   jax.ShapeDtypeStruct((T, LANES), F32),
                   jax.ShapeDtypeStruct((2 * H, T), F32)],
        compiler_params=_params("parallel"),
        name="mlstm_inproj",
    )(x2d, wq, wk, wvt, wot, wg, bg)

    ts = min(MLSTM_SEQ_BLOCK, seq)
    chunk = min(MLSTM_CHUNK, ts)
    nsb = seq // ts
    ng = jnp.broadcast_to(norm_gain.reshape(H * dv, 1), (H * dv, LANES))
    seq_rows = lambda w: pl.BlockSpec((ts, w), lambda b, s: (b * nsb + s, 0))
    seq_lanes = lambda h: pl.BlockSpec((h, ts), lambda b, s: (0, b * nsb + s))
    return pl.pallas_call(
        functools.partial(_mlstm_cell_kernel, chunk=chunk, n_chunks=ts // chunk),
        grid=(batch, nsb),
        in_specs=[seq_rows(H * dk), seq_rows(H * dk), seq_lanes(H * dv), seq_lanes(H * dv),
                  seq_rows(LANES), seq_rows(LANES), seq_lanes(2 * H),
                  pl.BlockSpec((H * dv, LANES), lambda b, s: (0, 0))],
        out_specs=seq_rows(H * dv),
        out_shape=jax.ShapeDtypeStruct((T, H * dv), BF16),
        scratch_shapes=[pltpu.VMEM((H, dv + BF16_SUBLANES, dk), F32), pltpu.VMEM((H, LANES), F32)],
        compiler_params=_params("parallel", "arbitrary"),
        name="mlstm_cell",
    )(q, k, vt, ot, gc, gs, gr, ng)


def _mla_proj_kernel(x_ref, cos_ref, sin_ref, wcq_ref, wckv_ref, wkr_ref, qn_ref, kvn_ref,
                     wq_ref, wqr_ref, wkn_ref, wvt_ref, q_ref, k_ref, vt_ref):
    H, dn = MLA_HEADS, MLA_NOPE
    dot = functools.partial(jnp.dot, preferred_element_type=F32)
    xb = x_ref[...].astype(BF16)
    cos, sin = cos_ref[...], sin_ref[...]
    c_q = _rms_norm(dot(xb, wcq_ref[...]), qn_ref[...]).astype(BF16)
    c_kv = _rms_norm(dot(xb, wckv_ref[...]), kvn_ref[...]).astype(BF16)
    kr2 = dot(xb, wkr_ref[...])
    kr = (kr2[:, :LANES] * cos + kr2[:, LANES:] * sin).astype(BF16)
    scale = (MLA_NOPE + MLA_ROPE) ** -0.5 * LOG2_E
    qa = dot(c_q, wq_ref[...])
    qr = dot(c_q, wqr_ref[...])
    kn = dot(c_kv, wkn_ref[...])
    for h in range(H):
        base = h * 2 * LANES
        q_ref[h, :, :dn] = (qa[:, base:base + dn] * scale).astype(BF16)
        rope = qa[:, base + dn:base + 2 * LANES] * cos + qr[:, h * LANES:(h + 1) * LANES] * sin
        q_ref[h, :, dn:] = (rope * scale).astype(BF16)
        k_ref[h, :, :dn] = kn[:, h * dn:(h + 1) * dn].astype(BF16)
        k_ref[h, :, dn:] = kr
    vt_ref[...] = lax.dot_general(wvt_ref[...], c_kv, _NT, preferred_element_type=F32).astype(BF16)


def _attn_kernel(q_ref, k_ref, vt_ref, o_ref, bad_ref, m_ref, mu_ref, acc_ref,
                 *, tile, heads, qsub, lagged):
    dv = MLA_V
    qi = pl.program_id(2)
    m_ref[...] = jnp.full_like(m_ref, -jnp.inf)
    mu_ref[...] = jnp.full_like(mu_ref, -jnp.inf)
    acc_ref[...] = jnp.zeros_like(acc_ref)
    ones_rows = jnp.ones((acc_ref.shape[1] - dv, tile), BF16)

    def emit(tiles):
        units = [(j, ds, ex, g, u) for j, ds, ex in tiles for g in range(heads)
                 for u in range(qsub) if ds is None or u >= ds]
        keys_of = lambda j: pl.ds(pl.multiple_of(j * tile, tile), tile)
        sts = {}

        def score(i):
            j, ds, ex, g, u = units[i]
            st = lax.dot_general(k_ref[g, keys_of(j), :], q_ref[g, u * tile:(u + 1) * tile, :], _NT,
                                 preferred_element_type=F32)
            if u == ds:
                kpos = lax.broadcasted_iota(I32, st.shape, 0)
                qpos = lax.broadcasted_iota(I32, st.shape, 1)
                st = jnp.where(kpos <= qpos, st, -jnp.inf)
            sts[i] = st

        def softmax_value(i):
            j, ds, ex, g, u = units[i]
            c = g * qsub + u
            st = sts.pop(i)
            tile_max = jnp.max(st, axis=0, keepdims=True)
            seen = m_ref[c]
            if ex == "two_pass":
                stab = jnp.maximum(seen, tile_max)
            elif ex == "key0":
                stab = st[0:1, :]
            else:
                stab = seen
            p = jnp.exp2(st - stab).astype(BF16)
            vt = jnp.concatenate([vt_ref[g * dv:(g + 1) * dv, keys_of(j)], ones_rows], axis=0)
            acc_ref[c] = jnp.exp2(mu_ref[c] - stab) * acc_ref[c] + jnp.dot(
                vt, p, preferred_element_type=F32)
            mu_ref[c] = stab
            m_ref[c] = jnp.maximum(seen, tile_max)

        ahead = ATTN_SCORE_LOOKAHEAD
        for i in range(min(ahead, len(units))):
            score(i)
        for i in range(len(units)):
            if i + ahead < len(units):
                score(i + ahead)
            softmax_value(i)

    ex = "earlier" if lagged else "two_pass"
    ex0 = "key0" if lagged else "two_pass"

    @pl.when(qi > 0)
    def _():
        emit([(0, None, ex0)] + [(d, None, ex) for d in range(1, qsub)])

    def body(i, carry):
        emit([(qsub * i + d, None, ex) for d in range(qsub)])
        return carry

    lax.fori_loop(1, qi, body, 0)

    @pl.when(qi > 0)
    def _():
        emit([(qsub * qi + u, u, ex) for u in range(qsub)])

    @pl.when(qi == 0)
    def _():
        emit([(0, 0, ex0)] + [(u, u, ex) for u in range(1, qsub)])

    for g in range(heads):
        for u in range(qsub):
            c = g * qsub + u
            acc = acc_ref[c]
            o_ref[u * tile:(u + 1) * tile, g * dv:(g + 1) * dv] = (
                acc[:dv] * (1.0 / acc[dv:dv + 1])).T.astype(o_ref.dtype)
            bad_ref[0, c] = jnp.max(jnp.where(acc - acc == 0.0, 0.0, 1.0), axis=0, keepdims=True)


def _mla_mixer(x2d, batch, cos_t, sin_t, w_in, q_norm, kv_norm, w_qb, w_kvb):
    T, D = x2d.shape
    H, dn, dr, dv = MLA_HEADS, MLA_NOPE, MLA_ROPE, MLA_V
    seq = T // batch
    tm = TOKEN_TILE
    ql, kl = MLA_Q_LORA, MLA_KV_LORA
    half = dr // 2

    def rot(w):
        return jnp.concatenate([-w[..., half:], w[..., :half]], axis=-1)

    wcq = w_in[:, :ql].astype(BF16)
    wckv = w_in[:, ql:ql + kl].astype(BF16)
    wr = w_in[:, ql + kl:]
    zr = jnp.zeros((D, LANES - dr), F32)
    wkr = jnp.concatenate([wr, zr, rot(wr), zr], axis=1).astype(BF16)
    wq3 = w_qb.reshape(ql, H, dn + dr)
    zq = jnp.zeros((ql, H, LANES - dr), F32)
    wq = jnp.concatenate([wq3, zq], axis=2).reshape(ql, H * 2 * LANES).astype(BF16)
    wqr = jnp.concatenate([rot(wq3[:, :, dn:]), zq], axis=2).reshape(ql, H * LANES).astype(BF16)
    wkv3 = w_kvb.reshape(kl, H, dn + dv)
    wkn = wkv3[:, :, :dn].reshape(kl, H * dn).astype(BF16)
    wvt = wkv3[:, :, dn:].reshape(kl, H * dv).T.astype(BF16)
    qn = q_norm.reshape(1, ql)
    kvn = kv_norm.reshape(1, kl)

    full = lambda a: pl.BlockSpec(a.shape, lambda i: (0,) * a.ndim)
    q, k, vt = pl.pallas_call(
        _mla_proj_kernel,
        grid=(T // tm,),
        in_specs=[pl.BlockSpec((tm, D), lambda i: (i, 0)),
                  pl.BlockSpec((tm, LANES), lambda i: (i, 0)),
                  pl.BlockSpec((tm, LANES), lambda i: (i, 0)),
                  full(wcq), full(wckv), full(wkr), full(qn), full(kvn),
                  full(wq), full(wqr), full(wkn), full(wvt)],
        out_specs=[pl.BlockSpec((H, tm, 2 * LANES), lambda i: (0, i, 0)),
                   pl.BlockSpec((H, tm, 2 * LANES), lambda i: (0, i, 0)),
                   pl.BlockSpec((H * dv, tm), lambda i: (0, i))],
        out_shape=[jax.ShapeDtypeStruct((H, T, 2 * LANES), BF16),
                   jax.ShapeDtypeStruct((H, T, 2 * LANES), BF16),
                   jax.ShapeDtypeStruct((H * dv, T), BF16)],
        compiler_params=_params("parallel"),
        name="mla_proj",
    )(x2d, cos_t, sin_t, wcq, wckv, wkr, qn, kvn, wq, wqr, wkn, wvt)

    tile = min(ATTN_TILE, seq)
    hp = ATTN_HEADS_PER_STEP
    qsub = min(ATTN_QUERY_TILES_PER_STEP, seq // tile)
    tq = tile * qsub
    nq = seq // tq
    nhp = H // hp
    nc = hp * qsub

    def attention(lagged):
        return pl.pallas_call(
            functools.partial(_attn_kernel, tile=tile, heads=hp, qsub=qsub, lagged=lagged),
            grid=(batch, nhp, nq),
            in_specs=[pl.BlockSpec((hp, tq, 2 * LANES), lambda b, h, i: (h, b * nq + i, 0)),
                      pl.BlockSpec((hp, seq, 2 * LANES), lambda b, h, i: (h, b, 0)),
                      pl.BlockSpec((hp * dv, seq), lambda b, h, i: (h, b))],
            out_specs=[pl.BlockSpec((tq, hp * dv), lambda b, h, i: (b * nq + i, h)),
                       pl.BlockSpec((1, nc, 1, tile), lambda b, h, i: ((b * nhp + h) * nq + i, 0, 0, 0))],
            out_shape=[jax.ShapeDtypeStruct((T, H * dv), BF16),
                       jax.ShapeDtypeStruct((batch * nhp * nq, nc, 1, tile), F32)],
            scratch_shapes=[pltpu.VMEM((nc, 1, tile), F32), pltpu.VMEM((nc, 1, tile), F32),
                            pltpu.VMEM((nc, dv + BF16_SUBLANES, tile), F32)],
            compiler_params=_params("parallel", "parallel", "arbitrary"),
            name="mla_attention" if lagged else "mla_attention_exact",
        )(q, k, vt)

    out, not_finite = attention(lagged=True)
    return lax.cond(jnp.any(not_finite > 0.0), lambda: attention(lagged=False)[0], lambda: out)


def _post_kernel(h_ref, x_ref, w_ref, g_ref, b_ref, wr_ref, br_ref, tri_ref,
                 xo_ref, xp_ref, idx_ref, gate_ref, pos_ref, cnt_ref, run_ref):
    E = N_EXPERTS
    tm = x_ref.shape[0]

    @pl.when(pl.program_id(0) == 0)
    def _():
        run_ref[...] = jnp.zeros_like(run_ref)

    mix = jnp.dot(h_ref[...], w_ref[...], preferred_element_type=F32)
    x1 = _layer_norm(DEEPNORM_ALPHA * x_ref[...] + mix, g_ref[...], b_ref[...])
    xo_ref[...] = x1
    _store_packed(xp_ref, x1)

    xh = x1.astype(BF16)
    xl = (x1 - xh.astype(F32)).astype(BF16)
    ntdot = lambda a, b: lax.dot_general(a, b, _NT, preferred_element_type=F32)
    both = ntdot(wr_ref[...].reshape(2 * E, -1), xh)
    rest = both[:E] + both[E:] + ntdot(wr_ref[0], xl) + br_ref[...]

    e_iota = lax.broadcasted_iota(I32, (E, tm), 0)
    vals, sels = [], []
    for k in range(TOP_K):
        v = jnp.max(rest, axis=0, keepdims=True)
        ik = jnp.min(jnp.where(rest == v, e_iota, E), axis=0, keepdims=True)
        sel = e_iota == ik
        rest = jnp.where(sel, -jnp.inf, rest)
        vals.append(v)
        sels.append(sel)
        idx_ref[k:k + 1, :] = ik
    ex = [jnp.exp(v - vals[0]) for v in vals]
    inv = 1.0 / sum(ex)
    for k in range(TOP_K):
        gate_ref[k:k + 1, :] = ex[k] * inv

    chosen = functools.reduce(jnp.logical_or, sels)
    before = jnp.dot(chosen.astype(BF16), tri_ref[...], preferred_element_type=F32)
    run = run_ref[:, 0:1]
    rank = before + run
    for k in range(TOP_K):
        pos_ref[k:k + 1, :] = jnp.sum(jnp.where(sels[k], rank, 0.0), axis=0,
                                      keepdims=True).astype(I32)
    run_new = run + jnp.sum(chosen.astype(F32), axis=1, keepdims=True)
    run_ref[...] = jnp.broadcast_to(run_new, run_ref.shape)
    cnt_ref[...] = jnp.broadcast_to(run_new, cnt_ref.shape)


def _post_mixer(h, x2d, w_out, ln_g, ln_b, w_router, b_router):
    T, D = x2d.shape
    nch = D // (2 * SC_COLS)
    E = N_EXPERTS
    tm = TOKEN_TILE
    wo = w_out.astype(BF16)
    wrt = w_router.T
    wrh = wrt.astype(BF16)
    wr = jnp.stack([wrh, (wrt - wrh.astype(F32)).astype(BF16)])
    full = lambda a: pl.BlockSpec(a.shape, lambda i: (0,) * a.ndim)
    g, b, br = ln_g.reshape(1, D), ln_b.reshape(1, D), b_router.reshape(E, 1)
    pos_ids = jnp.arange(tm, dtype=I32)
    tri = (pos_ids[:, None] < pos_ids[None, :]).astype(BF16)
    rows = TOP_K
    return pl.pallas_call(
        _post_kernel,
        grid=(T // tm,),
        in_specs=[pl.BlockSpec((tm, h.shape[1]), lambda i: (i, 0)),
                  pl.BlockSpec((tm, D), lambda i: (i, 0)),
                  full(wo), full(g), full(b), full(wr), full(br), full(tri)],
        out_specs=[pl.BlockSpec((tm, D), lambda i: (i, 0)),
                   pl.BlockSpec((nch, tm, SC_COLS), lambda i: (0, i, 0)),
                   pl.BlockSpec((rows, tm), lambda i: (0, i)),
                   pl.BlockSpec((rows, tm), lambda i: (0, i)),
                   pl.BlockSpec((rows, tm), lambda i: (0, i)),
                   pl.BlockSpec((E, LANES), lambda i: (0, 0))],
        out_shape=[jax.ShapeDtypeStruct((T, D), F32),
                   jax.ShapeDtypeStruct((nch, T, SC_COLS), U32),
                   jax.ShapeDtypeStruct((rows, T), I32),
                   jax.ShapeDtypeStruct((rows, T), F32),
                   jax.ShapeDtypeStruct((rows, T), I32),
                   jax.ShapeDtypeStruct((E, LANES), F32)],
        scratch_shapes=[pltpu.VMEM((E, LANES), F32)],
        compiler_params=_params("arbitrary"),
        name="post_mixer_router",
    )(h, x2d, wo, g, b, wr, br, tri)


def _sc_mesh():
    return plsc.VectorSubcoreMesh(core_axis_name="core", subcore_axis_name="subcore")


def _sc_scatter_rows(xc, dest, n_rows):
    nch, T, C = xc.shape
    K = dest.shape[0]
    W = SC_WINDOW
    xs = xc.reshape(nch * T, C)
    nb = (nch * T) // W
    offs = (jnp.arange(nch, dtype=I32) * n_rows)[None, :, None]
    idx = (dest[:, None, :] + offs).reshape(K, nch * T)

    @functools.partial(pl.kernel, out_type=jax.ShapeDtypeStruct((nch * n_rows, C), xc.dtype),
                       mesh=_sc_mesh(), scratch_types=[], name="moe_dispatch_scatter")
    def scatter(x_hbm, i_hbm, o_hbm):
        def body(x_vmem, i_vmem):
            for k in range(K):
                pltpu.sync_copy(x_vmem, o_hbm.at[i_vmem.at[k]])

        pltpu.emit_pipeline(
            body,
            grid=(nb,),
            in_specs=[pl.BlockSpec((W, C), lambda g: (g, 0)),
                      pl.BlockSpec((K, W), lambda g: (0, g))],
            out_specs=[],
            core_axis_name=("core", "subcore"),
            dimension_semantics=(pltpu.PARALLEL,),
        )(x_hbm, i_hbm)

    return scatter(xs, idx).reshape(nch, n_rows, C)


def _sc_gather_rows(yc, dest):
    nch, n_rows, C = yc.shape
    K, T = dest.shape
    W = SC_WINDOW
    ys = yc.reshape(nch * n_rows, C)
    offs = (jnp.arange(nch, dtype=I32) * n_rows)[:, None, None]
    idx = (dest[None, :, :] + offs).reshape(1, nch * K * T)
    n_sub = nch * K * T

    @functools.partial(pl.kernel, out_type=jax.ShapeDtypeStruct((n_sub, C), yc.dtype),
                       mesh=_sc_mesh(), scratch_types=[], name="moe_combine_gather")
    def gather(t_hbm, i_hbm, o_hbm):
        def body(i_vmem, o_vmem):
            pltpu.sync_copy(t_hbm.at[i_vmem.at[0]], o_vmem)

        pltpu.emit_pipeline(
            body,
            grid=(n_sub // W,),
            in_specs=[pl.BlockSpec((1, W), lambda g: (0, g))],
            out_specs=[pl.BlockSpec((W, C), lambda g: (g, 0))],
            core_axis_name=("core", "subcore"),
            dimension_semantics=(pltpu.PARALLEL,),
        )(i_hbm, o_hbm)

    return gather(ys, idx).reshape(nch, K, T, C)


def _expert_kernel(te_ref, nu_ref, x_ref, wgu_ref, bg_ref, bl_ref, wd_ref, bd_ref, perm_ref,
                   y_ref, wg_s, wl_s, wd_s):
    i = pl.program_id(0)
    active = i < nu_ref[0]
    e = te_ref[i]
    changed = jnp.logical_or(i == 0, e != te_ref[jnp.maximum(i - 1, 0)])
    dot = functools.partial(jnp.dot, preferred_element_type=F32)

    @pl.when(jnp.logical_and(active, changed))
    def _():
        n_blocks = wgu_ref.shape[3] // MXU_DIM
        for blk in range(n_blocks):
            wb = wgu_ref[0, 0, :, blk * MXU_DIM:(blk + 1) * MXU_DIM].astype(BF16)
            wp = dot(wb, perm_ref[...]).astype(BF16)
            wg_s[:, blk * LANES:(blk + 1) * LANES] = wp[:, :LANES]
            wl_s[:, blk * LANES:(blk + 1) * LANES] = wp[:, LANES:]
        wd_s[...] = wd_ref[0, 0].astype(BF16)

    @pl.when(active)
    def _():
        xb = _load_packed(x_ref).astype(BF16)
        g = jnp.minimum(dot(xb, wg_s[...]) + bg_ref[0], SWIGLU_LIMIT)
        lin = jnp.clip(dot(xb, wl_s[...]) + bl_ref[0], -SWIGLU_LIMIT, SWIGLU_LIMIT)
        act = (lin + 1.0) * g * jax.nn.sigmoid(SWIGLU_ALPHA * g)
        _store_packed(y_ref, dot(act.astype(BF16), wd_s[...]) + bd_ref[0])


def _combine_kernel(y0_ref, y1_ref, y2_ref, y3_ref, gate_ref, x_ref, g_ref, b_ref, o_ref):
    gate = gate_ref[...]
    ff = (gate[:, 0:1] * _load_packed(y0_ref) + gate[:, 1:2] * _load_packed(y1_ref)
          + gate[:, 2:3] * _load_packed(y2_ref) + gate[:, 3:4] * _load_packed(y3_ref))
    o_ref[...] = _layer_norm(DEEPNORM_ALPHA * x_ref[...] + ff, g_ref[...], b_ref[...])


def _moe_dispatch(x1p, idx, pos, cnt):
    nch, T, C = x1p.shape
    E, K = N_EXPERTS, TOP_K
    te = EXPERT_TILE
    n_tiles = (T * K) // te + E
    n_rows = n_tiles * te

    counts = cnt[:, 0].astype(I32)
    padded = (counts + te - 1) // te * te
    pad_end = jnp.cumsum(padded)
    pad_start = pad_end - padded
    experts = jnp.arange(E, dtype=I32)
    start_of = jnp.sum(jnp.where(idx[None] == experts[:, None, None],
                                 pad_start[:, None, None], 0), axis=0)
    dest = start_of + pos
    tile_start = jnp.arange(n_tiles, dtype=I32) * te
    tile_expert = jnp.minimum(
        jnp.sum((pad_end[None, :] <= tile_start[:, None]).astype(I32), axis=1), E - 1)
    n_used = (pad_end[-1] // te).astype(I32).reshape(1)

    x_rows = _sc_scatter_rows(x1p, dest, n_rows)
    return x_rows, dest, tile_expert, n_used


def _moe_experts(x_rows, tile_expert, n_used, layer, w_gu, b_gu, w_down, b_down):
    nch, n_rows, C = x_rows.shape
    E = N_EXPERTS
    D, dff = w_down.shape[3], w_down.shape[2]
    te = EXPERT_TILE
    n_tiles = n_rows // te
    half = MXU_DIM // 2
    src = jnp.arange(MXU_DIM)
    perm = (src[:, None] == jnp.where(src < half, 2 * src, 2 * (src - half) + 1)[None, :])
    perm = perm.astype(BF16)
    bg = b_gu[:, 0::2].reshape(E, 1, dff)
    bl = b_gu[:, 1::2].reshape(E, 1, dff)
    bd = b_down.reshape(E, 1, D)

    def row_map(i, te_ref, nu_ref):
        return (0, jnp.minimum(i, nu_ref[0] - 1), 0)

    def exp_map(i, te_ref, nu_ref):
        return (te_ref[i], 0, 0)

    def stacked_map(i, te_ref, nu_ref):
        return (layer, te_ref[i], 0, 0)

    return pl.pallas_call(
        _expert_kernel,
        grid_spec=pltpu.PrefetchScalarGridSpec(
            num_scalar_prefetch=2,
            grid=(n_tiles,),
            in_specs=[pl.BlockSpec((nch, te, C), row_map),
                      pl.BlockSpec((1, 1, D, 2 * dff), stacked_map),
                      pl.BlockSpec((1, 1, dff), exp_map),
                      pl.BlockSpec((1, 1, dff), exp_map),
                      pl.BlockSpec((1, 1, dff, D), stacked_map),
                      pl.BlockSpec((1, 1, D), exp_map),
                      pl.BlockSpec((MXU_DIM, MXU_DIM), lambda i, a, b: (0, 0))],
            out_specs=pl.BlockSpec((nch, te, C), row_map),
            scratch_shapes=[pltpu.VMEM((D, dff), BF16), pltpu.VMEM((D, dff), BF16),
                            pltpu.VMEM((dff, D), BF16)],
        ),
        out_shape=jax.ShapeDtypeStruct((nch, n_rows, C), U32),
        compiler_params=_params("arbitrary"),
        name="moe_experts",
    )(tile_expert, n_used, x_rows, w_gu, bg, bl, w_down, bd, perm)


def _moe_combine(y_tok, gate, x1, ln_g, ln_b):
    nch, K, T, C = y_tok.shape
    D = x1.shape[1]
    tm = TOKEN_TILE
    gate_col = gate.T
    slot = lambda k: pl.BlockSpec((nch, None, tm, C), lambda i, k=k: (0, k, i, 0))
    vec = pl.BlockSpec((1, D), lambda i: (0, 0))
    return pl.pallas_call(
        _combine_kernel,
        grid=(T // tm,),
        in_specs=[slot(0), slot(1), slot(2), slot(3),
                  pl.BlockSpec((tm, K), lambda i: (i, 0)),
                  pl.BlockSpec((tm, D), lambda i: (i, 0)), vec, vec],
        out_specs=pl.BlockSpec((tm, D), lambda i: (i, 0)),
        out_shape=jax.ShapeDtypeStruct((T, D), F32),
        compiler_params=_params("parallel"),
        name="moe_combine_norm",
    )(y_tok, y_tok, y_tok, y_tok, gate_col, x1, ln_g.reshape(1, D), ln_b.reshape(1, D))


def kernel(x, positions, ln_gain, ln_bias, mlstm_w_in, mlstm_b_gates, mlstm_norm_gain,
           mlstm_w_out, mla_w_in, mla_q_norm, mla_kv_norm, mla_w_qb, mla_w_kvb, mla_w_out,
           moe_w_router, moe_b_router, moe_w_gate_up, moe_b_gate_up, moe_w_down, moe_b_down):
    B, S, D = x.shape
    T = B * S
    x2d = x.reshape(T, D)

    inv_freq = ROPE_THETA ** (-jnp.arange(0, MLA_ROPE, 2, dtype=F32) / MLA_ROPE)
    ang = positions.astype(F32).reshape(T, 1) * jnp.tile(inv_freq, LANES // inv_freq.shape[0])
    cos_t = jnp.cos(ang)
    sin_t = jnp.sin(ang)

    for layer in range(DEPTH):
        j = layer // 2
        if layer % 2 == 0:
            h = _mlstm_mixer(x2d, B, mlstm_w_in[j], mlstm_b_gates[j], mlstm_norm_gain[j])
            w_out = mlstm_w_out[j]
        else:
            h = _mla_mixer(x2d, B, cos_t, sin_t, mla_w_in[j], mla_q_norm[j], mla_kv_norm[j],
                           mla_w_qb[j], mla_w_kvb[j])
            w_out = mla_w_out[j]
        x1, x1p, idx, gate, pos, cnt = _post_mixer(h, x2d, w_out, ln_gain[layer, 0], ln_bias[layer, 0],
                                                   moe_w_router[layer], moe_b_router[layer])
        x_rows, dest, tile_expert, n_used = _moe_dispatch(x1p, idx, pos, cnt)
        y_rows = _moe_experts(x_rows, tile_expert, n_used, layer, moe_w_gate_up, moe_b_gate_up[layer],
                              moe_w_down, moe_b_down[layer])
        y_tok = _sc_gather_rows(y_rows, dest)
        x2d = _moe_combine(y_tok, gate, x1, ln_gain[layer, 1], ln_bias[layer, 1])
    return x2d.reshape(B, S, D)
```

```python
import functools

import jax
import jax.numpy as jnp
from jax import lax
from jax.experimental import pallas as pl
from jax.experimental.pallas import tpu as pltpu
from jax.experimental.pallas import tpu_sc as plsc

F32 = jnp.float32
BF16 = jnp.bfloat16
I32 = jnp.int32
U32 = jnp.uint32

DEPTH = 4
MLSTM_HEADS = 8
MLSTM_DQK = 64
MLSTM_DV = 128
MLA_HEADS = 8
MLA_Q_LORA = 384
MLA_KV_LORA = 256
MLA_NOPE = 128
MLA_ROPE = 64
MLA_V = 128
ROPE_THETA = 10000.0
N_EXPERTS = 32
TOP_K = 4
SWIGLU_LIMIT = 7.0
SWIGLU_ALPHA = 1.702
DEEPNORM_ALPHA = (2.0 * DEPTH) ** 0.25
LN_EPS = 1e-5
LOG2_E = 1.4426950408889634
RMS_EPS = 1e-6

LANES = 128
BF16_SUBLANES = 16
MXU_DIM = 256
VMEM_LIMIT_BYTES = 56 * 1024 * 1024

TOKEN_TILE = 1024
MLSTM_SEQ_BLOCK = 1024
MLSTM_CHUNK = 256
ATTN_TILE = 512
ATTN_HEADS_PER_STEP = 2
ATTN_QUERY_TILES_PER_STEP = 4
ATTN_SCORE_LOOKAHEAD = 2
EXPERT_TILE = 512
SC_WINDOW = 128
SC_COLS = 256

_NT = (((1,), (1,)), ((), ()))
HALF_WORD_BITS = 16
HIGH_HALF_MASK = 0xFFFF0000


def _params(*sem):
    return pltpu.CompilerParams(dimension_semantics=sem, vmem_limit_bytes=VMEM_LIMIT_BYTES)


def _layer_norm(z, g, b):
    mu = jnp.mean(z, axis=-1, keepdims=True)
    zc = z - mu
    var = jnp.mean(zc * zc, axis=-1, keepdims=True)
    return zc * lax.rsqrt(var + LN_EPS) * g + b


def _rms_norm(z, g):
    return z * lax.rsqrt(jnp.mean(z * z, axis=-1, keepdims=True) + RMS_EPS) * g


def _store_packed(ref, a):
    half = a.shape[1] // 2
    rounded = lambda v: lax.bitcast_convert_type(v.astype(BF16).astype(F32), U32)
    words = ((rounded(a[:, :half]) >> HALF_WORD_BITS)
             | (rounded(a[:, half:]) & jnp.uint32(HIGH_HALF_MASK)))
    for c in range(ref.shape[0]):
        ref[c] = words[:, c * SC_COLS:(c + 1) * SC_COLS]


def _load_packed(ref):
    chunks = [ref[c] for c in range(ref.shape[0])]
    lo = [lax.bitcast_convert_type(w << HALF_WORD_BITS, F32) for w in chunks]
    hi = [lax.bitcast_convert_type(w & jnp.uint32(HIGH_HALF_MASK), F32) for w in chunks]
    return jnp.concatenate(lo + hi, axis=1)


def _split3(a):
    hi = a.astype(BF16)
    r = a - hi.astype(F32)
    mid = r.astype(BF16)
    lo = (r - mid.astype(F32)).astype(BF16)
    return hi, mid, lo


def _mlstm_inproj_kernel(x_ref, wq_ref, wk_ref, wvt_ref, wot_ref, wg_ref, bg_ref,
                         q_ref, k_ref, vt_ref, ot_ref, gc_ref, gs_ref, gr_ref):
    H = MLSTM_HEADS
    xb = x_ref[...].astype(BF16)
    dot = functools.partial(jnp.dot, preferred_element_type=F32)
    ntdot = lambda a, b: lax.dot_general(a, b, _NT, preferred_element_type=F32)
    q_ref[...] = (dot(xb, wq_ref[...]) * (MLSTM_DQK ** -0.5)).astype(BF16)
    k_ref[...] = dot(xb, wk_ref[...]).astype(BF16)
    vt_ref[...] = ntdot(wvt_ref[...], xb).astype(BF16)
    ot_ref[...] = ntdot(wot_ref[...], xb)
    z = dot(xb, wg_ref[...]) + bg_ref[...]
    lane = lax.broadcasted_iota(I32, z.shape, 1)
    log_sig = jnp.minimum(z, 0.0) - jnp.log1p(jnp.exp(-jnp.abs(z)))
    g = jnp.where(lane < H, z, log_sig) * LOG2_E
    gc_ref[...] = g
    gs_ref[...] = pltpu.roll(g, LANES - H, 1)
    gr_ref[...] = g.T[:2 * H, :]


def _mlstm_cell_kernel(q_ref, k_ref, vt_ref, ot_ref, gc_ref, gs_ref, gr_ref, ng_ref, out_ref,
                       c_ref, m_ref, *, chunk, n_chunks):
    H, dk, dv = MLSTM_HEADS, MLSTM_DQK, MLSTM_DV
    reps = chunk // LANES
    aug = c_ref.shape[1] - dv

    @pl.when(pl.program_id(1) == 0)
    def _():
        c_ref[...] = jnp.zeros_like(c_ref)
        m_ref[...] = jnp.zeros_like(m_ref)

    s_idx = lax.broadcasted_iota(I32, (chunk, chunk), 0)
    j_idx = lax.broadcasted_iota(I32, (chunk, chunk), 1)
    visible = s_idx <= j_idx
    tri_upper = visible.astype(BF16)
    tri_lower = (j_idx <= s_idx).astype(BF16)
    ones_rows = (lax.broadcasted_iota(I32, (aug, chunk), 0) == 0).astype(BF16)
    dot = functools.partial(jnp.dot, preferred_element_type=F32)
    ntdot = lambda a, b: lax.dot_general(a, b, _NT, preferred_element_type=F32)

    def chunk_body(c, carry):
        rows = pl.ds(pl.multiple_of(c * chunk, chunk), chunk)
        gr = gr_ref[:, rows]
        b_cols = sum(dot(tri_lower, p) for p in _split3(gs_ref[rows, :]))
        b_rows = sum(dot(p, tri_upper) for p in _split3(gr))
        a_cols = gc_ref[rows, :] - b_cols
        hs = range(H)
        ig_r = [gr[h:h + 1, :] for h in hs]
        b_r = [b_rows[H + h:H + h + 1, :] for h in hs]
        b_last = [b[:, chunk - 1:chunk] for b in b_r]
        m_row = [jnp.concatenate([m_ref[h:h + 1, :]] * reps, axis=1) for h in hs]
        m_old = [m[:, 0:1] for m in m_row]
        qh = [q_ref[rows, h * dk:(h + 1) * dk] for h in hs]
        kh = [k_ref[rows, h * dk:(h + 1) * dk] for h in hs]
        vaug = [jnp.concatenate([vt_ref[h * dv:(h + 1) * dv, rows], ones_rows], axis=0)
                for h in hs]
        ct = [c_ref[h] for h in hs]

        qk = [ntdot(kh[h], qh[h]) for h in hs]
        qc = [ntdot(ct[h].astype(BF16), qh[h]) for h in hs]

        m_new = [jnp.maximum(b_last[h] + m_old[h],
                             jnp.max(b_last[h] - b_r[h] + ig_r[h], axis=1, keepdims=True)) for h in hs]
        for h in hs:
            w_row = jnp.exp2(b_last[h] - b_r[h] + ig_r[h] - m_new[h])
            vw = vaug[h] * w_row.astype(BF16)
            c_ref[h] = jnp.exp2(b_last[h] + m_old[h] - m_new[h]) * ct[h] + dot(vw, kh[h])
            m_ref[h:h + 1, :] = jnp.broadcast_to(m_new[h], (1, LANES))

        a_mat = [jnp.where(visible, a_cols[:, h:h + 1], -jnp.inf) for h in hs]
        g = [jnp.maximum(m_row[h], jnp.max(a_mat[h], axis=0, keepdims=True)) for h in hs]
        s = [qk[h] * jnp.exp2(a_mat[h] - g[h]) for h in hs]
        nd = [dot(vaug[h], s[h].astype(BF16)) + jnp.exp2(m_row[h] - g[h]) * qc[h]
              for h in hs]
        for h in hs:
            num, den = nd[h][:dv], nd[h][dv:dv + 1]
            r = 1.0 / jnp.maximum(jnp.abs(den), jnp.exp2(-(b_r[h] + g[h])))
            scale = r * lax.rsqrt(r * r * jnp.mean(num * num, axis=0, keepdims=True) + RMS_EPS)
            gain = jnp.concatenate([ng_ref[h * dv:(h + 1) * dv, :]] * reps, axis=1)
            out_t = jax.nn.sigmoid(ot_ref[h * dv:(h + 1) * dv, rows]) * (num * scale * gain)
            out_ref[rows, h * dv:(h + 1) * dv] = out_t.T.astype(out_ref.dtype)
        return carry

    lax.fori_loop(0, n_chunks, chunk_body, 0)


def _mlstm_mixer(x2d, batch, w_in, b_gates, norm_gain):
    T, D = x2d.shape
    H, dk, dv = MLSTM_HEADS, MLSTM_DQK, MLSTM_DV
    seq = T // batch
    tm = TOKEN_TILE
    cq, ck, cv, co = H * dk, 2 * H * dk, 2 * H * dk + H * dv, 2 * H * dk + 2 * H * dv
    wq = w_in[:, :cq].astype(BF16)
    wk = w_in[:, cq:ck].astype(BF16)
    wvt = w_in[:, ck:cv].T.astype(BF16)
    wot = w_in[:, cv:co].T.astype(BF16)
    wg = jnp.pad(w_in[:, co:], ((0, 0), (0, LANES - 2 * H))).astype(BF16)
    bg = jnp.pad(b_gates, (0, LANES - 2 * H)).reshape(1, LANES)

    full = lambda a: pl.BlockSpec(a.shape, lambda i: (0,) * a.ndim)
    tok = lambda w: pl.BlockSpec((tm, w), lambda i: (i, 0))
    tok_t = lambda h: pl.BlockSpec((h, tm), lambda i: (0, i))
    q, k, vt, ot, gc, gs, gr = pl.pallas_call(
        _mlstm_inproj_kernel,
        grid=(T // tm,),
        in_specs=[tok(D), full(wq), full(wk), full(wvt), full(wot), full(wg), full(bg)],
        out_specs=[tok(H * dk), tok(H * dk), tok_t(H * dv), tok_t(H * dv),
                   tok(LANES), tok(LANES), tok_t(2 * H)],
        out_shape=[jax.ShapeDtypeStruct((T, H * dk), BF16),
                   jax.ShapeDtypeStruct((T, H * dk), BF16),
                   jax.ShapeDtypeStruct((H * dv, T), BF16),
                   jax.ShapeDtypeStruct((H * dv, T), F32),
                   jax.ShapeDtypeStruct((T, LANES), F32),
                   jax.ShapeDtypeStruct((T, LANES), F32),
                   jax.ShapeDtypeStruct((2 * H, T), F32)],
        compiler_params=_params("parallel"),
        name="mlstm_inproj",
    )(x2d, wq, wk, wvt, wot, wg, bg)

    ts = min(MLSTM_SEQ_BLOCK, seq)
    chunk = min(MLSTM_CHUNK, ts)
    nsb = seq // ts
    ng = jnp.broadcast_to(norm_gain.reshape(H * dv, 1), (H * dv, LANES))
    seq_rows = lambda w: pl.BlockSpec((ts, w), lambda b, s: (b * nsb + s, 0))
    seq_lanes = lambda h: pl.BlockSpec((h, ts), lambda b, s: (0, b * nsb + s))
    return pl.pallas_call(
        functools.partial(_mlstm_cell_kernel, chunk=chunk, n_chunks=ts // chunk),
        grid=(batch, nsb),
        in_specs=[seq_rows(H * dk), seq_rows(H * dk), seq_lanes(H * dv), seq_lanes(H * dv),
                  seq_rows(LANES), seq_rows(LANES), seq_lanes(2 * H),
                  pl.BlockSpec((H * dv, LANES), lambda b, s: (0, 0))],
        out_specs=seq_rows(H * dv),
        out_shape=jax.ShapeDtypeStruct((T, H * dv), BF16),
        scratch_shapes=[pltpu.VMEM((H, dv + BF16_SUBLANES, dk), F32), pltpu.VMEM((H, LANES), F32)],
        compiler_params=_params("parallel", "arbitrary"),
        name="mlstm_cell",
    )(q, k, vt, ot, gc, gs, gr, ng)


def _mla_proj_kernel(x_ref, cos_ref, sin_ref, wcq_ref, wckv_ref, wkr_ref, qn_ref, kvn_ref,
                     wq_ref, wqr_ref, wkn_ref, wvt_ref, q_ref, k_ref, vt_ref):
    H, dn = MLA_HEADS, MLA_NOPE
    dot = functools.partial(jnp.dot, preferred_element_type=F32)
    xb = x_ref[...].astype(BF16)
    cos, sin = cos_ref[...], sin_ref[...]
    c_q = _rms_norm(dot(xb, wcq_ref[...]), qn_ref[...]).astype(BF16)
    c_kv = _rms_norm(dot(xb, wckv_ref[...]), kvn_ref[...]).astype(BF16)
    kr2 = dot(xb, wkr_ref[...])
    kr = (kr2[:, :LANES] * cos + kr2[:, LANES:] * sin).astype(BF16)
    scale = (MLA_NOPE + MLA_ROPE) ** -0.5 * LOG2_E
    qa = dot(c_q, wq_ref[...])
    qr = dot(c_q, wqr_ref[...])
    kn = dot(c_kv, wkn_ref[...])
    for h in range(H):
        base = h * 2 * LANES
        q_ref[h, :, :dn] = (qa[:, base:base + dn] * scale).astype(BF16)
        rope = qa[:, base + dn:base + 2 * LANES] * cos + qr[:, h * LANES:(h + 1) * LANES] * sin
        q_ref[h, :, dn:] = (rope * scale).astype(BF16)
        k_ref[h, :, :dn] = kn[:, h * dn:(h + 1) * dn].astype(BF16)
        k_ref[h, :, dn:] = kr
    vt_ref[...] = lax.dot_general(wvt_ref[...], c_kv, _NT, preferred_element_type=F32).astype(BF16)


def _attn_kernel(q_ref, k_ref, vt_ref, o_ref, bad_ref, m_ref, mu_ref, acc_ref,
                 *, tile, heads, qsub, lagged):
    dv = MLA_V
    qi = pl.program_id(2)
    m_ref[...] = jnp.full_like(m_ref, -jnp.inf)
    mu_ref[...] = jnp.full_like(mu_ref, -jnp.inf)
    acc_ref[...] = jnp.zeros_like(acc_ref)
    ones_rows = jnp.ones((acc_ref.shape[1] - dv, tile), BF16)

    def emit(tiles):
        units = [(j, ds, ex, g, u) for j, ds, ex in tiles for g in range(heads)
                 for u in range(qsub) if ds is None or u >= ds]
        keys_of = lambda j: pl.ds(pl.multiple_of(j * tile, tile), tile)
        sts = {}

        def score(i):
            j, ds, ex, g, u = units[i]
            st = lax.dot_general(k_ref[g, keys_of(j), :], q_ref[g, u * tile:(u + 1) * tile, :], _NT,
                                 preferred_element_type=F32)
            if u == ds:
                kpos = lax.broadcasted_iota(I32, st.shape, 0)
                qpos = lax.broadcasted_iota(I32, st.shape, 1)
                st = jnp.where(kpos <= qpos, st, -jnp.inf)
            sts[i] = st

        def softmax_value(i):
            j, ds, ex, g, u = units[i]
            c = g * qsub + u
            st = sts.pop(i)
            tile_max = jnp.max(st, axis=0, keepdims=True)
            seen = m_ref[c]
            if ex == "two_pass":
                stab = jnp.maximum(seen, tile_max)
            elif ex == "key0":
                stab = st[0:1, :]
            else:
                stab = seen
            p = jnp.exp2(st - stab).astype(BF16)
            vt = jnp.concatenate([vt_ref[g * dv:(g + 1) * dv, keys_of(j)], ones_rows], axis=0)
            acc_ref[c] = jnp.exp2(mu_ref[c] - stab) * acc_ref[c] + jnp.dot(
                vt, p, preferred_element_type=F32)
            mu_ref[c] = stab
            m_ref[c] = jnp.maximum(seen, tile_max)

        ahead = ATTN_SCORE_LOOKAHEAD
        for i in range(min(ahead, len(units))):
            score(i)
        for i in range(len(units)):
            if i + ahead < len(units):
                score(i + ahead)
            softmax_value(i)

    ex = "earlier" if lagged else "two_pass"
    ex0 = "key0" if lagged else "two_pass"

    @pl.when(qi > 0)
    def _():
        emit([(0, None, ex0)] + [(d, None, ex) for d in range(1, qsub)])

    def body(i, carry):
        emit([(qsub * i + d, None, ex) for d in range(qsub)])
        return carry

    lax.fori_loop(1, qi, body, 0)

    @pl.when(qi > 0)
    def _():
        emit([(qsub * qi + u, u, ex) for u in range(qsub)])

    @pl.when(qi == 0)
    def _():
        emit([(0, 0, ex0)] + [(u, u, ex) for u in range(1, qsub)])

    for g in range(heads):
        for u in range(qsub):
            c = g * qsub + u
            acc = acc_ref[c]
            o_ref[u * tile:(u + 1) * tile, g * dv:(g + 1) * dv] = (
                acc[:dv] * (1.0 / acc[dv:dv + 1])).T.astype(o_ref.dtype)
            bad_ref[0, c] = jnp.max(jnp.where(acc - acc == 0.0, 0.0, 1.0), axis=0, keepdims=True)


def _mla_mixer(x2d, batch, cos_t, sin_t, w_in, q_norm, kv_norm, w_qb, w_kvb):
    T, D = x2d.shape
    H, dn, dr, dv = MLA_HEADS, MLA_NOPE, MLA_ROPE, MLA_V
    seq = T // batch
    tm = TOKEN_TILE
    ql, kl = MLA_Q_LORA, MLA_KV_LORA
    half = dr // 2

    def rot(w):
        return jnp.concatenate([-w[..., half:], w[..., :half]], axis=-1)

    wcq = w_in[:, :ql].astype(BF16)
    wckv = w_in[:, ql:ql + kl].astype(BF16)
    wr = w_in[:, ql + kl:]
    zr = jnp.zeros((D, LANES - dr), F32)
    wkr = jnp.concatenate([wr, zr, rot(wr), zr], axis=1).astype(BF16)
    wq3 = w_qb.reshape(ql, H, dn + dr)
    zq = jnp.zeros((ql, H, LANES - dr), F32)
    wq = jnp.concatenate([wq3, zq], axis=2).reshape(ql, H * 2 * LANES).astype(BF16)
    wqr = jnp.concatenate([rot(wq3[:, :, dn:]), zq], axis=2).reshape(ql, H * LANES).astype(BF16)
    wkv3 = w_kvb.reshape(kl, H, dn + dv)
    wkn = wkv3[:, :, :dn].reshape(kl, H * dn).astype(BF16)
    wvt = wkv3[:, :, dn:].reshape(kl, H * dv).T.astype(BF16)
    qn = q_norm.reshape(1, ql)
    kvn = kv_norm.reshape(1, kl)

    full = lambda a: pl.BlockSpec(a.shape, lambda i: (0,) * a.ndim)
    q, k, vt = pl.pallas_call(
        _mla_proj_kernel,
        grid=(T // tm,),
        in_specs=[pl.BlockSpec((tm, D), lambda i: (i, 0)),
                  pl.BlockSpec((tm, LANES), lambda i: (i, 0)),
                  pl.BlockSpec((tm, LANES), lambda i: (i, 0)),
                  full(wcq), full(wckv), full(wkr), full(qn), full(kvn),
                  full(wq), full(wqr), full(wkn), full(wvt)],
        out_specs=[pl.BlockSpec((H, tm, 2 * LANES), lambda i: (0, i, 0)),
                   pl.BlockSpec((H, tm, 2 * LANES), lambda i: (0, i, 0)),
                   pl.BlockSpec((H * dv, tm), lambda i: (0, i))],
        out_shape=[jax.ShapeDtypeStruct((H, T, 2 * LANES), BF16),
                   jax.ShapeDtypeStruct((H, T, 2 * LANES), BF16),
                   jax.ShapeDtypeStruct((H * dv, T), BF16)],
        compiler_params=_params("parallel"),
        name="mla_proj",
    )(x2d, cos_t, sin_t, wcq, wckv, wkr, qn, kvn, wq, wqr, wkn, wvt)

    tile = min(ATTN_TILE, seq)
    hp = ATTN_HEADS_PER_STEP
    qsub = min(ATTN_QUERY_TILES_PER_STEP, seq // tile)
    tq = tile * qsub
    nq = seq // tq
    nhp = H // hp
    nc = hp * qsub

    def attention(lagged):
        return pl.pallas_call(
            functools.partial(_attn_kernel, tile=tile, heads=hp, qsub=qsub, lagged=lagged),
            grid=(batch, nhp, nq),
            in_specs=[pl.BlockSpec((hp, tq, 2 * LANES), lambda b, h, i: (h, b * nq + i, 0)),
                      pl.BlockSpec((hp, seq, 2 * LANES), lambda b, h, i: (h, b, 0)),
                      pl.BlockSpec((hp * dv, seq), lambda b, h, i: (h, b))],
            out_specs=[pl.BlockSpec((tq, hp * dv), lambda b, h, i: (b * nq + i, h)),
                       pl.BlockSpec((1, nc, 1, tile), lambda b, h, i: ((b * nhp + h) * nq + i, 0, 0, 0))],
            out_shape=[jax.ShapeDtypeStruct((T, H * dv), BF16),
                       jax.ShapeDtypeStruct((batch * nhp * nq, nc, 1, tile), F32)],
            scratch_shapes=[pltpu.VMEM((nc, 1, tile), F32), pltpu.VMEM((nc, 1, tile), F32),
                            pltpu.VMEM((nc, dv + BF16_SUBLANES, tile), F32)],
            compiler_params=_params("parallel", "parallel", "arbitrary"),
            name="mla_attention" if lagged else "mla_attention_exact",
        )(q, k, vt)

    out, not_finite = attention(lagged=True)
    return lax.cond(jnp.any(not_finite > 0.0), lambda: attention(lagged=False)[0], lambda: out)


def _post_kernel(h_ref, x_ref, w_ref, g_ref, b_ref, wr_ref, br_ref, tri_ref,
                 xo_ref, xp_ref, idx_ref, gate_ref, pos_ref, cnt_ref, run_ref):
    E = N_EXPERTS
    tm = x_ref.shape[0]

    @pl.when(pl.program_id(0) == 0)
    def _():
        run_ref[...] = jnp.zeros_like(run_ref)

    mix = jnp.dot(h_ref[...], w_ref[...], preferred_element_type=F32)
    x1 = _layer_norm(DEEPNORM_ALPHA * x_ref[...] + mix, g_ref[...], b_ref[...])
    xo_ref[...] = x1
    _store_packed(xp_ref, x1)

    xh = x1.astype(BF16)
    xl = (x1 - xh.astype(F32)).astype(BF16)
    ntdot = lambda a, b: lax.dot_general(a, b, _NT, preferred_element_type=F32)
    both = ntdot(wr_ref[...].reshape(2 * E, -1), xh)
    rest = both[:E] + both[E:] + ntdot(wr_ref[0], xl) + br_ref[...]

    e_iota = lax.broadcasted_iota(I32, (E, tm), 0)
    vals, sels = [], []
    for k in range(TOP_K):
        v = jnp.max(rest, axis=0, keepdims=True)
        ik = jnp.min(jnp.where(rest == v, e_iota, E), axis=0, keepdims=True)
        sel = e_iota == ik
        rest = jnp.where(sel, -jnp.inf, rest)
        vals.append(v)
        sels.append(sel)
        idx_ref[k:k + 1, :] = ik
    ex = [jnp.exp(v - vals[0]) for v in vals]
    inv = 1.0 / sum(ex)
    for k in range(TOP_K):
        gate_ref[k:k + 1, :] = ex[k] * inv

    chosen = functools.reduce(jnp.logical_or, sels)
    before = jnp.dot(chosen.astype(BF16), tri_ref[...], preferred_element_type=F32)
    run = run_ref[:, 0:1]
    rank = before + run
    for k in range(TOP_K):
        pos_ref[k:k + 1, :] = jnp.sum(jnp.where(sels[k], rank, 0.0), axis=0,
                                      keepdims=True).astype(I32)
    run_new = run + jnp.sum(chosen.astype(F32), axis=1, keepdims=True)
    run_ref[...] = jnp.broadcast_to(run_new, run_ref.shape)
    cnt_ref[...] = jnp.broadcast_to(run_new, cnt_ref.shape)


def _post_mixer(h, x2d, w_out, ln_g, ln_b, w_router, b_router):
    T, D = x2d.shape
    nch = D // (2 * SC_COLS)
    E = N_EXPERTS
    tm = TOKEN_TILE
    wo = w_out.astype(BF16)
    wrt = w_router.T
    wrh = wrt.astype(BF16)
    wr = jnp.stack([wrh, (wrt - wrh.astype(F32)).astype(BF16)])
    full = lambda a: pl.BlockSpec(a.shape, lambda i: (0,) * a.ndim)
    g, b, br = ln_g.reshape(1, D), ln_b.reshape(1, D), b_router.reshape(E, 1)
    pos_ids = jnp.arange(tm, dtype=I32)
    tri = (pos_ids[:, None] < pos_ids[None, :]).astype(BF16)
    rows = TOP_K
    return pl.pallas_call(
        _post_kernel,
        grid=(T // tm,),
        in_specs=[pl.BlockSpec((tm, h.shape[1]), lambda i: (i, 0)),
                  pl.BlockSpec((tm, D), lambda i: (i, 0)),
                  full(wo), full(g), full(b), full(wr), full(br), full(tri)],
        out_specs=[pl.BlockSpec((tm, D), lambda i: (i, 0)),
                   pl.BlockSpec((nch, tm, SC_COLS), lambda i: (0, i, 0)),
                   pl.BlockSpec((rows, tm), lambda i: (0, i)),
                   pl.BlockSpec((rows, tm), lambda i: (0, i)),
                   pl.BlockSpec((rows, tm), lambda i: (0, i)),
                   pl.BlockSpec((E, LANES), lambda i: (0, 0))],
        out_shape=[jax.ShapeDtypeStruct((T, D), F32),
                   jax.ShapeDtypeStruct((nch, T, SC_COLS), U32),
                   jax.ShapeDtypeStruct((rows, T), I32),
                   jax.ShapeDtypeStruct((rows, T), F32),
                   jax.ShapeDtypeStruct((rows, T), I32),
                   jax.ShapeDtypeStruct((E, LANES), F32)],
        scratch_shapes=[pltpu.VMEM((E, LANES), F32)],
        compiler_params=_params("arbitrary"),
        name="post_mixer_router",
    )(h, x2d, wo, g, b, wr, br, tri)


def _sc_mesh():
    return plsc.VectorSubcoreMesh(core_axis_name="core", subcore_axis_name="subcore")


def _sc_scatter_rows(xc, dest, n_rows):
    nch, T, C = xc.shape
    K = dest.shape[0]
    W = SC_WINDOW
    xs = xc.reshape(nch * T, C)
    nb = (nch * T) // W
    offs = (jnp.arange(nch, dtype=I32) * n_rows)[None, :, None]
    idx = (dest[:, None, :] + offs).reshape(K, nch * T)

    @functools.partial(pl.kernel, out_type=jax.ShapeDtypeStruct((nch * n_rows, C), xc.dtype),
                       mesh=_sc_mesh(), scratch_types=[], name="moe_dispatch_scatter")
    def scatter(x_hbm, i_hbm, o_hbm):
        def body(x_vmem, i_vmem):
            for k in range(K):
                pltpu.sync_copy(x_vmem, o_hbm.at[i_vmem.at[k]])

        pltpu.emit_pipeline(
            body,
            grid=(nb,),
            in_specs=[pl.BlockSpec((W, C), lambda g: (g, 0)),
                      pl.BlockSpec((K, W), lambda g: (0, g))],
            out_specs=[],
            core_axis_name=("core", "subcore"),
            dimension_semantics=(pltpu.PARALLEL,),
        )(x_hbm, i_hbm)

    return scatter(xs, idx).reshape(nch, n_rows, C)


def _sc_gather_rows(yc, dest):
    nch, n_rows, C = yc.shape
    K, T = dest.shape
    W = SC_WINDOW
    ys = yc.reshape(nch * n_rows, C)
    offs = (jnp.arange(nch, dtype=I32) * n_rows)[:, None, None]
    idx = (dest[None, :, :] + offs).reshape(1, nch * K * T)
    n_sub = nch * K * T

    @functools.partial(pl.kernel, out_type=jax.ShapeDtypeStruct((n_sub, C), yc.dtype),
                       mesh=_sc_mesh(), scratch_types=[], name="moe_combine_gather")
    def gather(t_hbm, i_hbm, o_hbm):
        def body(i_vmem, o_vmem):
            pltpu.sync_copy(t_hbm.at[i_vmem.at[0]], o_vmem)

        pltpu.emit_pipeline(
            body,
            grid=(n_sub // W,),
            in_specs=[pl.BlockSpec((1, W), lambda g: (0, g))],
            out_specs=[pl.BlockSpec((W, C), lambda g: (g, 0))],
            core_axis_name=("core", "subcore"),
            dimension_semantics=(pltpu.PARALLEL,),
        )(i_hbm, o_hbm)

    return gather(ys, idx).reshape(nch, K, T, C)


def _expert_kernel(te_ref, nu_ref, x_ref, wgu_ref, bg_ref, bl_ref, wd_ref, bd_ref, perm_ref,
                   y_ref, wg_s, wl_s, wd_s):
    i = pl.program_id(0)
    active = i < nu_ref[0]
    e = te_ref[i]
    changed = jnp.logical_or(i == 0, e != te_ref[jnp.maximum(i - 1, 0)])
    dot = functools.partial(jnp.dot, preferred_element_type=F32)

    @pl.when(jnp.logical_and(active, changed))
    def _():
        n_blocks = wgu_ref.shape[3] // MXU_DIM
        for blk in range(n_blocks):
            wb = wgu_ref[0, 0, :, blk * MXU_DIM:(blk + 1) * MXU_DIM].astype(BF16)
            wp = dot(wb, perm_ref[...]).astype(BF16)
            wg_s[:, blk * LANES:(blk + 1) * LANES] = wp[:, :LANES]
            wl_s[:, blk * LANES:(blk + 1) * LANES] = wp[:, LANES:]
        wd_s[...] = wd_ref[0, 0].astype(BF16)

    @pl.when(active)
    def _():
        xb = _load_packed(x_ref).astype(BF16)
        g = jnp.minimum(dot(xb, wg_s[...]) + bg_ref[0], SWIGLU_LIMIT)
        lin = jnp.clip(dot(xb, wl_s[...]) + bl_ref[0], -SWIGLU_LIMIT, SWIGLU_LIMIT)
        act = (lin + 1.0) * g * jax.nn.sigmoid(SWIGLU_ALPHA * g)
        _store_packed(y_ref, dot(act.astype(BF16), wd_s[...]) + bd_ref[0])


def _combine_kernel(y0_ref, y1_ref, y2_ref, y3_ref, gate_ref, x_ref, g_ref, b_ref, o_ref):
    gate = gate_ref[...]
    ff = (gate[:, 0:1] * _load_packed(y0_ref) + gate[:, 1:2] * _load_packed(y1_ref)
          + gate[:, 2:3] * _load_packed(y2_ref) + gate[:, 3:4] * _load_packed(y3_ref))
    o_ref[...] = _layer_norm(DEEPNORM_ALPHA * x_ref[...] + ff, g_ref[...], b_ref[...])


def _moe_dispatch(x1p, idx, pos, cnt):
    nch, T, C = x1p.shape
    E, K = N_EXPERTS, TOP_K
    te = EXPERT_TILE
    n_tiles = (T * K) // te + E
    n_rows = n_tiles * te

    counts = cnt[:, 0].astype(I32)
    padded = (counts + te - 1) // te * te
    pad_end = jnp.cumsum(padded)
    pad_start = pad_end - padded
    experts = jnp.arange(E, dtype=I32)
    start_of = jnp.sum(jnp.where(idx[None] == experts[:, None, None],
                                 pad_start[:, None, None], 0), axis=0)
    dest = start_of + pos
    tile_start = jnp.arange(n_tiles, dtype=I32) * te
    tile_expert = jnp.minimum(
        jnp.sum((pad_end[None, :] <= tile_start[:, None]).astype(I32), axis=1), E - 1)
    n_used = (pad_end[-1] // te).astype(I32).reshape(1)

    x_rows = _sc_scatter_rows(x1p, dest, n_rows)
    return x_rows, dest, tile_expert, n_used


def _moe_experts(x_rows, tile_expert, n_used, layer, w_gu, b_gu, w_down, b_down):
    nch, n_rows, C = x_rows.shape
    E = N_EXPERTS
    D, dff = w_down.shape[3], w_down.shape[2]
    te = EXPERT_TILE
    n_tiles = n_rows // te
    half = MXU_DIM // 2
    src = jnp.arange(MXU_DIM)
    perm = (src[:, None] == jnp.where(src < half, 2 * src, 2 * (src - half) + 1)[None, :])
    perm = perm.astype(BF16)
    bg = b_gu[:, 0::2].reshape(E, 1, dff)
    bl = b_gu[:, 1::2].reshape(E, 1, dff)
    bd = b_down.reshape(E, 1, D)

    def row_map(i, te_ref, nu_ref):
        return (0, jnp.minimum(i, nu_ref[0] - 1), 0)

    def exp_map(i, te_ref, nu_ref):
        return (te_ref[i], 0, 0)

    def stacked_map(i, te_ref, nu_ref):
        return (layer, te_ref[i], 0, 0)

    return pl.pallas_call(
        _expert_kernel,
        grid_spec=pltpu.PrefetchScalarGridSpec(
            num_scalar_prefetch=2,
            grid=(n_tiles,),
            in_specs=[pl.BlockSpec((nch, te, C), row_map),
                      pl.BlockSpec((1, 1, D, 2 * dff), stacked_map),
                      pl.BlockSpec((1, 1, dff), exp_map),
                      pl.BlockSpec((1, 1, dff), exp_map),
                      pl.BlockSpec((1, 1, dff, D), stacked_map),
                      pl.BlockSpec((1, 1, D), exp_map),
                      pl.BlockSpec((MXU_DIM, MXU_DIM), lambda i, a, b: (0, 0))],
            out_specs=pl.BlockSpec((nch, te, C), row_map),
            scratch_shapes=[pltpu.VMEM((D, dff), BF16), pltpu.VMEM((D, dff), BF16),
                            pltpu.VMEM((dff, D), BF16)],
        ),
        out_shape=jax.ShapeDtypeStruct((nch, n_rows, C), U32),
        compiler_params=_params("arbitrary"),
        name="moe_experts",
    )(tile_expert, n_used, x_rows, w_gu, bg, bl, w_down, bd, perm)


def _moe_combine(y_tok, gate, x1, ln_g, ln_b):
    nch, K, T, C = y_tok.shape
    D = x1.shape[1]
    tm = TOKEN_TILE
    gate_col = gate.T
    slot = lambda k: pl.BlockSpec((nch, None, tm, C), lambda i, k=k: (0, k, i, 0))
    vec = pl.BlockSpec((1, D), lambda i: (0, 0))
    return pl.pallas_call(
        _combine_kernel,
        grid=(T // tm,),
        in_specs=[slot(0), slot(1), slot(2), slot(3),
                  pl.BlockSpec((tm, K), lambda i: (i, 0)),
                  pl.BlockSpec((tm, D), lambda i: (i, 0)), vec, vec],
        out_specs=pl.BlockSpec((tm, D), lambda i: (i, 0)),
        out_shape=jax.ShapeDtypeStruct((T, D), F32),
        compiler_params=_params("parallel"),
        name="moe_combine_norm",
    )(y_tok, y_tok, y_tok, y_tok, gate_col, x1, ln_g.reshape(1, D), ln_b.reshape(1, D))


def kernel(x, positions, ln_gain, ln_bias, mlstm_w_in, mlstm_b_gates, mlstm_norm_gain,
           mlstm_w_out, mla_w_in, mla_q_norm, mla_kv_norm, mla_w_qb, mla_w_kvb, mla_w_out,
           moe_w_router, moe_b_router, moe_w_gate_up, moe_b_gate_up, moe_w_down, moe_b_down):
    B, S, D = x.shape
    T = B * S
    x2d = x.reshape(T, D)

    inv_freq = ROPE_THETA ** (-jnp.arange(0, MLA_ROPE, 2, dtype=F32) / MLA_ROPE)
    ang = positions.astype(F32).reshape(T, 1) * jnp.tile(inv_freq, LANES // inv_freq.shape[0])
    cos_t = jnp.cos(ang)
    sin_t = jnp.sin(ang)

    for layer in range(DEPTH):
        j = layer // 2
        if layer % 2 == 0:
            h = _mlstm_mixer(x2d, B, mlstm_w_in[j], mlstm_b_gates[j], mlstm_norm_gain[j])
            w_out = mlstm_w_out[j]
        else:
            h = _mla_mixer(x2d, B, cos_t, sin_t, mla_w_in[j], mla_q_norm[j], mla_kv_norm[j],
                           mla_w_qb[j], mla_w_kvb[j])
            w_out = mla_w_out[j]
        x1, x1p, idx, gate, pos, cnt = _post_mixer(h, x2d, w_out, ln_gain[layer, 0], ln_bias[layer, 0],
                                                   moe_w_router[layer], moe_b_router[layer])
        x_rows, dest, tile_expert, n_used = _moe_dispatch(x1p, idx, pos, cnt)
        y_rows = _moe_experts(x_rows, tile_expert, n_used, layer, moe_w_gate_up, moe_b_gate_up[layer],
                              moe_w_down, moe_b_down[layer])
        y_tok = _sc_gather_rows(y_rows, dest)
        x2d = _moe_combine(y_tok, gate, x1, ln_gain[layer, 1], ln_bias[layer, 1])
    return x2d.reshape(B, S, D)
```
